```python
import jax, jax.numpy as jnp
from jax import lax
import numpy as np

D_MODEL = 2048
BATCH = 16
SEQ = 256
DEPTH = 2
DEC_BATCH = 8
DEC_SEQ = 4096
PAST_LEN = 256

GRID_W = 64
MLA_HEADS = 16
MLA_NOPE = 128
MLA_ROPE = 64
MLA_V = 128
KV_LORA = 512
ROPE_BASE = 10000.0
Q_BLOCK = 128
FNET_GROUPS = 4
FNET_GC = 512
FNET_WIDTH = FNET_GROUPS * FNET_GC
RET_HEADS = 8
RET_DK = 128
RET_DV = 256
RET_CHUNK = 128
D_FF = 5632
N_EXPERTS = 8
TOP_K = 2
D_FF_EXPERT = 7168
MOE_BLOCK = 128
N_BRANCH = 3
EPS = 1e-6

kernel_name = 'hybrid_mla_fnet_retention_diffusion_step'


def in_sizes():
    return (MLA_HEADS * (MLA_NOPE + MLA_ROPE),
            KV_LORA,
            MLA_ROPE,
            FNET_WIDTH,
            RET_HEADS * RET_DK,
            RET_HEADS * RET_DK,
            RET_HEADS * RET_DV,
            RET_HEADS * RET_DV,
            N_BRANCH * D_MODEL)


def split_points():
    pts, acc = [], 0
    for s in in_sizes()[:-1]:
        acc += s
        pts.append(acc)
    return pts


def rms_norm(x, g):
    xf = x.astype(jnp.float32)
    y = xf * lax.rsqrt(jnp.mean(xf * xf, axis=-1, keepdims=True) + EPS)
    return (y * g.astype(jnp.float32)).astype(x.dtype)


def adaln(cond, w, b):
    mod = jax.nn.silu(cond) @ w + b
    return jnp.split(mod[:, None, :], 6, axis=-1)


def axial_rope(n_tok, dtype):
    rows = n_tok // GRID_W
    r, cl = jnp.meshgrid(jnp.arange(rows, dtype=jnp.float32), jnp.arange(GRID_W, dtype=jnp.float32), indexing='ij')
    n_freq = MLA_ROPE // 4
    inv = ROPE_BASE ** (-jnp.arange(n_freq, dtype=jnp.float32) / n_freq)
    ang = jnp.concatenate([r.reshape(-1, 1) * inv, cl.reshape(-1, 1) * inv], axis=-1)
    return jnp.cos(ang).astype(dtype), jnp.sin(ang).astype(dtype)


def apply_rope(x, cos, sin):
    half = MLA_ROPE // 2
    x1, x2 = x[..., :half], x[..., half:]
    return jnp.concatenate([x1 * cos - x2 * sin, x2 * cos + x1 * sin], axis=-1)


def mla_expand(ckv, w_ukv_l):
    B, K, _ = ckv.shape
    kv = (ckv @ w_ukv_l).reshape(B, K, MLA_HEADS, MLA_NOPE + MLA_V)
    return kv[..., :MLA_NOPE], kv[..., MLA_NOPE:]


def mla_attend(q_nope, q_rope, k_nope, k_rope, v):
    B, L, H, _ = q_nope.shape
    nb = L // Q_BLOCK
    scale = (MLA_NOPE + MLA_ROPE) ** -0.5

    def blocks(a):
        return jnp.moveaxis(a.reshape(B, nb, Q_BLOCK, *a.shape[2:]), 1, 0)

    def one(qs):
        qn, qr = qs
        s = (jnp.einsum('bqhd,bkhd->bhqk', qn, k_nope, preferred_element_type=jnp.float32)
             + jnp.einsum('bqhr,bkr->bhqk', qr, k_rope, preferred_element_type=jnp.float32))
        p = jax.nn.softmax(s * scale, axis=-1).astype(v.dtype)
        return jnp.einsum('bhqk,bkhd->bqhd', p, v)

    o = lax.map(one, (blocks(q_nope), blocks(q_rope)))
    return jnp.moveaxis(o, 0, 1).reshape(B, L, H * MLA_V)


def fourier_mix(f):
    B, L, _ = f.shape
    fg = f.reshape(B, L, FNET_GROUPS, FNET_GC).astype(jnp.float32)
    return jnp.fft.fft2(fg, axes=(1, 3), norm='ortho').real.reshape(B, L, FNET_WIDTH).astype(f.dtype)


def retention_scan(q, k, v, log_g, s0):
    B, L, H, _ = q.shape
    nc, C = L // RET_CHUNK, RET_CHUNK

    def to_chunks(a):
        return a.reshape(B, nc, C, H, a.shape[-1]).transpose(1, 0, 3, 2, 4).astype(jnp.float32)

    idx = jnp.arange(C, dtype=jnp.float32)
    rel = idx[:, None] - idx[None, :]
    inner_decay = jnp.where(rel >= 0, jnp.exp(jnp.maximum(rel, 0.0)[None] * log_g[:, None, None]), 0.0)
    q_decay = jnp.exp((idx + 1.0)[None, :] * log_g[:, None])[None, :, :, None]
    k_decay = jnp.exp((C - 1.0 - idx)[None, :] * log_g[:, None])[None, :, :, None]
    chunk_decay = jnp.exp(C * log_g)[None, :, None, None]

    def step(S, inp):
        qb, kb, vb = inp
        inner = jnp.einsum('bhid,bhjd->bhij', qb, kb) * inner_decay[None]
        o = jnp.einsum('bhij,bhjv->bhiv', inner, vb) + jnp.einsum('bhid,bhdv->bhiv', qb, S) * q_decay
        S_new = S * chunk_decay + jnp.einsum('bhjd,bhjv->bhdv', kb * k_decay, vb)
        return S_new, o

    s_fin, o = lax.scan(step, s0, (to_chunks(q), to_chunks(k), to_chunks(v)))
    return o.transpose(1, 0, 3, 2, 4).reshape(B, L, H, v.shape[-1]), s_fin


def retention_bidir(q, k, v, log_g, s0):
    of, sf = retention_scan(q, k, v, log_g[0], s0[:, 0])
    ob, sb = retention_scan(q[:, ::-1], k[:, ::-1], v[:, ::-1], log_g[1], s0[:, 1])
    return of + ob[:, ::-1], jnp.stack([sf, sb], axis=1)


def head_group_norm(o, g):
    B, L, H, DV = o.shape
    mu = jnp.mean(o, axis=-1, keepdims=True)
    d = o - mu
    y = d * lax.rsqrt(jnp.mean(d * d, axis=-1, keepdims=True) + 1e-5)
    return y.reshape(B, L, H * DV) * g.astype(jnp.float32)


def mixer(h, w_in_l, kv_norm_l, w_ukv_l, decay_logit_l, gn_l, w_o_mla_l, w_o_fnet_l, w_o_ret_l, w_out_l,
          rope=None, ctx=None):
    B, L, _ = h.shape
    q, ckv_raw, kr, f, rq, rk, rv, rg, gates = jnp.split(h @ w_in_l, split_points(), axis=-1)
    q = q.reshape(B, L, MLA_HEADS, MLA_NOPE + MLA_ROPE)
    q_nope, q_rope = q[..., :MLA_NOPE], q[..., MLA_NOPE:]
    ckv = rms_norm(ckv_raw, kv_norm_l)
    if rope is None:
        ckv_keys, krope_keys = ckv, kr
        s0 = jnp.zeros((B, 2, RET_HEADS, RET_DK, RET_DV), jnp.float32)
    else:
        cos, sin = rope
        ckv_ctx, krope_ctx, s_ctx = ctx
        q_rope = apply_rope(q_rope, cos[:, None, :], sin[:, None, :])
        ckv_keys = jnp.concatenate([ckv, ckv_ctx.astype(ckv.dtype)], axis=1)
        krope_keys = jnp.concatenate([apply_rope(kr, cos, sin), krope_ctx.astype(kr.dtype)], axis=1)
        s0 = s_ctx.astype(jnp.float32)
    k_nope, v = mla_expand(ckv_keys, w_ukv_l)
    att = mla_attend(q_nope, q_rope, k_nope, krope_keys, v)
    four = fourier_mix(f)
    log_g = jax.nn.log_sigmoid(decay_logit_l.astype(jnp.float32))
    ret, s_fin = retention_bidir(rq.reshape(B, L, RET_HEADS, RET_DK),
                                 rk.reshape(B, L, RET_HEADS, RET_DK) * (RET_DK ** -0.5),
                                 rv.reshape(B, L, RET_HEADS, RET_DV), log_g, s0)
    ret = head_group_norm(ret, gn_l).astype(h.dtype) * jax.nn.silu(rg)
    g_att, g_four, g_ret = jnp.split(jax.nn.sigmoid(gates), N_BRANCH, axis=-1)
    merged = g_att * (att @ w_o_mla_l) + g_four * (four @ w_o_fnet_l) + g_ret * (ret @ w_o_ret_l)
    return merged @ w_out_l, ckv, kr, s_fin


def swiglu(h, wg, wu, wd):
    return (jax.nn.silu(h @ wg) * (h @ wu)) @ wd


def moe_swiglu(h, w_router, w_gate, w_up, w_down):
    B, L, D = h.shape
    xt = h.reshape(-1, D)
    N = xt.shape[0]
    logits = (xt @ w_router).astype(jnp.float32)
    top_v, top_e = lax.top_k(logits, TOP_K)
    wts = jax.nn.softmax(top_v, axis=-1)
    A = N * TOP_K
    e_flat = top_e.reshape(-1)
    tok_flat = jnp.repeat(jnp.arange(N, dtype=jnp.int32), TOP_K)
    order = jnp.argsort(e_flat)
    e_s, tok_s, w_s = e_flat[order], tok_flat[order], wts.reshape(-1)[order]
    counts = jnp.zeros((N_EXPERTS,), jnp.int32).at[e_flat].add(1)
    starts = jnp.cumsum(counts) - counts
    padded = (counts + MOE_BLOCK - 1) // MOE_BLOCK * MOE_BLOCK
    pad_ends = jnp.cumsum(padded)
    pad_starts = pad_ends - padded
    dest = pad_starts[e_s] + (jnp.arange(A, dtype=jnp.int32) - starts[e_s])
    n_blocks = (A + MOE_BLOCK - 1) // MOE_BLOCK + N_EXPERTS
    P = n_blocks * MOE_BLOCK
    tok_pad = jnp.zeros((P,), jnp.int32).at[dest].set(tok_s)
    w_pad = jnp.zeros((P,), xt.dtype).at[dest].set(w_s.astype(xt.dtype))
    blk_expert = jnp.minimum(
        jnp.searchsorted(pad_ends, jnp.arange(n_blocks, dtype=jnp.int32) * MOE_BLOCK, side='right'),
        N_EXPERTS - 1)

    def run(args):
        tb, e = args
        xb = xt[tb]
        return (jax.nn.silu(xb @ w_gate[e]) * (xb @ w_up[e])) @ w_down[e]

    y = lax.map(run, (tok_pad.reshape(n_blocks, MOE_BLOCK), blk_expert))
    out = jnp.zeros_like(xt).at[tok_pad].add(y.reshape(P, D) * w_pad[:, None])
    return out.reshape(B, L, D)


def setup_inputs(seed: int = 0) -> dict:
    key = jax.random.key(seed)
    k = jax.random.split(key, 32)
    f32 = jnp.float32
    n_in = sum(in_sizes())
    n_dense = (DEPTH + 1) // 2
    n_moe = DEPTH // 2

    def nrm(i, shape, scale):
        return jax.random.normal(k[i], shape, f32) * scale

    def gain(i, shape):
        return 1.0 + 0.02 * jax.random.normal(k[i], shape, f32)

    a = 5.0 + jnp.arange(RET_HEADS, dtype=f32)
    decay0 = jnp.log(2.0 ** a - 1.0)
    return {
        'x_prompt': nrm(0, (BATCH, SEQ, D_MODEL), 1.0),
        'x_sample': nrm(1, (DEC_BATCH, DEC_SEQ, D_MODEL), 1.0),
        'c': nrm(2, (DEC_BATCH, D_MODEL), 1.0),
        'cache_ckv': nrm(3, (DEC_BATCH, DEPTH, PAST_LEN, KV_LORA), 1.0),
        'cache_krope': nrm(4, (DEC_BATCH, DEPTH, PAST_LEN, MLA_ROPE), 1.0),
        'state_ret': nrm(5, (DEC_BATCH, DEPTH, 2, RET_HEADS, RET_DK, RET_DV), 1.0),
        'c_ctx': nrm(6, (D_MODEL,), 1.0),
        'w_ada': nrm(7, (DEPTH, D_MODEL, 6 * D_MODEL), 0.5 * D_MODEL ** -0.5),
        'b_ada': nrm(8, (DEPTH, 6 * D_MODEL), 0.01),
        'norm1_g': gain(9, (DEPTH, D_MODEL)),
        'w_in': nrm(10, (DEPTH, D_MODEL, n_in), D_MODEL ** -0.5),
        'kv_norm_g': gain(11, (DEPTH, KV_LORA)),
        'w_ukv': nrm(12, (DEPTH, KV_LORA, MLA_HEADS * (MLA_NOPE + MLA_V)), KV_LORA ** -0.5),
        'ret_decay_logit': decay0 + nrm(13, (DEPTH, 2, RET_HEADS), 0.01),
        'ret_gn_g': gain(14, (DEPTH, RET_HEADS * RET_DV)),
        'w_o_mla': nrm(15, (DEPTH, MLA_HEADS * MLA_V, D_MODEL), (MLA_HEADS * MLA_V) ** -0.5),
        'w_o_fnet': nrm(16, (DEPTH, FNET_WIDTH, D_MODEL), FNET_WIDTH ** -0.5),
        'w_o_ret': nrm(17, (DEPTH, RET_HEADS * RET_DV, D_MODEL), (RET_HEADS * RET_DV) ** -0.5),
        'w_out': nrm(18, (DEPTH, D_MODEL, D_MODEL), D_MODEL ** -0.5),
        'norm2_g': gain(19, (DEPTH, D_MODEL)),
        'w_gate_dense': nrm(20, (n_dense, D_MODEL, D_FF), D_MODEL ** -0.5),
        'w_up_dense': nrm(21, (n_dense, D_MODEL, D_FF), D_MODEL ** -0.5),
        'w_down_dense': nrm(22, (n_dense, D_FF, D_MODEL), D_FF ** -0.5),
        'w_router': nrm(23, (n_moe, D_MODEL, N_EXPERTS), D_MODEL ** -0.5),
        'w_gate_moe': nrm(24, (n_moe, N_EXPERTS, D_MODEL, D_FF_EXPERT), D_MODEL ** -0.5),
        'w_up_moe': nrm(25, (n_moe, N_EXPERTS, D_MODEL, D_FF_EXPERT), D_MODEL ** -0.5),
        'w_down_moe': nrm(26, (n_moe, N_EXPERTS, D_FF_EXPERT, D_MODEL), D_FF_EXPERT ** -0.5),
        'final_norm_g': gain(27, (D_MODEL,)),
    }


def reference(x_prompt, x_sample, c, cache_ckv, cache_krope, state_ret, c_ctx, w_ada, b_ada, norm1_g, w_in,
              kv_norm_g, w_ukv, ret_decay_logit, ret_gn_g, w_o_mla, w_o_fnet, w_o_ret, w_out, norm2_g,
              w_gate_dense, w_up_dense, w_down_dense, w_router, w_gate_moe, w_up_moe, w_down_moe, final_norm_g):
    def channel_mixer(h, l):
        i = l // 2
        if l % 2 == 0:
            return swiglu(h, w_gate_dense[i], w_up_dense[i], w_down_dense[i])
        return moe_swiglu(h, w_router[i], w_gate_moe[i], w_up_moe[i], w_down_moe[i])

    rope = axial_rope(x_sample.shape[1], x_sample.dtype)
    xp, xs = x_prompt, x_sample
    ckv_list, krope_list, ret_list = [], [], []
    for l in range(DEPTH):
        mix_w = (w_in[l], kv_norm_g[l], w_ukv[l], ret_decay_logit[l], ret_gn_g[l],
                 w_o_mla[l], w_o_fnet[l], w_o_ret[l], w_out[l])
        sh1, sc1, g1, sh2, sc2, g2 = adaln(c_ctx[None, :], w_ada[l], b_ada[l])
        h = rms_norm(xp, norm1_g[l]) * (1 + sc1) + sh1
        mix, ckv_c, krope_c, s_c = mixer(h, *mix_w)
        xp = xp + g1 * mix
        h = rms_norm(xp, norm2_g[l]) * (1 + sc2) + sh2
        xp = xp + g2 * channel_mixer(h, l)
        ckv_list.append(ckv_c)
        krope_list.append(krope_c)
        ret_list.append(s_c)
        sh1, sc1, g1, sh2, sc2, g2 = adaln(c, w_ada[l], b_ada[l])
        h = rms_norm(xs, norm1_g[l]) * (1 + sc1) + sh1
        mix, _, _, _ = mixer(h, *mix_w, rope=rope,
                             ctx=(cache_ckv[:, l], cache_krope[:, l], state_ret[:, l]))
        xs = xs + g1 * mix
        h = rms_norm(xs, norm2_g[l]) * (1 + sc2) + sh2
        xs = xs + g2 * channel_mixer(h, l)
    y_prompt = rms_norm(xp, final_norm_g)
    y_sample = rms_norm(xs, final_norm_g)
    new_ckv = jnp.stack(ckv_list, axis=1)
    new_krope = jnp.stack(krope_list, axis=1)
    new_ret = jnp.stack(ret_list, axis=1)
    return (y_prompt, y_sample, new_ckv, new_krope, new_ret)
```

```python
import functools
import math

import jax
import jax.numpy as jnp
from jax import lax
from jax.experimental import pallas as pl
from jax.experimental.pallas import tpu as pltpu

f32 = jnp.float32
bf16 = jnp.bfloat16

D_MODEL = 2048
GRID_W = 64
MLA_HEADS = 16
MLA_NOPE = 128
MLA_ROPE = 64
MLA_V = 128
KV_LORA = 512
ROPE_BASE = 10000.0
FNET_GROUPS = 4
FNET_GC = 512
RET_HEADS = 8
RET_DK = 128
RET_DV = 256
RET_CHUNK = 128
N_EXPERTS = 8
EPS = 1e-6
GN_EPS = 1e-5

LANES = 128
VMEM_LIMIT_BYTES = 56 * 1024 * 1024

Q_NOPE_W = MLA_HEADS * MLA_NOPE
Q_ROPE_W = MLA_HEADS * MLA_ROPE
OFF_Q = 0
OFF_QROPE = Q_NOPE_W
OFF_F = OFF_Q + Q_NOPE_W + Q_ROPE_W
OFF_RQ = OFF_F + FNET_GROUPS * FNET_GC
OFF_RK = OFF_RQ + RET_HEADS * RET_DK
OFF_RV = OFF_RK + RET_HEADS * RET_DK
OFF_RG = OFF_RV + RET_HEADS * RET_DV
OFF_GATES = OFF_RG + RET_HEADS * RET_DV
N_MAIN = OFF_GATES + 3 * D_MODEL
CKR_W = KV_LORA + LANES


def _cparams(*sem):
    return pltpu.CompilerParams(dimension_semantics=sem, vmem_limit_bytes=VMEM_LIMIT_BYTES)


def _tile(n, pref):
    t = min(n, pref)
    while n % t:
        t -= 8
    return t


def _silu(x):
    return x * jax.nn.sigmoid(x)


def _rope128(x, cos, sin):
    lane = lax.broadcasted_iota(jnp.int32, x.shape, 1)
    first = (lane % MLA_ROPE) < (MLA_ROPE // 2)
    swapped = jnp.where(first, pltpu.roll(x, LANES - MLA_ROPE // 2, 1), pltpu.roll(x, MLA_ROPE // 2, 1))
    return x * cos + swapped * sin


def _adaln_kernel(c_ref, w_ref, b_ref, o_ref):
    s = _silu(c_ref[...]).astype(bf16)
    o_ref[0] = jnp.dot(s, w_ref[0].astype(bf16), preferred_element_type=f32) + b_ref[0]


def adaln_all(cond, w_ada, b_ada):
    depth, d, n = w_ada.shape
    r = cond.shape[0]
    tn = _tile(n, 1024)
    return pl.pallas_call(
        _adaln_kernel,
        out_shape=jax.ShapeDtypeStruct((depth, r, n), f32),
        grid=(depth, n // tn),
        in_specs=[pl.BlockSpec((r, d), lambda l, j: (0, 0)),
                  pl.BlockSpec((1, d, tn), lambda l, j: (l, 0, j)),
                  pl.BlockSpec((1, 1, tn), lambda l, j: (l, 0, j))],
        out_specs=pl.BlockSpec((1, r, tn), lambda l, j: (l, 0, j)),
        compiler_params=_cparams("arbitrary", "arbitrary"),
        name="adaln",
    )(cond, w_ada, b_ada.reshape(depth, 1, n))


def _norm_mod_kernel(x_ref, g_ref, sc_ref, sh_ref, o_ref):
    x = x_ref[0]
    y = x * lax.rsqrt(jnp.mean(x * x, axis=-1, keepdims=True) + EPS)
    y = y * g_ref[...]
    o_ref[0] = (y * (1.0 + sc_ref[0]) + sh_ref[0]).astype(o_ref.dtype)


def norm_mod(x, g, sc, sh, out_dtype):
    nb, lr, d = x.shape
    tm = _tile(lr, 512)
    return pl.pallas_call(
        _norm_mod_kernel,
        out_shape=jax.ShapeDtypeStruct((nb, lr, d), out_dtype),
        grid=(nb, lr // tm),
        in_specs=[pl.BlockSpec((1, tm, d), lambda b, i: (b, i, 0)),
                  pl.BlockSpec((1, d), lambda b, i: (0, 0)),
                  pl.BlockSpec((1, 1, d), lambda b, i: (b, 0, 0)),
                  pl.BlockSpec((1, 1, d), lambda b, i: (b, 0, 0))],
        out_specs=pl.BlockSpec((1, tm, d), lambda b, i: (b, i, 0)),
        compiler_params=_cparams("arbitrary", "arbitrary"),
        name="norm_mod",
    )(x, g.reshape(1, d), sc, sh)


def _final_norm_kernel(x_ref, g_ref, o_ref):
    x = x_ref[0]
    y = x * lax.rsqrt(jnp.mean(x * x, axis=-1, keepdims=True) + EPS)
    o_ref[0] = y * g_ref[...]


def final_norm(x, g, b_off, nb_out):
    _, lr, d = x.shape
    tm = _tile(lr, 512)
    return pl.pallas_call(
        _final_norm_kernel,
        out_shape=jax.ShapeDtypeStruct((nb_out, lr, d), f32),
        grid=(nb_out, lr // tm),
        in_specs=[pl.BlockSpec((1, tm, d), lambda b, i: (b + b_off, i, 0)),
                  pl.BlockSpec((1, d), lambda b, i: (0, 0))],
        out_specs=pl.BlockSpec((1, tm, d), lambda b, i: (b, i, 0)),
        compiler_params=_cparams("arbitrary", "arbitrary"),
        name="final_norm",
    )(x, g.reshape(1, d))


def _inproj_kernel(h_ref, w_ref, cos_ref, sin_ref, o_ref, *, tn):
    b = pl.program_id(0)
    j = pl.program_id(2)
    acc = jnp.dot(h_ref[0], w_ref[...], preferred_element_type=f32)
    j_rope = OFF_QROPE // tn
    j_rk = OFF_RK // tn
    j_rv = OFF_RV // tn
    j_rg = OFF_RG // tn
    j_gates = OFF_GATES // tn
    is_rope = jnp.logical_and(jnp.logical_and(j >= j_rope, j < OFF_F // tn), b >= 1)
    is_rk = jnp.logical_and(j >= j_rk, j < j_rv)
    is_silu = jnp.logical_and(j >= j_rg, j < j_gates)
    is_sig = j >= j_gates
    plain = jnp.logical_not(is_rope | is_rk | is_silu | is_sig)

    @pl.when(plain)
    def _():
        o_ref[0] = acc.astype(o_ref.dtype)

    @pl.when(is_rope)
    def _():
        cos = cos_ref[...]
        sin = sin_ref[...]
        for s in range(tn // LANES):
            sl = slice(s * LANES, (s + 1) * LANES)
            o_ref[0, :, sl] = _rope128(acc[:, sl], cos, sin).astype(o_ref.dtype)

    @pl.when(is_rk)
    def _():
        o_ref[0] = (acc * (RET_DK ** -0.5)).astype(o_ref.dtype)

    @pl.when(is_silu)
    def _():
        o_ref[0] = _silu(acc).astype(o_ref.dtype)

    @pl.when(is_sig)
    def _():
        o_ref[0] = jax.nn.sigmoid(acc).astype(o_ref.dtype)


def inproj_main(h, w_main, cos128, sin128):
    nb, lr, d = h.shape
    tm = _tile(lr, 1024)
    tn = 1024
    return pl.pallas_call(
        functools.partial(_inproj_kernel, tn=tn),
        out_shape=jax.ShapeDtypeStruct((nb, lr, N_MAIN), bf16),
        grid=(nb, lr // tm, N_MAIN // tn),
        in_specs=[pl.BlockSpec((1, tm, d), lambda b, i, j: (b, i, 0)),
                  pl.BlockSpec((d, tn), lambda b, i, j: (0, j)),
                  pl.BlockSpec((tm, LANES), lambda b, i, j: (i, 0)),
                  pl.BlockSpec((tm, LANES), lambda b, i, j: (i, 0))],
        out_specs=pl.BlockSpec((1, tm, tn), lambda b, i, j: (b, i, j)),
        compiler_params=_cparams("arbitrary", "arbitrary", "arbitrary"),
        name="inproj_main",
    )(h, w_main, cos128, sin128)


def _ckr_kernel(h_ref, w_ref, g_ref, cos_ref, sin_ref, ckv_ref, kr_ref, krr_ref):
    acc = jnp.dot(h_ref[0], w_ref[...], preferred_element_type=f32)
    c = acc[:, :KV_LORA]
    y = c * lax.rsqrt(jnp.mean(c * c, axis=-1, keepdims=True) + EPS)
    ckv_ref[0] = y * g_ref[...]
    kr = acc[:, KV_LORA:]
    kr_ref[0] = kr
    krr_ref[0] = _rope128(kr, cos_ref[...], sin_ref[...]).astype(krr_ref.dtype)


def inproj_ckr(h, w_ckr, kv_g, cos128, sin128):
    nb, lr, d = h.shape
    tm = _tile(lr, 1024)
    return pl.pallas_call(
        _ckr_kernel,
        out_shape=(jax.ShapeDtypeStruct((nb, lr, KV_LORA), f32),
                   jax.ShapeDtypeStruct((nb, lr, LANES), f32),
                   jax.ShapeDtypeStruct((nb, lr, LANES), bf16)),
        grid=(nb, lr // tm),
        in_specs=[pl.BlockSpec((1, tm, d), lambda b, i: (b, i, 0)),
                  pl.BlockSpec((d, CKR_W), lambda b, i: (0, 0)),
                  pl.BlockSpec((1, KV_LORA), lambda b, i: (0, 0)),
                  pl.BlockSpec((tm, LANES), lambda b, i: (i, 0)),
                  pl.BlockSpec((tm, LANES), lambda b, i: (i, 0))],
        out_specs=(pl.BlockSpec((1, tm, KV_LORA), lambda b, i: (b, i, 0)),
                   pl.BlockSpec((1, tm, LANES), lambda b, i: (b, i, 0)),
                   pl.BlockSpec((1, tm, LANES), lambda b, i: (b, i, 0))),
        compiler_params=_cparams("arbitrary", "arbitrary"),
        name="inproj_ckr",
    )(h, w_ckr, kv_g.reshape(1, KV_LORA), cos128, sin128)


def _mm_kernel(x_ref, w_ref, o_ref):
    o_ref[...] = jnp.dot(x_ref[...], w_ref[...], preferred_element_type=f32).astype(o_ref.dtype)


def matmul(x, w, out_dtype, tm_pref=512, tn_pref=4096):
    m, k = x.shape
    _, n = w.shape
    tm = _tile(m, tm_pref)
    tn = _tile(n, tn_pref)
    return pl.pallas_call(
        _mm_kernel,
        out_shape=jax.ShapeDtypeStruct((m, n), out_dtype),
        grid=(m // tm, n // tn),
        in_specs=[pl.BlockSpec((tm, k), lambda i, j: (i, 0)),
                  pl.BlockSpec((k, tn), lambda i, j: (0, j))],
        out_specs=pl.BlockSpec((tm, tn), lambda i, j: (i, j)),
        compiler_params=_cparams("arbitrary", "arbitrary"),
        name="matmul",
    )(x, w)


def _attn_kernel(qn_ref, qr_ref, kn_ref, v_ref, kr_ref, o_ref, kcat_ref, *, nk, tk, scale):
    h = pl.program_id(1)
    qi = pl.program_id(2)

    @pl.when(qi == 0)
    def _():
        kcat_ref[:, :MLA_NOPE] = kn_ref[0]
        kcat_ref[:, MLA_NOPE:] = kr_ref[0]

    qr = qr_ref[0]
    lane = lax.broadcasted_iota(jnp.int32, qr.shape, 1)
    qr = jnp.where((lane // MLA_ROPE) == (h % 2), qr, jnp.zeros_like(qr))
    qcat = jnp.concatenate([qn_ref[0], qr], axis=1)
    tq = qcat.shape[0]

    def body(c, carry):
        m, l, acc = carry
        r = pl.multiple_of(c * tk, tk)
        k = kcat_ref[pl.ds(r, tk), :]
        s = lax.dot_general(qcat, k, (((1,), (1,)), ((), ())), preferred_element_type=f32) * scale
        m_new = jnp.maximum(m, jnp.max(s, axis=1, keepdims=True))
        a = jnp.exp(m - m_new)
        p = jnp.exp(s - m_new)
        l = a * l + jnp.sum(p, axis=1, keepdims=True)
        acc = a * acc + jnp.dot(p.astype(bf16), v_ref[0, pl.ds(r, tk), :], preferred_element_type=f32)
        return m_new, l, acc

    m0 = jnp.full((tq, 1), -jnp.inf, f32)
    l0 = jnp.zeros((tq, 1), f32)
    a0 = jnp.zeros((tq, MLA_V), f32)
    _, l, acc = lax.fori_loop(0, nk, body, (m0, l0, a0))
    o_ref[0] = (acc / l).astype(o_ref.dtype)


def attention(main, kv, krope, nb, lq, b_off):
    lk = kv.shape[1]
    tq = _tile(lq, 256)
    tk = _tile(lk, 256)
    scale = (MLA_NOPE + MLA_ROPE) ** -0.5
    qr_blk = OFF_QROPE // LANES
    return pl.pallas_call(
        functools.partial(_attn_kernel, nk=lk // tk, tk=tk, scale=scale),
        out_shape=jax.ShapeDtypeStruct((nb, lq, MLA_HEADS * MLA_V), bf16),
        grid=(nb, MLA_HEADS, lq // tq),
        in_specs=[pl.BlockSpec((1, tq, MLA_NOPE), lambda b, h, i: (b + b_off, i, h)),
                  pl.BlockSpec((1, tq, LANES), lambda b, h, i: (b + b_off, i, qr_blk + h // 2)),
                  pl.BlockSpec((1, lk, MLA_NOPE), lambda b, h, i: (b, 0, 2 * h)),
                  pl.BlockSpec((1, lk, MLA_V), lambda b, h, i: (b, 0, 2 * h + 1)),
                  pl.BlockSpec((1, lk, LANES), lambda b, h, i: (b, 0, 0))],
        out_specs=pl.BlockSpec((1, tq, MLA_V), lambda b, h, i: (b, i, h)),
        scratch_shapes=[pltpu.VMEM((lk, MLA_NOPE + LANES), bf16)],
        compiler_params=_cparams("arbitrary", "arbitrary", "arbitrary"),
        name="attention",
    )(main, main, kv, kv, krope)


def _dft_ch_kernel(x_ref, w_ref, o_ref):
    o_ref[0, 0] = jnp.dot(x_ref[0], w_ref[...], preferred_element_type=f32).astype(o_ref.dtype)


def dft_channels(main, w_ch, nb, ls, b_off):
    tm = _tile(ls, 1024)
    f_blk = OFF_F // FNET_GC
    return pl.pallas_call(
        _dft_ch_kernel,
        out_shape=jax.ShapeDtypeStruct((nb, 2, ls, FNET_GROUPS * FNET_GC), bf16),
        grid=(nb, ls // tm, FNET_GROUPS, 2),
        in_specs=[pl.BlockSpec((1, tm, FNET_GC), lambda b, i, g, p: (b + b_off, i, f_blk + g)),
                  pl.BlockSpec((FNET_GC, FNET_GC), lambda b, i, g, p: (0, p))],
        out_specs=pl.BlockSpec((1, 1, tm, FNET_GC), lambda b, i, g, p: (b, p, i, g)),
        compiler_params=_cparams("arbitrary", "arbitrary", "arbitrary", "arbitrary"),
        name="dft_channels",
    )(main, w_ch)


def _dft_seq_kernel(l_ref, r_ref, o_ref):
    o_ref[0] = jnp.dot(l_ref[...], r_ref[0], preferred_element_type=f32).astype(o_ref.dtype)


def dft_sequence(cs, ab):
    nb, k, w = ab.shape
    ls = cs.shape[0]
    tm = _tile(ls, 512)
    tn = _tile(w, 512)
    return pl.pallas_call(
        _dft_seq_kernel,
        out_shape=jax.ShapeDtypeStruct((nb, ls, w), bf16),
        grid=(nb, w // tn, ls // tm),
        in_specs=[pl.BlockSpec((tm, k), lambda b, j, i: (i, 0)),
                  pl.BlockSpec((1, k, tn), lambda b, j, i: (b, 0, j))],
        out_specs=pl.BlockSpec((1, tm, tn), lambda b, j, i: (b, i, j)),
        compiler_params=_cparams("arbitrary", "arbitrary", "arbitrary"),
        name="dft_sequence",
    )(cs, ab)


def _dft_mats(n):
    idx = jnp.arange(n, dtype=jnp.int32)
    ang = ((idx[:, None] * idx[None, :]) % n).astype(f32) * (2.0 * math.pi / n)
    s = n ** -0.5
    return jnp.cos(ang) * s, jnp.sin(ang) * s


def _log_sigmoid(x):
    return jnp.minimum(x, 0.0) - jnp.log1p(jnp.exp(-jnp.abs(x)))


def _ret_kernel(*refs, nc, use_s0):
    if use_s0:
        dl_ref, q_ref, k_ref, v_ref, g_ref, gn_ref, s0_ref, o_ref, sfin_ref, of_ref, ob_ref = refs
    else:
        dl_ref, q_ref, k_ref, v_ref, g_ref, gn_ref, o_ref, sfin_ref, of_ref, ob_ref = refs
        s0_ref = None
    C = RET_CHUNK
    lgf = _log_sigmoid(dl_ref[0, 0])[0:1, :]
    lgb = _log_sigmoid(dl_ref[1, 0])[0:1, :]
    lgf2 = jnp.concatenate([lgf, lgf], axis=1)
    lgb2 = jnp.concatenate([lgb, lgb], axis=1)
    rows = lax.broadcasted_iota(jnp.int32, (C, C), 0)
    cols = lax.broadcasted_iota(jnp.int32, (C, C), 1)
    rel = (rows - cols).astype(f32)
    d_f = jnp.where(rel >= 0, jnp.exp(jnp.maximum(rel, 0.0) * lgf), 0.0)
    d_b = jnp.where(rel <= 0, jnp.exp(jnp.maximum(-rel, 0.0) * lgb), 0.0)
    r_v = lax.broadcasted_iota(jnp.int32, (C, RET_DV), 0).astype(f32)
    r_k = lax.broadcasted_iota(jnp.int32, (C, RET_DK), 0).astype(f32)
    qdec_f = jnp.exp((r_v + 1.0) * lgf2)
    qdec_b = jnp.exp((C - r_v) * lgb2)
    kdec_f = jnp.exp((C - 1.0 - r_k) * lgf)
    kdec_b = jnp.exp(r_k * lgb)
    cdec_f = jnp.exp(C * lgf2)
    cdec_b = jnp.exp(C * lgb2)

    def chunk(c, s, dmat, qdec, kdec, cdec):
        r = pl.multiple_of(c * C, C)
        q = q_ref[0, pl.ds(r, C), :]
        k = k_ref[0, pl.ds(r, C), :]
        v = v_ref[0, pl.ds(r, C), :]
        inner = lax.dot_general(q, k, (((1,), (1,)), ((), ())), preferred_element_type=f32) * dmat
        o = (jnp.dot(inner.astype(bf16), v, preferred_element_type=f32)
             + jnp.dot(q, s.astype(bf16), preferred_element_type=f32) * qdec)
        kd = (k.astype(f32) * kdec).T.astype(bf16)
        s_new = s * cdec + jnp.dot(kd, v, preferred_element_type=f32)
        return r, o, s_new

    def body(c, carry):
        sf, sb = carry
        r, o, sf = chunk(c, sf, d_f, qdec_f, kdec_f, cdec_f)
        of_ref[pl.ds(r, C), :] = o
        r, o, sb = chunk(nc - 1 - c, sb, d_b, qdec_b, kdec_b, cdec_b)
        ob_ref[pl.ds(r, C), :] = o
        return sf, sb

    if use_s0:
        init = (s0_ref[0, 0, 0], s0_ref[0, 1, 0])
    else:
        init = (jnp.zeros((RET_DK, RET_DV), f32), jnp.zeros((RET_DK, RET_DV), f32))
    sf, sb = lax.fori_loop(0, nc, body, init)
    sfin_ref[0, 0, 0] = sf
    sfin_ref[0, 1, 0] = sb

    gn = gn_ref[...]

    def norm_body(c, carry):
        r = pl.multiple_of(c * C, C)
        o = of_ref[pl.ds(r, C), :] + ob_ref[pl.ds(r, C), :]
        mu = jnp.mean(o, axis=-1, keepdims=True)
        d = o - mu
        y = d * lax.rsqrt(jnp.mean(d * d, axis=-1, keepdims=True) + GN_EPS) * gn
        o_ref[0, pl.ds(r, C), :] = (y * g_ref[0, pl.ds(r, C), :].astype(f32)).astype(o_ref.dtype)
        return carry

    lax.fori_loop(0, nc, norm_body, 0)


def retention(main, decay_logit, gn_g, s0, nb, ls, b_off):
    use_s0 = s0 is not None
    nc = ls // RET_CHUNK
    dl = jnp.broadcast_to(decay_logit.astype(f32)[:, :, None, None], (2, RET_HEADS, 8, LANES))
    in_specs = [pl.BlockSpec((2, 1, 8, LANES), lambda b, h: (0, h, 0, 0)),
                pl.BlockSpec((1, ls, RET_DK), lambda b, h: (b + b_off, 0, OFF_RQ // RET_DK + h)),
                pl.BlockSpec((1, ls, RET_DK), lambda b, h: (b + b_off, 0, OFF_RK // RET_DK + h)),
                pl.BlockSpec((1, ls, RET_DV), lambda b, h: (b + b_off, 0, OFF_RV // RET_DV + h)),
                pl.BlockSpec((1, ls, RET_DV), lambda b, h: (b + b_off, 0, OFF_RG // RET_DV + h)),
                pl.BlockSpec((1, RET_DV), lambda b, h: (0, h))]
    args = [dl, main, main, main, main, gn_g.reshape(1, RET_HEADS * RET_DV)]
    if use_s0:
        in_specs.append(pl.BlockSpec((1, 2, 1, RET_DK, RET_DV), lambda b, h: (b, 0, h, 0, 0)))
        args.append(s0)
    return pl.pallas_call(
        functools.partial(_ret_kernel, nc=nc, use_s0=use_s0),
        out_shape=(jax.ShapeDtypeStruct((nb, ls, RET_HEADS * RET_DV), bf16),
                   jax.ShapeDtypeStruct((nb, 2, RET_HEADS, RET_DK, RET_DV), f32)),
        grid=(nb, RET_HEADS),
        in_specs=in_specs,
        out_specs=(pl.BlockSpec((1, ls, RET_DV), lambda b, h: (b, 0, h)),
                   pl.BlockSpec((1, 2, 1, RET_DK, RET_DV), lambda b, h: (b, 0, h, 0, 0))),
        scratch_shapes=[pltpu.VMEM((ls, RET_DV), f32), pltpu.VMEM((ls, RET_DV), f32)],
        compiler_params=_cparams("arbitrary", "arbitrary"),
        name="retention",
    )(*args)


def _merge_kernel(a_ref, f_ref, r_ref, wa_ref, wf_ref, wr_ref, ga_ref, gf_ref, gr_ref, o_ref):
    m = ga_ref[...].astype(f32) * jnp.dot(a_ref[...], wa_ref[...], preferred_element_type=f32)
    m += gf_ref[...].astype(f32) * jnp.dot(f_ref[...], wf_ref[...], preferred_element_type=f32)
    m += gr_ref[...].astype(f32) * jnp.dot(r_ref[...], wr_ref[...], preferred_element_type=f32)
    o_ref[...] = m.astype(o_ref.dtype)


def merge_branches(att, four, ret, wa, wf, wr, main2d):
    m, d = att.shape
    tm = _tile(m, 512)
    tn = 512
    gb = OFF_GATES // tn
    nd = d // tn
    x_spec = pl.BlockSpec((tm, d), lambda i, j: (i, 0))
    w_spec = pl.BlockSpec((d, tn), lambda i, j: (0, j))
    return pl.pallas_call(
        _merge_kernel,
        out_shape=jax.ShapeDtypeStruct((m, d), bf16),
        grid=(m // tm, nd),
        in_specs=[x_spec, x_spec, x_spec, w_spec, w_spec, w_spec,
                  pl.BlockSpec((tm, tn), lambda i, j: (i, gb + j)),
                  pl.BlockSpec((tm, tn), lambda i, j: (i, gb + nd + j)),
                  pl.BlockSpec((tm, tn), lambda i, j: (i, gb + 2 * nd + j))],
        out_specs=pl.BlockSpec((tm, tn), lambda i, j: (i, j)),
        compiler_params=_cparams("arbitrary", "arbitrary"),
        name="merge_branches",
    )(att, four, ret, wa, wf, wr, main2d, main2d, main2d)


def _proj_res_kernel(m_ref, w_ref, x_ref, g_ref, o_ref):
    y = jnp.dot(m_ref[0], w_ref[...], preferred_element_type=f32)
    o_ref[0] = x_ref[0] + g_ref[0] * y


def proj_residual(merged, w, x, gate):
    nb, lr, d = x.shape
    tm = _tile(lr, 1024)
    tn = 1024
    return pl.pallas_call(
        _proj_res_kernel,
        out_shape=jax.ShapeDtypeStruct((nb, lr, d), f32),
        grid=(nb, lr // tm, d // tn),
        in_specs=[pl.BlockSpec((1, tm, d), lambda b, i, j: (b, i, 0)),
                  pl.BlockSpec((d, tn), lambda b, i, j: (0, j)),
                  pl.BlockSpec((1, tm, tn), lambda b, i, j: (b, i, j)),
                  pl.BlockSpec((1, 1, tn), lambda b, i, j: (b, 0, j))],
        out_specs=pl.BlockSpec((1, tm, tn), lambda b, i, j: (b, i, j)),
        compiler_params=_cparams("arbitrary", "arbitrary", "arbitrary"),
        name="proj_residual",
    )(merged, w, x, gate)


def _ffn_kernel(h_ref, wg_ref, wu_ref, wd_ref, x_ref, g_ref, o_ref, acc_ref):
    j = pl.program_id(2)

    @pl.when(j == 0)
    def _():
        acc_ref[...] = jnp.zeros_like(acc_ref)

    h = h_ref[0]
    a = jnp.dot(h, wg_ref[...], preferred_element_type=f32)
    u = jnp.dot(h, wu_ref[...], preferred_element_type=f32)
    t = (_silu(a) * u).astype(bf16)
    acc_ref[...] += jnp.dot(t, wd_ref[...], preferred_element_type=f32)

    @pl.when(j == pl.num_programs(2) - 1)
    def _():
        o_ref[0] = x_ref[0] + g_ref[0] * acc_ref[...]


def ffn_dense(h, wg, wu, wd, x, gate):
    nb, lr, d = x.shape
    dff = wg.shape[1]
    tm = _tile(lr, 512)
    tf = _tile(dff, 512)
    return pl.pallas_call(
        _ffn_kernel,
        out_shape=jax.ShapeDtypeStruct((nb, lr, d), f32),
        grid=(nb, lr // tm, dff // tf),
        in_specs=[pl.BlockSpec((1, tm, d), lambda b, i, j: (b, i, 0)),
                  pl.BlockSpec((d, tf), lambda b, i, j: (0, j)),
                  pl.BlockSpec((d, tf), lambda b, i, j: (0, j)),
                  pl.BlockSpec((tf, d), lambda b, i, j: (j, 0)),
                  pl.BlockSpec((1, tm, d), lambda b, i, j: (b, i, 0)),
                  pl.BlockSpec((1, 1, d), lambda b, i, j: (b, 0, 0))],
        out_specs=pl.BlockSpec((1, tm, d), lambda b, i, j: (b, i, 0)),
        scratch_shapes=[pltpu.VMEM((tm, d), f32)],
        compiler_params=_cparams("arbitrary", "arbitrary", "arbitrary"),
        name="ffn_dense",
    )(h, wg, wu, wd, x, gate)


MOE_TM = 512
MOE_TT = 256


def _router_kernel(h_ref, w_ref, e_ref, p_ref):
    logits = jnp.dot(h_ref[...].astype(bf16), w_ref[...], preferred_element_type=f32)
    lane = lax.broadcasted_iota(jnp.int32, logits.shape, 1)
    lane_f = lane.astype(f32)
    neg = jnp.full_like(logits, -jnp.inf)
    l0 = jnp.where(lane < N_EXPERTS, logits, neg)
    v1 = jnp.max(l0, axis=1, keepdims=True)
    i1 = jnp.min(jnp.where(l0 == v1, lane_f, float(LANES)), axis=1, keepdims=True)
    l1 = jnp.where(lane_f == i1, neg, l0)
    v2 = jnp.max(l1, axis=1, keepdims=True)
    i2 = jnp.min(jnp.where(l1 == v2, lane_f, float(LANES)), axis=1, keepdims=True)
    e = jnp.exp(v2 - v1)
    den = 1.0 + e
    e_ref[...] = jnp.where(lane == 0, i1, jnp.where(lane == 1, i2, 0.0)).astype(jnp.int32)
    p_ref[...] = jnp.where(lane == 0, 1.0 / den, jnp.where(lane == 1, e / den, 0.0))


def moe_router(h2d, w_router_pad):
    n, d = h2d.shape
    tm = _tile(n, 512)
    return pl.pallas_call(
        _router_kernel,
        out_shape=(jax.ShapeDtypeStruct((n, LANES), jnp.int32), jax.ShapeDtypeStruct((n, LANES), f32)),
        grid=(n // tm,),
        in_specs=[pl.BlockSpec((tm, d), lambda i: (i, 0)),
                  pl.BlockSpec((d, LANES), lambda i: (0, 0))],
        out_specs=(pl.BlockSpec((tm, LANES), lambda i: (i, 0)), pl.BlockSpec((tm, LANES), lambda i: (i, 0))),
        compiler_params=_cparams("arbitrary"),
        name="moe_router",
    )(h2d, w_router_pad)


def _dispatch_kernel(pos_ref, h_ref, xs_in_ref, xs_ref, sem):
    del xs_in_ref
    tt = h_ref.shape[0]

    def row_copy(t, p):
        return pltpu.make_async_copy(h_ref.at[pl.ds(t, 1), :], xs_ref.at[pl.ds(p, 1), :], sem)

    def start(t, c):
        row_copy(t, pos_ref[0, 0, 2 * t]).start()
        row_copy(t, pos_ref[0, 0, 2 * t + 1]).start()
        return c

    lax.fori_loop(0, tt, start, 0)

    def wait(t, c):
        row_copy(0, 0).wait()
        row_copy(0, 0).wait()
        return c

    lax.fori_loop(0, tt, wait, 0)


def moe_dispatch(h2d, pos, p_rows):
    n, d = h2d.shape
    tt = _tile(n, MOE_TT)
    pos3 = pos.reshape(n // tt, 1, 2 * tt)
    xs0 = jnp.zeros((p_rows, d), h2d.dtype)
    return pl.pallas_call(
        _dispatch_kernel,
        out_shape=jax.ShapeDtypeStruct((p_rows, d), h2d.dtype),
        grid=(n // tt,),
        in_specs=[pl.BlockSpec((1, 1, 2 * tt), lambda i: (i, 0, 0), memory_space=pltpu.SMEM),
                  pl.BlockSpec((tt, d), lambda i: (i, 0)),
                  pl.BlockSpec(memory_space=pl.ANY)],
        out_specs=pl.BlockSpec(memory_space=pl.ANY),
        scratch_shapes=[pltpu.SemaphoreType.DMA(())],
        input_output_aliases={2: 0},
        compiler_params=_cparams("arbitrary"),
        name="moe_dispatch",
    )(pos3, h2d, xs0)


def _moe_ffn_kernel(be_ref, nu_ref, x_ref, wg_ref, wu_ref, wd_ref, y_ref, xb_ref, acc_ref):
    del be_ref
    i = pl.program_id(0)
    j = pl.program_id(1)
    active = i < nu_ref[0]

    @pl.when(jnp.logical_and(active, j == 0))
    def _():
        xb_ref[...] = x_ref[...].astype(bf16)
        acc_ref[...] = jnp.zeros_like(acc_ref)

    @pl.when(active)
    def _():
        xb = xb_ref[...]
        a = jnp.dot(xb, wg_ref[0], preferred_element_type=f32)
        u = jnp.dot(xb, wu_ref[0], preferred_element_type=f32)
        t = (_silu(a) * u).astype(bf16)
        acc_ref[...] += jnp.dot(t, wd_ref[0], preferred_element_type=f32)

    @pl.when(j == pl.num_programs(1) - 1)
    def _():
        y_ref[...] = jnp.where(active, acc_ref[...], 0.0)


def moe_ffn(xs, blk_expert, n_used, wg, wu, wd):
    p_rows, d = xs.shape
    dff = wg.shape[2]
    tm = MOE_TM
    tf = _tile(dff, 512)
    nf = dff // tf

    def jj(i, j, nu):
        return jnp.where(i < nu[0], j, nf - 1)

    grid_spec = pltpu.PrefetchScalarGridSpec(
        num_scalar_prefetch=2,
        grid=(p_rows // tm, nf),
        in_specs=[pl.BlockSpec((tm, d), lambda i, j, be, nu: (i, 0)),
                  pl.BlockSpec((1, d, tf), lambda i, j, be, nu: (be[i], 0, jj(i, j, nu))),
                  pl.BlockSpec((1, d, tf), lambda i, j, be, nu: (be[i], 0, jj(i, j, nu))),
                  pl.BlockSpec((1, tf, d), lambda i, j, be, nu: (be[i], jj(i, j, nu), 0))],
        out_specs=pl.BlockSpec((tm, d), lambda i, j, be, nu: (i, 0)),
        scratch_shapes=[pltpu.VMEM((tm, d), bf16), pltpu.VMEM((tm, d), f32)],
    )
    return pl.pallas_call(
        _moe_ffn_kernel,
        out_shape=jax.ShapeDtypeStruct((p_rows, d), f32),
        grid_spec=grid_spec,
        compiler_params=_cparams("arbitrary", "arbitrary"),
        name="moe_ffn",
    )(blk_expert, n_used, xs, wg, wu, wd)


def _combine_kernel(pos_ref, y_ref, p_ref, x_ref, g_ref, o_ref, ybuf_ref, sem):
    tt = x_ref.shape[1]

    def row_copy(p, dst):
        return pltpu.make_async_copy(y_ref.at[pl.ds(p, 1), :], ybuf_ref.at[pl.ds(dst, 1), :], sem)

    def start(t, c):
        row_copy(pos_ref[0, 0, 2 * t], t).start()
        row_copy(pos_ref[0, 0, 2 * t + 1], tt + t).start()
        return c

    lax.fori_loop(0, tt, start, 0)

    def wait(t, c):
        row_copy(0, 0).wait()
        row_copy(0, 0).wait()
        return c

    lax.fori_loop(0, tt, wait, 0)
    p = p_ref[0]
    mix = p[:, 0:1] * ybuf_ref[pl.ds(0, tt), :] + p[:, 1:2] * ybuf_ref[pl.ds(tt, tt), :]
    o_ref[0] = x_ref[0] + g_ref[0] * mix


def moe_combine(y, pos, probs, x, gate):
    nb, lr, d = x.shape
    tt = _tile(lr, MOE_TT)
    nt = lr // tt
    pos3 = pos.reshape(nb * nt, 1, 2 * tt)
    probs3 = probs.reshape(nb, lr, LANES)
    return pl.pallas_call(
        _combine_kernel,
        out_shape=jax.ShapeDtypeStruct((nb, lr, d), f32),
        grid=(nb, nt),
        in_specs=[pl.BlockSpec((1, 1, 2 * tt), lambda b, i: (b * nt + i, 0, 0), memory_space=pltpu.SMEM),
                  pl.BlockSpec(memory_space=pl.ANY),
                  pl.BlockSpec((1, tt, LANES), lambda b, i: (b, i, 0)),
                  pl.BlockSpec((1, tt, d), lambda b, i: (b, i, 0)),
                  pl.BlockSpec((1, 1, d), lambda b, i: (b, 0, 0))],
        out_specs=pl.BlockSpec((1, tt, d), lambda b, i: (b, i, 0)),
        scratch_shapes=[pltpu.VMEM((2 * tt, d), f32), pltpu.SemaphoreType.DMA(())],
        compiler_params=_cparams("arbitrary", "arbitrary"),
        name="moe_combine",
    )(pos3, y, probs3, x, gate)


def moe_layer(h, x, gate, w_router, wg, wu, wd):
    nb, lr, d = h.shape
    n = nb * lr
    h2d = h.reshape(n, d)
    w_r = jnp.zeros((d, LANES), bf16).at[:, :N_EXPERTS].set(w_router.astype(bf16))
    e_out, p_out = moe_router(h2d, w_r)
    e_flat = e_out[:, :2].reshape(-1)
    onehot = (e_flat[:, None] == jnp.arange(N_EXPERTS, dtype=jnp.int32)[None, :]).astype(jnp.int32)
    csum = jnp.cumsum(onehot, axis=0)
    counts = csum[-1]
    rank = jnp.sum(onehot * csum, axis=1) - 1
    padded = (counts + MOE_TM - 1) // MOE_TM * MOE_TM
    pad_ends = jnp.cumsum(padded)
    pad_starts = pad_ends - padded
    pos = jnp.sum(onehot * pad_starts[None, :], axis=1) + rank
    n_blocks = (2 * n) // MOE_TM + N_EXPERTS
    blk_start = jnp.arange(n_blocks, dtype=jnp.int32) * MOE_TM
    blk_expert = jnp.minimum(jnp.sum((blk_start[:, None] >= pad_ends[None, :]).astype(jnp.int32), axis=1),
                             N_EXPERTS - 1).astype(jnp.int32)
    n_used = (pad_ends[-1] // MOE_TM).astype(jnp.int32).reshape(1)
    xs = moe_dispatch(h2d, pos.astype(jnp.int32), n_blocks * MOE_TM)
    y = moe_ffn(xs, blk_expert, n_used, wg, wu, wd)
    return moe_combine(y, pos.astype(jnp.int32), p_out, x, gate)


def _rope_tables(n_tok):
    rows = n_tok // GRID_W
    r = jnp.repeat(jnp.arange(rows, dtype=f32), GRID_W)
    cl = jnp.tile(jnp.arange(GRID_W, dtype=f32), rows)
    n_freq = MLA_ROPE // 4
    inv = ROPE_BASE ** (-jnp.arange(n_freq, dtype=f32) / n_freq)
    ang = jnp.concatenate([r[:, None] * inv, cl[:, None] * inv], axis=-1)
    cos, sin = jnp.cos(ang), jnp.sin(ang)
    return jnp.tile(cos, (1, 4)), jnp.concatenate([-sin, sin, -sin, sin], axis=-1)


def _prep_in_weights(w_in_l):
    d = w_in_l.shape[0]
    nq = MLA_HEADS * (MLA_NOPE + MLA_ROPE)
    wq = w_in_l[:, :nq].reshape(d, MLA_HEADS, MLA_NOPE + MLA_ROPE)
    w_ckv = w_in_l[:, nq:nq + KV_LORA]
    w_kr = w_in_l[:, nq + KV_LORA:nq + KV_LORA + MLA_ROPE]
    rest = w_in_l[:, nq + KV_LORA + MLA_ROPE:]
    w_main = jnp.concatenate([wq[:, :, :MLA_NOPE].reshape(d, Q_NOPE_W), wq[:, :, MLA_NOPE:].reshape(d, Q_ROPE_W),
                              rest], axis=1).astype(bf16)
    w_ckr = jnp.concatenate([w_ckv, w_kr, w_kr], axis=1).astype(bf16)
    return w_main, w_ckr


def kernel(x_prompt, x_sample, c, cache_ckv, cache_krope, state_ret, c_ctx, w_ada, b_ada, norm1_g, w_in, kv_norm_g, w_ukv, ret_decay_logit, ret_gn_g, w_o_mla, w_o_fnet, w_o_ret, w_out, norm2_g, w_gate_dense, w_up_dense, w_down_dense, w_router, w_gate_moe, w_up_moe, w_down_moe, final_norm_g):
    nbc, seq, d = x_prompt.shape
    nbs, ls, _ = x_sample.shape
    depth = w_in.shape[0]
    lr = nbc * seq
    assert lr == ls, "unified row layout needs BATCH * SEQ == DEC_SEQ"
    nb = 1 + nbs
    n_rows = nb * lr

    x = jnp.concatenate([x_prompt.reshape(1, lr, d), x_sample], axis=0)
    cond = jnp.concatenate([c_ctx[None, :], c, jnp.zeros((16 - nb, d), f32)], axis=0)
    mod = adaln_all(cond, w_ada, b_ada)[:, :nb]
    cos128, sin128 = _rope_tables(ls)

    w_ch = jnp.concatenate(_dft_mats(FNET_GC), axis=1).astype(bf16)
    c_c, s_c = _dft_mats(seq)
    cs_ctx = jnp.concatenate([c_c, -s_c], axis=1).astype(bf16)
    c_s, s_s = _dft_mats(ls)
    cs_lat = jnp.concatenate([c_s, -s_s], axis=1).astype(bf16)

    ckv_list, krope_list, ret_list = [], [], []
    for l in range(depth):
        sh1, sc1, g1, sh2, sc2, g2 = [m.reshape(nb, 1, d) for m in jnp.split(mod[l], 6, axis=-1)]
        w_main, w_ckr = _prep_in_weights(w_in[l])
        h = norm_mod(x, norm1_g[l], sc1, sh1, bf16)
        main = inproj_main(h, w_main, cos128, sin128)
        ckv, kr, krr = inproj_ckr(h, w_ckr, kv_norm_g[l], cos128, sin128)
        w_ukv_b = w_ukv[l].astype(bf16)

        ckv_c = ckv[0].reshape(nbc, seq, KV_LORA)
        kr_c = kr[0].reshape(nbc, seq, LANES)
        kv_c = matmul(ckv_c.reshape(lr, KV_LORA).astype(bf16), w_ukv_b, bf16).reshape(nbc, seq, -1)
        main_c = main.reshape(nb * nbc, seq, N_MAIN)
        att_c = attention(main_c, kv_c, kr_c.astype(bf16), nbc, seq, 0)
        ab_c = dft_channels(main_c, w_ch, nbc, seq, 0).reshape(nbc, 2 * seq, FNET_GROUPS * FNET_GC)
        four_c = dft_sequence(cs_ctx, ab_c)
        ret_c, sfin_c = retention(main_c, ret_decay_logit[l], ret_gn_g[l], None, nbc, seq, 0)
        ckv_list.append(ckv_c)
        krope_list.append(kr_c[:, :, :MLA_ROPE])
        ret_list.append(sfin_c)

        ckv_keys = jnp.concatenate([ckv[1:], cache_ckv[:, l]], axis=1).astype(bf16)
        lk = ckv_keys.shape[1]
        kv_s = matmul(ckv_keys.reshape(nbs * lk, KV_LORA), w_ukv_b, bf16).reshape(nbs, lk, -1)
        ck = cache_krope[:, l].astype(bf16)
        kr_keys = jnp.concatenate([krr[1:], jnp.concatenate([ck, ck], axis=-1)], axis=1)
        att_s = attention(main, kv_s, kr_keys, nbs, ls, 1)
        ab_s = dft_channels(main, w_ch, nbs, ls, 1).reshape(nbs, 2 * ls, FNET_GROUPS * FNET_GC)
        four_s = dft_sequence(cs_lat, ab_s)
        ret_s, _ = retention(main, ret_decay_logit[l], ret_gn_g[l], state_ret[:, l], nbs, ls, 1)

        att = jnp.concatenate([att_c.reshape(lr, -1), att_s.reshape(nbs * ls, -1)], axis=0)
        four = jnp.concatenate([four_c.reshape(lr, -1), four_s.reshape(nbs * ls, -1)], axis=0)
        ret = jnp.concatenate([ret_c.reshape(lr, -1), ret_s.reshape(nbs * ls, -1)], axis=0)
        merged = merge_branches(att, four, ret, w_o_mla[l].astype(bf16), w_o_fnet[l].astype(bf16),
                                w_o_ret[l].astype(bf16), main.reshape(n_rows, N_MAIN))
        x = proj_residual(merged.reshape(nb, lr, d), w_out[l].astype(bf16), x, g1)

        i = l // 2
        if l % 2 == 0:
            h2 = norm_mod(x, norm2_g[l], sc2, sh2, bf16)
            x = ffn_dense(h2, w_gate_dense[i].astype(bf16), w_up_dense[i].astype(bf16),
                          w_down_dense[i].astype(bf16), x, g2)
        else:
            h2 = norm_mod(x, norm2_g[l], sc2, sh2, f32)
            x = moe_layer(h2, x, g2, w_router[i], w_gate_moe[i].astype(bf16), w_up_moe[i].astype(bf16),
                          w_down_moe[i].astype(bf16))

    y_prompt = final_norm(x, final_norm_g, 0, 1).reshape(nbc, seq, d)
    y_sample = final_norm(x, final_norm_g, 1, nbs)
    new_ckv = jnp.stack(ckv_list, axis=1)
    new_krope = jnp.stack(krope_list, axis=1)
    new_ret = jnp.stack(ret_list, axis=1)
    return (y_prompt, y_sample, new_ckv, new_krope, new_ret)
```

```python
import functools
import math

import jax
import jax.numpy as jnp
from jax import lax
from jax.experimental import pallas as pl
from jax.experimental.pallas import tpu as pltpu

f32 = jnp.float32
bf16 = jnp.bfloat16

D_MODEL = 2048
GRID_W = 64
MLA_HEADS = 16
MLA_NOPE = 128
MLA_ROPE = 64
MLA_V = 128
KV_LORA = 512
ROPE_BASE = 10000.0
FNET_GROUPS = 4
FNET_GC = 512
RET_HEADS = 8
RET_DK = 128
RET_DV = 256
RET_CHUNK = 256
N_EXPERTS = 8
EPS = 1e-6
GN_EPS = 1e-5

LANES = 128
VMEM_LIMIT_BYTES = 56 * 1024 * 1024

Q_NOPE_W = MLA_HEADS * MLA_NOPE
Q_ROPE_W = MLA_HEADS * MLA_ROPE
OFF_Q = 0
OFF_QROPE = Q_NOPE_W
OFF_F = OFF_Q + Q_NOPE_W + Q_ROPE_W
OFF_RQ = OFF_F + FNET_GROUPS * FNET_GC
OFF_RK = OFF_RQ + RET_HEADS * RET_DK
OFF_RV = OFF_RK + RET_HEADS * RET_DK
OFF_RG = OFF_RV + RET_HEADS * RET_DV
OFF_GATES = OFF_RG + RET_HEADS * RET_DV
N_MAIN = OFF_GATES + 3 * D_MODEL
CKR_W = KV_LORA + LANES
Q_SCALE = (MLA_NOPE + MLA_ROPE) ** -0.5 * math.log2(math.e)


def _cparams(*sem):
    return pltpu.CompilerParams(dimension_semantics=sem, vmem_limit_bytes=VMEM_LIMIT_BYTES)


def _tile(n, pref):
    t = min(n, pref)
    while n % t:
        t -= 8
    return t


def _silu(x):
    return x * jax.nn.sigmoid(x)


def _rope128(x, cos, sin):
    lane = lax.broadcasted_iota(jnp.int32, x.shape, 1)
    first = (lane % MLA_ROPE) < (MLA_ROPE // 2)
    swapped = jnp.where(first, pltpu.roll(x, LANES - MLA_ROPE // 2, 1), pltpu.roll(x, MLA_ROPE // 2, 1))
    return x * cos + swapped * sin


def _adaln_kernel(c_ref, w_ref, b_ref, o_ref):
    s = _silu(c_ref[...]).astype(bf16)
    o_ref[0] = jnp.dot(s, w_ref[0].astype(bf16), preferred_element_type=f32) + b_ref[0]


def adaln_all(cond, w_ada, b_ada):
    depth, d, n = w_ada.shape
    r = cond.shape[0]
    tn = _tile(n, 1024)
    return pl.pallas_call(
        _adaln_kernel,
        out_shape=jax.ShapeDtypeStruct((depth, r, n), f32),
        grid=(depth, n // tn),
        in_specs=[pl.BlockSpec((r, d), lambda l, j: (0, 0)),
                  pl.BlockSpec((1, d, tn), lambda l, j: (l, 0, j)),
                  pl.BlockSpec((1, 1, tn), lambda l, j: (l, 0, j))],
        out_specs=pl.BlockSpec((1, r, tn), lambda l, j: (l, 0, j)),
        compiler_params=_cparams("arbitrary", "arbitrary"),
        name="adaln",
    )(cond, w_ada, b_ada.reshape(depth, 1, n))


def _norm_mod_kernel(x_ref, g_ref, sc_ref, sh_ref, o_ref):
    x = x_ref[0]
    y = x * lax.rsqrt(jnp.mean(x * x, axis=-1, keepdims=True) + EPS)
    y = y * g_ref[...]
    o_ref[0] = (y * (1.0 + sc_ref[0]) + sh_ref[0]).astype(o_ref.dtype)


def norm_mod(x, g, sc, sh, out_dtype):
    nb, lr, d = x.shape
    tm = _tile(lr, 512)
    return pl.pallas_call(
        _norm_mod_kernel,
        out_shape=jax.ShapeDtypeStruct((nb, lr, d), out_dtype),
        grid=(nb, lr // tm),
        in_specs=[pl.BlockSpec((1, tm, d), lambda b, i: (b, i, 0)),
                  pl.BlockSpec((1, d), lambda b, i: (0, 0)),
                  pl.BlockSpec((1, 1, d), lambda b, i: (b, 0, 0)),
                  pl.BlockSpec((1, 1, d), lambda b, i: (b, 0, 0))],
        out_specs=pl.BlockSpec((1, tm, d), lambda b, i: (b, i, 0)),
        compiler_params=_cparams("arbitrary", "arbitrary"),
        name="norm_mod",
    )(x, g.reshape(1, d), sc, sh)


def _final_norm_kernel(x_ref, g_ref, o_ref):
    x = x_ref[0]
    y = x * lax.rsqrt(jnp.mean(x * x, axis=-1, keepdims=True) + EPS)
    o_ref[0] = y * g_ref[...]


def final_norm(x, g, b_off, nb_out):
    _, lr, d = x.shape
    tm = _tile(lr, 512)
    return pl.pallas_call(
        _final_norm_kernel,
        out_shape=jax.ShapeDtypeStruct((nb_out, lr, d), f32),
        grid=(nb_out, lr // tm),
        in_specs=[pl.BlockSpec((1, tm, d), lambda b, i: (b + b_off, i, 0)),
                  pl.BlockSpec((1, d), lambda b, i: (0, 0))],
        out_specs=pl.BlockSpec((1, tm, d), lambda b, i: (b, i, 0)),
        compiler_params=_cparams("arbitrary", "arbitrary"),
        name="final_norm",
    )(x, g.reshape(1, d))


def _inproj_kernel(h_ref, w_ref, cos_ref, sin_ref, o_ref, *, tn):
    b = pl.program_id(0)
    j = pl.program_id(2)
    acc = jnp.dot(h_ref[0], w_ref[...], preferred_element_type=f32)
    j_rope = OFF_QROPE // tn
    j_rk = OFF_RK // tn
    j_rv = OFF_RV // tn
    j_rg = OFF_RG // tn
    j_gates = OFF_GATES // tn
    is_q = j < OFF_F // tn
    is_rope = jnp.logical_and(jnp.logical_and(j >= j_rope, is_q), b >= 1)
    is_qplain = jnp.logical_and(is_q, jnp.logical_not(is_rope))
    is_rk = jnp.logical_and(j >= j_rk, j < j_rv)
    is_silu = jnp.logical_and(j >= j_rg, j < j_gates)
    is_sig = j >= j_gates
    plain = jnp.logical_not(is_q | is_rk | is_silu | is_sig)

    @pl.when(plain)
    def _():
        o_ref[0] = acc.astype(o_ref.dtype)

    @pl.when(is_qplain)
    def _():
        o_ref[0] = (acc * Q_SCALE).astype(o_ref.dtype)

    @pl.when(is_rope)
    def _():
        cos = cos_ref[...] * Q_SCALE
        sin = sin_ref[...] * Q_SCALE
        for s in range(tn // LANES):
            sl = slice(s * LANES, (s + 1) * LANES)
            o_ref[0, :, sl] = _rope128(acc[:, sl], cos, sin).astype(o_ref.dtype)

    @pl.when(is_rk)
    def _():
        o_ref[0] = (acc * (RET_DK ** -0.5)).astype(o_ref.dtype)

    @pl.when(is_silu)
    def _():
        o_ref[0] = _silu(acc).astype(o_ref.dtype)

    @pl.when(is_sig)
    def _():
        o_ref[0] = jax.nn.sigmoid(acc).astype(o_ref.dtype)


def inproj_main(h, w_main, cos128, sin128):
    nb, lr, d = h.shape
    tm = _tile(lr, 1024)
    tn = 1024
    return pl.pallas_call(
        functools.partial(_inproj_kernel, tn=tn),
        out_shape=jax.ShapeDtypeStruct((nb, lr, N_MAIN), bf16),
        grid=(nb, lr // tm, N_MAIN // tn),
        in_specs=[pl.BlockSpec((1, tm, d), lambda b, i, j: (b, i, 0)),
                  pl.BlockSpec((d, tn), lambda b, i, j: (0, j)),
                  pl.BlockSpec((tm, LANES), lambda b, i, j: (i, 0)),
                  pl.BlockSpec((tm, LANES), lambda b, i, j: (i, 0))],
        out_specs=pl.BlockSpec((1, tm, tn), lambda b, i, j: (b, i, j)),
        compiler_params=_cparams("arbitrary", "arbitrary", "arbitrary"),
        name="inproj_main",
    )(h, w_main, cos128, sin128)


def _ckr_kernel(h_ref, w_ref, g_ref, cos_ref, sin_ref, ckv_ref, kr_ref, krr_ref):
    acc = jnp.dot(h_ref[0], w_ref[...], preferred_element_type=f32)
    c = acc[:, :KV_LORA]
    y = c * lax.rsqrt(jnp.mean(c * c, axis=-1, keepdims=True) + EPS)
    ckv_ref[0] = y * g_ref[...]
    kr = acc[:, KV_LORA:]
    kr_ref[0] = kr
    krr_ref[0] = _rope128(kr, cos_ref[...], sin_ref[...]).astype(krr_ref.dtype)


def inproj_ckr(h, w_ckr, kv_g, cos128, sin128):
    nb, lr, d = h.shape
    tm = _tile(lr, 1024)
    return pl.pallas_call(
        _ckr_kernel,
        out_shape=(jax.ShapeDtypeStruct((nb, lr, KV_LORA), f32),
                   jax.ShapeDtypeStruct((nb, lr, LANES), f32),
                   jax.ShapeDtypeStruct((nb, lr, LANES), bf16)),
        grid=(nb, lr // tm),
        in_specs=[pl.BlockSpec((1, tm, d), lambda b, i: (b, i, 0)),
                  pl.BlockSpec((d, CKR_W), lambda b, i: (0, 0)),
                  pl.BlockSpec((1, KV_LORA), lambda b, i: (0, 0)),
                  pl.BlockSpec((tm, LANES), lambda b, i: (i, 0)),
                  pl.BlockSpec((tm, LANES), lambda b, i: (i, 0))],
        out_specs=(pl.BlockSpec((1, tm, KV_LORA), lambda b, i: (b, i, 0)),
                   pl.BlockSpec((1, tm, LANES), lambda b, i: (b, i, 0)),
                   pl.BlockSpec((1, tm, LANES), lambda b, i: (b, i, 0))),
        compiler_params=_cparams("arbitrary", "arbitrary"),
        name="inproj_ckr",
    )(h, w_ckr, kv_g.reshape(1, KV_LORA), cos128, sin128)


def _mm_kernel(x_ref, w_ref, o_ref):
    o_ref[...] = jnp.dot(x_ref[...], w_ref[...], preferred_element_type=f32).astype(o_ref.dtype)


def matmul(x, w, out_dtype, tm_pref=512, tn_pref=4096):
    m, k = x.shape
    _, n = w.shape
    tm = _tile(m, tm_pref)
    tn = _tile(n, tn_pref)
    return pl.pallas_call(
        _mm_kernel,
        out_shape=jax.ShapeDtypeStruct((m, n), out_dtype),
        grid=(m // tm, n // tn),
        in_specs=[pl.BlockSpec((tm, k), lambda i, j: (i, 0)),
                  pl.BlockSpec((k, tn), lambda i, j: (0, j))],
        out_specs=pl.BlockSpec((tm, tn), lambda i, j: (i, j)),
        compiler_params=_cparams("arbitrary", "arbitrary"),
        name="matmul",
    )(x, w)


def _attn_kernel(qn_ref, qr_ref, kn_ref, v_ref, kr_ref, o_ref, kcat_ref, vt_ref, s_ref, *, nk, tk, nparts):
    h = pl.program_id(1)
    qi = pl.program_id(2)
    lk = kcat_ref.shape[0]

    @pl.when(qi == 0)
    def _():
        kcat_ref[:, :MLA_NOPE] = kn_ref[0]
        kcat_ref[:, MLA_NOPE:] = kr_ref[0]
        for c in range(nk):
            vt_ref[:, c * tk:(c + 1) * tk] = v_ref[0, c * tk:(c + 1) * tk, :].astype(f32).T.astype(bf16)

    qr = qr_ref[0]
    lane = lax.broadcasted_iota(jnp.int32, qr.shape, 1)
    qr = jnp.where((lane // MLA_ROPE) == (h % 2), qr, jnp.zeros_like(qr))
    qcat = jnp.concatenate([qn_ref[0], qr], axis=1)
    tq = qcat.shape[0]

    bounds = [((nk * i) // nparts) * tk for i in range(nparts + 1)]
    m = jnp.full((1, tq), -jnp.inf, f32)
    pending = None
    for i in range(nparts):
        r0, r1 = bounds[i], bounds[i + 1]
        s = lax.dot_general(kcat_ref[r0:r1, :], qcat, (((1,), (1,)), ((), ())), preferred_element_type=f32)
        s_ref[r0:r1, :] = s
        if pending is not None:
            m = jnp.maximum(m, jnp.max(pending, axis=0, keepdims=True))
        pending = s
    m = jnp.maximum(m, jnp.max(pending, axis=0, keepdims=True))

    l = jnp.zeros((1, tq), f32)
    acc = jnp.zeros((MLA_V, tq), f32)
    for c in range(nk):
        p = jnp.exp2(s_ref[c * tk:(c + 1) * tk, :] - m)
        l = l + jnp.sum(p, axis=0, keepdims=True)
        acc = acc + jnp.dot(vt_ref[:, c * tk:(c + 1) * tk], p.astype(bf16), preferred_element_type=f32)
    o_ref[0] = (acc * (1.0 / l)).T.astype(o_ref.dtype)


def _with_prev(kern, in_specs, args, prev):
    if prev is None:
        return kern, in_specs, args, {}
    n_in = len(args)

    def wrapped(*refs):
        return kern(*refs[:n_in], *refs[n_in + 1:])

    return wrapped, in_specs + [pl.BlockSpec(memory_space=pl.ANY)], args + [prev], {n_in: 0}


def attention(main, kv, krope, nb, lq, b_off, nb_total, prev):
    lk = kv.shape[1]
    tq = _tile(lq, 256)
    tk = _tile(lk, 256)
    qr_blk = OFF_QROPE // LANES
    nk = lk // tk
    nparts = min(4, nk)
    in_specs = [pl.BlockSpec((1, tq, MLA_NOPE), lambda b, h, i: (b + b_off, i, h)),
                pl.BlockSpec((1, tq, LANES), lambda b, h, i: (b + b_off, i, qr_blk + h // 2)),
                pl.BlockSpec((1, lk, MLA_NOPE), lambda b, h, i: (b, 0, 2 * h)),
                pl.BlockSpec((1, lk, MLA_V), lambda b, h, i: (b, 0, 2 * h + 1)),
                pl.BlockSpec((1, lk, LANES), lambda b, h, i: (b, 0, 0))]
    kern, in_specs, args, aliases = _with_prev(functools.partial(_attn_kernel, nk=nk, tk=tk, nparts=nparts),
                                               in_specs, [main, main, kv, kv, krope], prev)
    return pl.pallas_call(
        kern,
        out_shape=jax.ShapeDtypeStruct((nb_total, lq, MLA_HEADS * MLA_V), bf16),
        grid=(nb, MLA_HEADS, lq // tq),
        in_specs=in_specs,
        out_specs=pl.BlockSpec((1, tq, MLA_V), lambda b, h, i: (b + b_off, i, h)),
        scratch_shapes=[pltpu.VMEM((lk, MLA_NOPE + LANES), bf16), pltpu.VMEM((MLA_V, lk), bf16),
                        pltpu.VMEM((lk, tq), f32)],
        input_output_aliases=aliases,
        compiler_params=_cparams("arbitrary", "arbitrary", "arbitrary"),
        name="attention",
    )(*args)


def _dft_ch_kernel(x_ref, w_ref, o_ref):
    o_ref[0, 0] = jnp.dot(x_ref[0], w_ref[...], preferred_element_type=f32).astype(o_ref.dtype)


def dft_channels(main, w_ch, nb, ls, b_off):
    tm = _tile(ls, 1024)
    f_blk = OFF_F // FNET_GC
    return pl.pallas_call(
        _dft_ch_kernel,
        out_shape=jax.ShapeDtypeStruct((nb, 2, ls, FNET_GROUPS * FNET_GC), bf16),
        grid=(nb, ls // tm, FNET_GROUPS, 2),
        in_specs=[pl.BlockSpec((1, tm, FNET_GC), lambda b, i, g, p: (b + b_off, i, f_blk + g)),
                  pl.BlockSpec((FNET_GC, FNET_GC), lambda b, i, g, p: (0, p))],
        out_specs=pl.BlockSpec((1, 1, tm, FNET_GC), lambda b, i, g, p: (b, p, i, g)),
        compiler_params=_cparams("arbitrary", "arbitrary", "arbitrary", "arbitrary"),
        name="dft_channels",
    )(main, w_ch)


def _dft_seq_kernel(l_ref, r_ref, o_ref):
    o_ref[0] = jnp.dot(l_ref[...], r_ref[0], preferred_element_type=f32).astype(o_ref.dtype)


def dft_sequence(cs, ab, b_off, nb_total, prev):
    nb, k, w = ab.shape
    ls = cs.shape[0]
    tm = _tile(ls, 512)
    tn = _tile(w, 512)
    in_specs = [pl.BlockSpec((tm, k), lambda b, j, i: (i, 0)),
                pl.BlockSpec((1, k, tn), lambda b, j, i: (b, 0, j))]
    kern, in_specs, args, aliases = _with_prev(_dft_seq_kernel, in_specs, [cs, ab], prev)
    return pl.pallas_call(
        kern,
        out_shape=jax.ShapeDtypeStruct((nb_total, ls, w), bf16),
        grid=(nb, w // tn, ls // tm),
        in_specs=in_specs,
        out_specs=pl.BlockSpec((1, tm, tn), lambda b, j, i: (b + b_off, i, j)),
        input_output_aliases=aliases,
        compiler_params=_cparams("arbitrary", "arbitrary", "arbitrary"),
        name="dft_sequence",
    )(*args)


def _dft_mats(n):
    idx = jnp.arange(n, dtype=jnp.int32)
    ang = ((idx[:, None] * idx[None, :]) % n).astype(f32) * (2.0 * math.pi / n)
    s = n ** -0.5
    return jnp.cos(ang) * s, jnp.sin(ang) * s


def _log_sigmoid(x):
    return jnp.minimum(x, 0.0) - jnp.log1p(jnp.exp(-jnp.abs(x)))


def _ret_kernel(*refs, nc, use_s0):
    if use_s0:
        dl_ref, q_ref, k_ref, v_ref, g_ref, gn_ref, s0_ref, o_ref, sfin_ref, of_ref, ob_ref = refs
    else:
        dl_ref, q_ref, k_ref, v_ref, g_ref, gn_ref, o_ref, sfin_ref, of_ref, ob_ref = refs
        s0_ref = None
    C = RET_CHUNK
    lgf = _log_sigmoid(dl_ref[0, 0])[0:1, :]
    lgb = _log_sigmoid(dl_ref[1, 0])[0:1, :]
    lgf2 = jnp.concatenate([lgf, lgf], axis=1)
    lgb2 = jnp.concatenate([lgb, lgb], axis=1)
    lgf_c = jnp.concatenate([lgf] * (C // LANES), axis=1)
    lgb_c = jnp.concatenate([lgb] * (C // LANES), axis=1)
    rows = lax.broadcasted_iota(jnp.int32, (C, C), 0)
    cols = lax.broadcasted_iota(jnp.int32, (C, C), 1)
    rel = (rows - cols).astype(f32)
    d_f = jnp.where(rel >= 0, jnp.exp(jnp.maximum(rel, 0.0) * lgf_c), 0.0)
    d_b = jnp.where(rel <= 0, jnp.exp(jnp.maximum(-rel, 0.0) * lgb_c), 0.0)
    r_v = lax.broadcasted_iota(jnp.int32, (C, RET_DV), 0).astype(f32)
    r_k = lax.broadcasted_iota(jnp.int32, (C, RET_DK), 0).astype(f32)
    qdec_f = jnp.exp((r_v + 1.0) * lgf2)
    qdec_b = jnp.exp((C - r_v) * lgb2)
    kdec_f = jnp.exp((C - 1.0 - r_k) * lgf)
    kdec_b = jnp.exp(r_k * lgb)
    cdec_f = jnp.exp(C * lgf2)
    cdec_b = jnp.exp(C * lgb2)

    def chunk(c, s, dmat, qdec, kdec, cdec):
        r = pl.multiple_of(c * C, C)
        q = q_ref[0, pl.ds(r, C), :]
        k = k_ref[0, pl.ds(r, C), :]
        v = v_ref[0, pl.ds(r, C), :]
        inner = lax.dot_general(q, k, (((1,), (1,)), ((), ())), preferred_element_type=f32) * dmat
        o = (jnp.dot(inner.astype(bf16), v, preferred_element_type=f32)
             + jnp.dot(q, s.astype(bf16), preferred_element_type=f32) * qdec)
        kd = (k.astype(f32) * kdec).T.astype(bf16)
        s_new = s * cdec + jnp.dot(kd, v, preferred_element_type=f32)
        return r, o, s_new

    def body(c, carry):
        sf, sb = carry
        r, o, sf = chunk(c, sf, d_f, qdec_f, kdec_f, cdec_f)
        of_ref[pl.ds(r, C), :] = o
        r, o, sb = chunk(nc - 1 - c, sb, d_b, qdec_b, kdec_b, cdec_b)
        ob_ref[pl.ds(r, C), :] = o
        return sf, sb

    if use_s0:
        init = (s0_ref[0, 0, 0], s0_ref[0, 1, 0])
    else:
        init = (jnp.zeros((RET_DK, RET_DV), f32), jnp.zeros((RET_DK, RET_DV), f32))
    sf, sb = lax.fori_loop(0, nc, body, init, unroll=2)
    sfin_ref[0, 0, 0] = sf
    sfin_ref[0, 1, 0] = sb

    gn = gn_ref[...]

    def norm_body(c, carry):
        r = pl.multiple_of(c * C, C)
        o = of_ref[pl.ds(r, C), :] + ob_ref[pl.ds(r, C), :]
        mu = jnp.mean(o, axis=-1, keepdims=True)
        d = o - mu
        y = d * lax.rsqrt(jnp.mean(d * d, axis=-1, keepdims=True) + GN_EPS) * gn
        o_ref[0, pl.ds(r, C), :] = (y * g_ref[0, pl.ds(r, C), :].astype(f32)).astype(o_ref.dtype)
        return carry

    lax.fori_loop(0, nc, norm_body, 0, unroll=2)


def retention(main, decay_logit, gn_g, s0, nb, ls, b_off, nb_total, prev):
    use_s0 = s0 is not None
    nc = ls // RET_CHUNK
    dl = jnp.broadcast_to(decay_logit.astype(f32)[:, :, None, None], (2, RET_HEADS, 8, LANES))
    in_specs = [pl.BlockSpec((2, 1, 8, LANES), lambda b, h: (0, h, 0, 0)),
                pl.BlockSpec((1, ls, RET_DK), lambda b, h: (b + b_off, 0, OFF_RQ // RET_DK + h)),
                pl.BlockSpec((1, ls, RET_DK), lambda b, h: (b + b_off, 0, OFF_RK // RET_DK + h)),
                pl.BlockSpec((1, ls, RET_DV), lambda b, h: (b + b_off, 0, OFF_RV // RET_DV + h)),
                pl.BlockSpec((1, ls, RET_DV), lambda b, h: (b + b_off, 0, OFF_RG // RET_DV + h)),
                pl.BlockSpec((1, RET_DV), lambda b, h: (0, h))]
    args = [dl, main, main, main, main, gn_g.reshape(1, RET_HEADS * RET_DV)]
    if use_s0:
        in_specs.append(pl.BlockSpec((1, 2, 1, RET_DK, RET_DV), lambda b, h: (b, 0, h, 0, 0)))
        args.append(s0)
    kern, in_specs, args, aliases = _with_prev(functools.partial(_ret_kernel, nc=nc, use_s0=use_s0),
                                               in_specs, args, prev)
    return pl.pallas_call(
        kern,
        out_shape=(jax.ShapeDtypeStruct((nb_total, ls, RET_HEADS * RET_DV), bf16),
                   jax.ShapeDtypeStruct((nb, 2, RET_HEADS, RET_DK, RET_DV), f32)),
        grid=(nb, RET_HEADS),
        in_specs=in_specs,
        out_specs=(pl.BlockSpec((1, ls, RET_DV), lambda b, h: (b + b_off, 0, h)),
                   pl.BlockSpec((1, 2, 1, RET_DK, RET_DV), lambda b, h: (b, 0, h, 0, 0))),
        scratch_shapes=[pltpu.VMEM((ls, RET_DV), f32), pltpu.VMEM((ls, RET_DV), f32)],
        input_output_aliases=aliases,
        compiler_params=_cparams("arbitrary", "arbitrary"),
        name="retention",
    )(*args)


def _merge_kernel(a_ref, f_ref, r_ref, wa_ref, wf_ref, wr_ref, ga_ref, gf_ref, gr_ref, o_ref):
    m = ga_ref[...].astype(f32) * jnp.dot(a_ref[...], wa_ref[...], preferred_element_type=f32)
    m += gf_ref[...].astype(f32) * jnp.dot(f_ref[...], wf_ref[...], preferred_element_type=f32)
    m += gr_ref[...].astype(f32) * jnp.dot(r_ref[...], wr_ref[...], preferred_element_type=f32)
    o_ref[...] = m.astype(o_ref.dtype)


def merge_branches(att, four, ret, wa, wf, wr, main2d):
    m, d = att.shape
    tm = _tile(m, 512)
    tn = 512
    gb = OFF_GATES // tn
    nd = d // tn
    x_spec = pl.BlockSpec((tm, d), lambda i, j: (i, 0))
    w_spec = pl.BlockSpec((d, tn), lambda i, j: (0, j))
    return pl.pallas_call(
        _merge_kernel,
        out_shape=jax.ShapeDtypeStruct((m, d), bf16),
        grid=(m // tm, nd),
        in_specs=[x_spec, x_spec, x_spec, w_spec, w_spec, w_spec,
                  pl.BlockSpec((tm, tn), lambda i, j: (i, gb + j)),
                  pl.BlockSpec((tm, tn), lambda i, j: (i, gb + nd + j)),
                  pl.BlockSpec((tm, tn), lambda i, j: (i, gb + 2 * nd + j))],
        out_specs=pl.BlockSpec((tm, tn), lambda i, j: (i, j)),
        compiler_params=_cparams("arbitrary", "arbitrary"),
        name="merge_branches",
    )(att, four, ret, wa, wf, wr, main2d, main2d, main2d)


def _proj_res_kernel(m_ref, w_ref, x_ref, g_ref, o_ref):
    y = jnp.dot(m_ref[0], w_ref[...], preferred_element_type=f32)
    o_ref[0] = x_ref[0] + g_ref[0] * y


def proj_residual(merged, w, x, gate):
    nb, lr, d = x.shape
    tm = _tile(lr, 1024)
    tn = 1024
    return pl.pallas_call(
        _proj_res_kernel,
        out_shape=jax.ShapeDtypeStruct((nb, lr, d), f32),
        grid=(nb, lr // tm, d // tn),
        in_specs=[pl.BlockSpec((1, tm, d), lambda b, i, j: (b, i, 0)),
                  pl.BlockSpec((d, tn), lambda b, i, j: (0, j)),
                  pl.BlockSpec((1, tm, tn), lambda b, i, j: (b, i, j)),
                  pl.BlockSpec((1, 1, tn), lambda b, i, j: (b, 0, j))],
        out_specs=pl.BlockSpec((1, tm, tn), lambda b, i, j: (b, i, j)),
        compiler_params=_cparams("arbitrary", "arbitrary", "arbitrary"),
        name="proj_residual",
    )(merged, w, x, gate)


def _ffn_kernel(h_ref, wg_ref, wu_ref, wd_ref, x_ref, g_ref, o_ref, acc_ref):
    j = pl.program_id(2)

    @pl.when(j == 0)
    def _():
        acc_ref[...] = jnp.zeros_like(acc_ref)

    h = h_ref[0]
    a = jnp.dot(h, wg_ref[...], preferred_element_type=f32)
    u = jnp.dot(h, wu_ref[...], preferred_element_type=f32)
    t = (_silu(a) * u).astype(bf16)
    acc_ref[...] += jnp.dot(t, wd_ref[...], preferred_element_type=f32)

    @pl.when(j == pl.num_programs(2) - 1)
    def _():
        o_ref[0] = x_ref[0] + g_ref[0] * acc_ref[...]


def ffn_dense(h, wg, wu, wd, x, gate):
    nb, lr, d = x.shape
    dff = wg.shape[1]
    tm = _tile(lr, 512)
    tf = _tile(dff, 512)
    return pl.pallas_call(
        _ffn_kernel,
        out_shape=jax.ShapeDtypeStruct((nb, lr, d), f32),
        grid=(nb, lr // tm, dff // tf),
        in_specs=[pl.BlockSpec((1, tm, d), lambda b, i, j: (b, i, 0)),
                  pl.BlockSpec((d, tf), lambda b, i, j: (0, j)),
                  pl.BlockSpec((d, tf), lambda b, i, j: (0, j)),
                  pl.BlockSpec((tf, d), lambda b, i, j: (j, 0)),
                  pl.BlockSpec((1, tm, d), lambda b, i, j: (b, i, 0)),
                  pl.BlockSpec((1, 1, d), lambda b, i, j: (b, 0, 0))],
        out_specs=pl.BlockSpec((1, tm, d), lambda b, i, j: (b, i, 0)),
        scratch_shapes=[pltpu.VMEM((tm, d), f32)],
        compiler_params=_cparams("arbitrary", "arbitrary", "arbitrary"),
        name="ffn_dense",
    )(h, wg, wu, wd, x, gate)


MOE_TM = 512
MOE_TT = 256


def _router_kernel(h_ref, w_ref, e_ref, p_ref):
    logits = jnp.dot(h_ref[...].astype(bf16), w_ref[...], preferred_element_type=f32)
    lane = lax.broadcasted_iota(jnp.int32, logits.shape, 1)
    lane_f = lane.astype(f32)
    neg = jnp.full_like(logits, -jnp.inf)
    l0 = jnp.where(lane < N_EXPERTS, logits, neg)
    v1 = jnp.max(l0, axis=1, keepdims=True)
    i1 = jnp.min(jnp.where(l0 == v1, lane_f, float(LANES)), axis=1, keepdims=True)
    l1 = jnp.where(lane_f == i1, neg, l0)
    v2 = jnp.max(l1, axis=1, keepdims=True)
    i2 = jnp.min(jnp.where(l1 == v2, lane_f, float(LANES)), axis=1, keepdims=True)
    e = jnp.exp(v2 - v1)
    den = 1.0 + e
    e_ref[...] = jnp.where(lane == 0, i1, jnp.where(lane == 1, i2, 0.0)).astype(jnp.int32)
    p_ref[...] = jnp.where(lane == 0, 1.0 / den, jnp.where(lane == 1, e / den, 0.0))


def moe_router(h2d, w_router_pad):
    n, d = h2d.shape
    tm = _tile(n, 512)
    return pl.pallas_call(
        _router_kernel,
        out_shape=(jax.ShapeDtypeStruct((n, LANES), jnp.int32), jax.ShapeDtypeStruct((n, LANES), f32)),
        grid=(n // tm,),
        in_specs=[pl.BlockSpec((tm, d), lambda i: (i, 0)),
                  pl.BlockSpec((d, LANES), lambda i: (0, 0))],
        out_specs=(pl.BlockSpec((tm, LANES), lambda i: (i, 0)), pl.BlockSpec((tm, LANES), lambda i: (i, 0))),
        compiler_params=_cparams("arbitrary"),
        name="moe_router",
    )(h2d, w_router_pad)


def _dispatch_kernel(pos_ref, h_ref, xs_in_ref, xs_ref, sem):
    del xs_in_ref
    tt = h_ref.shape[0]

    def row_copy(t, p):
        return pltpu.make_async_copy(h_ref.at[pl.ds(t, 1), :], xs_ref.at[pl.ds(p, 1), :], sem)

    def start(t, c):
        row_copy(t, pos_ref[0, 0, 2 * t]).start()
        row_copy(t, pos_ref[0, 0, 2 * t + 1]).start()
        return c

    lax.fori_loop(0, tt, start, 0)

    def wait(t, c):
        row_copy(0, 0).wait()
        row_copy(0, 0).wait()
        return c

    lax.fori_loop(0, tt, wait, 0)


def moe_dispatch(h2d, pos, p_rows):
    n, d = h2d.shape
    tt = _tile(n, MOE_TT)
    pos3 = pos.reshape(n // tt, 1, 2 * tt)
    xs0 = jnp.zeros((p_rows, d), h2d.dtype)
    return pl.pallas_call(
        _dispatch_kernel,
        out_shape=jax.ShapeDtypeStruct((p_rows, d), h2d.dtype),
        grid=(n // tt,),
        in_specs=[pl.BlockSpec((1, 1, 2 * tt), lambda i: (i, 0, 0), memory_space=pltpu.SMEM),
                  pl.BlockSpec((tt, d), lambda i: (i, 0)),
                  pl.BlockSpec(memory_space=pl.ANY)],
        out_specs=pl.BlockSpec(memory_space=pl.ANY),
        scratch_shapes=[pltpu.SemaphoreType.DMA(())],
        input_output_aliases={2: 0},
        compiler_params=_cparams("arbitrary"),
        name="moe_dispatch",
    )(pos3, h2d, xs0)


def _moe_ffn_kernel(be_ref, nu_ref, x_ref, wg_ref, wu_ref, wd_ref, y_ref, xb_ref, acc_ref):
    del be_ref
    i = pl.program_id(0)
    j = pl.program_id(1)
    active = i < nu_ref[0]

    @pl.when(jnp.logical_and(active, j == 0))
    def _():
        xb_ref[...] = x_ref[...].astype(bf16)
        acc_ref[...] = jnp.zeros_like(acc_ref)

    @pl.when(active)
    def _():
        xb = xb_ref[...]
        a = jnp.dot(xb, wg_ref[0], preferred_element_type=f32)
        u = jnp.dot(xb, wu_ref[0], preferred_element_type=f32)
        t = (_silu(a) * u).astype(bf16)
        acc_ref[...] += jnp.dot(t, wd_ref[0], preferred_element_type=f32)

    @pl.when(j == pl.num_programs(1) - 1)
    def _():
        y_ref[...] = jnp.where(active, acc_ref[...], 0.0)


def moe_ffn(xs, blk_expert, n_used, wg, wu, wd):
    p_rows, d = xs.shape
    dff = wg.shape[2]
    tm = MOE_TM
    tf = _tile(dff, 512)
    nf = dff // tf

    def jj(i, j, nu):
        return jnp.where(i < nu[0], j, nf - 1)

    grid_spec = pltpu.PrefetchScalarGridSpec(
        num_scalar_prefetch=2,
        grid=(p_rows // tm, nf),
        in_specs=[pl.BlockSpec((tm, d), lambda i, j, be, nu: (i, 0)),
                  pl.BlockSpec((1, d, tf), lambda i, j, be, nu: (be[i], 0, jj(i, j, nu))),
                  pl.BlockSpec((1, d, tf), lambda i, j, be, nu: (be[i], 0, jj(i, j, nu))),
                  pl.BlockSpec((1, tf, d), lambda i, j, be, nu: (be[i], jj(i, j, nu), 0))],
        out_specs=pl.BlockSpec((tm, d), lambda i, j, be, nu: (i, 0)),
        scratch_shapes=[pltpu.VMEM((tm, d), bf16), pltpu.VMEM((tm, d), f32)],
    )
    return pl.pallas_call(
        _moe_ffn_kernel,
        out_shape=jax.ShapeDtypeStruct((p_rows, d), f32),
        grid_spec=grid_spec,
        compiler_params=_cparams("arbitrary", "arbitrary"),
        name="moe_ffn",
    )(blk_expert, n_used, xs, wg, wu, wd)


def _combine_kernel(pos_ref, y_ref, p_ref, x_ref, g_ref, o_ref, ybuf_ref, sem):
    tt = x_ref.shape[1]

    def row_copy(p, dst):
        return pltpu.make_async_copy(y_ref.at[pl.ds(p, 1), :], ybuf_ref.at[pl.ds(dst, 1), :], sem)

    def start(t, c):
        row_copy(pos_ref[0, 0, 2 * t], t).start()
        row_copy(pos_ref[0, 0, 2 * t + 1], tt + t).start()
        return c

    lax.fori_loop(0, tt, start, 0)

    def wait(t, c):
        row_copy(0, 0).wait()
        row_copy(0, 0).wait()
        return c

    lax.fori_loop(0, tt, wait, 0)
    p = p_ref[0]
    mix = p[:, 0:1] * ybuf_ref[pl.ds(0, tt), :] + p[:, 1:2] * ybuf_ref[pl.ds(tt, tt), :]
    o_ref[0] = x_ref[0] + g_ref[0] * mix


def moe_combine(y, pos, probs, x, gate):
    nb, lr, d = x.shape
    tt = _tile(lr, MOE_TT)
    nt = lr // tt
    pos3 = pos.reshape(nb * nt, 1, 2 * tt)
    probs3 = probs.reshape(nb, lr, LANES)
    return pl.pallas_call(
        _combine_kernel,
        out_shape=jax.ShapeDtypeStruct((nb, lr, d), f32),
        grid=(nb, nt),
        in_specs=[pl.BlockSpec((1, 1, 2 * tt), lambda b, i: (b * nt + i, 0, 0), memory_space=pltpu.SMEM),
                  pl.BlockSpec(memory_space=pl.ANY),
                  pl.BlockSpec((1, tt, LANES), lambda b, i: (b, i, 0)),
                  pl.BlockSpec((1, tt, d), lambda b, i: (b, i, 0)),
                  pl.BlockSpec((1, 1, d), lambda b, i: (b, 0, 0))],
        out_specs=pl.BlockSpec((1, tt, d), lambda b, i: (b, i, 0)),
        scratch_shapes=[pltpu.VMEM((2 * tt, d), f32), pltpu.SemaphoreType.DMA(())],
        compiler_params=_cparams("arbitrary", "arbitrary"),
        name="moe_combine",
    )(pos3, y, probs3, x, gate)


def moe_layer(h, x, gate, w_router, wg, wu, wd):
    nb, lr, d = h.shape
    n = nb * lr
    h2d = h.reshape(n, d)
    w_r = jnp.zeros((d, LANES), bf16).at[:, :N_EXPERTS].set(w_router.astype(bf16))
    e_out, p_out = moe_router(h2d, w_r)
    e_flat = e_out[:, :2].reshape(-1)
    onehot = (e_flat[:, None] == jnp.arange(N_EXPERTS, dtype=jnp.int32)[None, :]).astype(jnp.int32)
    csum = jnp.cumsum(onehot, axis=0)
    counts = csum[-1]
    rank = jnp.sum(onehot * csum, axis=1) - 1
    padded = (counts + MOE_TM - 1) // MOE_TM * MOE_TM
    pad_ends = jnp.cumsum(padded)
    pad_starts = pad_ends - padded
    pos = jnp.sum(onehot * pad_starts[None, :], axis=1) + rank
    n_blocks = (2 * n) // MOE_TM + N_EXPERTS
    blk_start = jnp.arange(n_blocks, dtype=jnp.int32) * MOE_TM
    blk_expert = jnp.minimum(jnp.sum((blk_start[:, None] >= pad_ends[None, :]).astype(jnp.int32), axis=1),
                             N_EXPERTS - 1).astype(jnp.int32)
    n_used = (pad_ends[-1] // MOE_TM).astype(jnp.int32).reshape(1)
    xs = moe_dispatch(h2d, pos.astype(jnp.int32), n_blocks * MOE_TM)
    y = moe_ffn(xs, blk_expert, n_used, wg, wu, wd)
    return moe_combine(y, pos.astype(jnp.int32), p_out, x, gate)


def _rope_tables(n_tok):
    rows = n_tok // GRID_W
    r = jnp.repeat(jnp.arange(rows, dtype=f32), GRID_W)
    cl = jnp.tile(jnp.arange(GRID_W, dtype=f32), rows)
    n_freq = MLA_ROPE // 4
    inv = ROPE_BASE ** (-jnp.arange(n_freq, dtype=f32) / n_freq)
    ang = jnp.concatenate([r[:, None] * inv, cl[:, None] * inv], axis=-1)
    cos, sin = jnp.cos(ang), jnp.sin(ang)
    return jnp.tile(cos, (1, 4)), jnp.concatenate([-sin, sin, -sin, sin], axis=-1)


def _prep_in_weights(w_in_l):
    d = w_in_l.shape[0]
    nq = MLA_HEADS * (MLA_NOPE + MLA_ROPE)
    wq = w_in_l[:, :nq].reshape(d, MLA_HEADS, MLA_NOPE + MLA_ROPE)
    w_ckv = w_in_l[:, nq:nq + KV_LORA]
    w_kr = w_in_l[:, nq + KV_LORA:nq + KV_LORA + MLA_ROPE]
    rest = w_in_l[:, nq + KV_LORA + MLA_ROPE:]
    w_main = jnp.concatenate([wq[:, :, :MLA_NOPE].reshape(d, Q_NOPE_W), wq[:, :, MLA_NOPE:].reshape(d, Q_ROPE_W),
                              rest], axis=1).astype(bf16)
    w_ckr = jnp.concatenate([w_ckv, w_kr, w_kr], axis=1).astype(bf16)
    return w_main, w_ckr


def kernel(x_prompt, x_sample, c, cache_ckv, cache_krope, state_ret, c_ctx, w_ada, b_ada, norm1_g, w_in, kv_norm_g, w_ukv, ret_decay_logit, ret_gn_g, w_o_mla, w_o_fnet, w_o_ret, w_out, norm2_g, w_gate_dense, w_up_dense, w_down_dense, w_router, w_gate_moe, w_up_moe, w_down_moe, final_norm_g):
    nbc, seq, d = x_prompt.shape
    nbs, ls, _ = x_sample.shape
    depth = w_in.shape[0]
    lr = nbc * seq
    assert lr == ls, "unified row layout needs BATCH * SEQ == DEC_SEQ"
    nb = 1 + nbs
    n_rows = nb * lr

    x = jnp.concatenate([x_prompt.reshape(1, lr, d), x_sample], axis=0)
    cond = jnp.concatenate([c_ctx[None, :], c, jnp.zeros((16 - nb, d), f32)], axis=0)
    mod = adaln_all(cond, w_ada, b_ada)[:, :nb]
    cos128, sin128 = _rope_tables(ls)

    w_ch = jnp.concatenate(_dft_mats(FNET_GC), axis=1).astype(bf16)
    c_c, s_c = _dft_mats(seq)
    cs_ctx = jnp.concatenate([c_c, -s_c], axis=1).astype(bf16)
    c_s, s_s = _dft_mats(ls)
    cs_lat = jnp.concatenate([c_s, -s_s], axis=1).astype(bf16)

    ckv_list, krope_list, ret_list = [], [], []
    for l in range(depth):
        sh1, sc1, g1, sh2, sc2, g2 = [m.reshape(nb, 1, d) for m in jnp.split(mod[l], 6, axis=-1)]
        w_main, w_ckr = _prep_in_weights(w_in[l])
        h = norm_mod(x, norm1_g[l], sc1, sh1, bf16)
        main = inproj_main(h, w_main, cos128, sin128)
        ckv, kr, krr = inproj_ckr(h, w_ckr, kv_norm_g[l], cos128, sin128)
        w_ukv_b = w_ukv[l].astype(bf16)

        ckv_c = ckv[0].reshape(nbc, seq, KV_LORA)
        kr_c = kr[0].reshape(nbc, seq, LANES)
        kv_c = matmul(ckv_c.reshape(lr, KV_LORA).astype(bf16), w_ukv_b, bf16).reshape(nbc, seq, -1)
        main_c = main.reshape(nb * nbc, seq, N_MAIN)
        att_c = attention(main_c, kv_c, kr_c.astype(bf16), nbc, seq, 0, nb * nbc, None)
        ab_c = dft_channels(main_c, w_ch, nbc, seq, 0).reshape(nbc, 2 * seq, FNET_GROUPS * FNET_GC)
        four_c = dft_sequence(cs_ctx, ab_c, 0, nb * nbc, None)
        ret_c, sfin_c = retention(main_c, ret_decay_logit[l], ret_gn_g[l], None, nbc, seq, 0, nb * nbc, None)
        ckv_list.append(ckv_c)
        krope_list.append(kr_c[:, :, :MLA_ROPE])
        ret_list.append(sfin_c)

        ckv_keys = jnp.concatenate([ckv[1:], cache_ckv[:, l]], axis=1).astype(bf16)
        lk = ckv_keys.shape[1]
        kv_s = matmul(ckv_keys.reshape(nbs * lk, KV_LORA), w_ukv_b, bf16).reshape(nbs, lk, -1)
        ck = cache_krope[:, l].astype(bf16)
        kr_keys = jnp.concatenate([krr[1:], jnp.concatenate([ck, ck], axis=-1)], axis=1)
        att = attention(main, kv_s, kr_keys, nbs, ls, 1, nb, att_c.reshape(nb, ls, -1)).reshape(n_rows, -1)
        ab_s = dft_channels(main, w_ch, nbs, ls, 1).reshape(nbs, 2 * ls, FNET_GROUPS * FNET_GC)
        four = dft_sequence(cs_lat, ab_s, 1, nb, four_c.reshape(nb, ls, -1)).reshape(n_rows, -1)
        ret, _ = retention(main, ret_decay_logit[l], ret_gn_g[l], state_ret[:, l], nbs, ls, 1, nb,
                           ret_c.reshape(nb, ls, -1))
        ret = ret.reshape(n_rows, -1)

        merged = merge_branches(att, four, ret, w_o_mla[l].astype(bf16), w_o_fnet[l].astype(bf16),
                                w_o_ret[l].astype(bf16), main.reshape(n_rows, N_MAIN))
        x = proj_residual(merged.reshape(nb, lr, d), w_out[l].astype(bf16), x, g1)

        i = l // 2
        if l % 2 == 0:
            h2 = norm_mod(x, norm2_g[l], sc2, sh2, bf16)
            x = ffn_dense(h2, w_gate_dense[i].astype(bf16), w_up_dense[i].astype(bf16),
                          w_down_dense[i].astype(bf16), x, g2)
        else:
            h2 = norm_mod(x, norm2_g[l], sc2, sh2, f32)
            x = moe_layer(h2, x, g2, w_router[i], w_gate_moe[i].astype(bf16), w_up_moe[i].astype(bf16),
                          w_down_moe[i].astype(bf16))

    y_prompt = final_norm(x, final_norm_g, 0, 1).reshape(nbc, seq, d)
    y_sample = final_norm(x, final_norm_g, 1, nbs)
    new_ckv = jnp.stack(ckv_list, axis=1)
    new_krope = jnp.stack(krope_list, axis=1)
    new_ret = jnp.stack(ret_list, axis=1)
    return (y_prompt, y_sample, new_ckv, new_krope, new_ret)
```

```python
import functools
import math

import jax
import jax.numpy as jnp
from jax import lax
from jax.experimental import pallas as pl
from jax.experimental.pallas import tpu as pltpu

f32 = jnp.float32
bf16 = jnp.bfloat16

D_MODEL = 2048
GRID_W = 64
MLA_HEADS = 16
MLA_NOPE = 128
MLA_ROPE = 64
MLA_V = 128
KV_LORA = 512
ROPE_BASE = 10000.0
FNET_GROUPS = 4
FNET_GC = 512
RET_HEADS = 8
RET_DK = 128
RET_DV = 256
RET_CHUNK = 256
N_EXPERTS = 8
EPS = 1e-6
GN_EPS = 1e-5

LANES = 128
VMEM_LIMIT_BYTES = 56 * 1024 * 1024

Q_NOPE_W = MLA_HEADS * MLA_NOPE
Q_ROPE_W = MLA_HEADS * MLA_ROPE
OFF_Q = 0
OFF_QROPE = Q_NOPE_W
OFF_F = OFF_Q + Q_NOPE_W + Q_ROPE_W
OFF_RQ = OFF_F + FNET_GROUPS * FNET_GC
OFF_RK = OFF_RQ + RET_HEADS * RET_DK
OFF_RV = OFF_RK + RET_HEADS * RET_DK
OFF_RG = OFF_RV + RET_HEADS * RET_DV
OFF_GATES = OFF_RG + RET_HEADS * RET_DV
N_MAIN = OFF_GATES + 3 * D_MODEL
CKR_W = KV_LORA + LANES
Q_SCALE = (MLA_NOPE + MLA_ROPE) ** -0.5 * math.log2(math.e)


def _cparams(*sem):
    return pltpu.CompilerParams(dimension_semantics=sem, vmem_limit_bytes=VMEM_LIMIT_BYTES)


def _tile(n, pref):
    t = min(n, pref)
    while n % t:
        t -= 8
    return t


def _silu(x):
    return x * jax.nn.sigmoid(x)


def _rope128(x, cos, sin):
    lane = lax.broadcasted_iota(jnp.int32, x.shape, 1)
    first = (lane % MLA_ROPE) < (MLA_ROPE // 2)
    swapped = jnp.where(first, pltpu.roll(x, LANES - MLA_ROPE // 2, 1), pltpu.roll(x, MLA_ROPE // 2, 1))
    return x * cos + swapped * sin


def _adaln_kernel(c_ref, w_ref, b_ref, o_ref):
    s = _silu(c_ref[...]).astype(bf16)
    o_ref[0] = jnp.dot(s, w_ref[0].astype(bf16), preferred_element_type=f32) + b_ref[0]


def adaln_all(cond, w_ada, b_ada):
    depth, d, n = w_ada.shape
    r = cond.shape[0]
    tn = _tile(n, 1024)
    return pl.pallas_call(
        _adaln_kernel,
        out_shape=jax.ShapeDtypeStruct((depth, r, n), f32),
        grid=(depth, n // tn),
        in_specs=[pl.BlockSpec((r, d), lambda l, j: (0, 0)),
                  pl.BlockSpec((1, d, tn), lambda l, j: (l, 0, j)),
                  pl.BlockSpec((1, 1, tn), lambda l, j: (l, 0, j))],
        out_specs=pl.BlockSpec((1, r, tn), lambda l, j: (l, 0, j)),
        compiler_params=_cparams("arbitrary", "arbitrary"),
        name="adaln",
    )(cond, w_ada, b_ada.reshape(depth, 1, n))


def _norm_mod_kernel(x_ref, g_ref, sc_ref, sh_ref, o_ref):
    x = x_ref[0]
    y = x * lax.rsqrt(jnp.mean(x * x, axis=-1, keepdims=True) + EPS)
    y = y * g_ref[...]
    o_ref[0] = (y * (1.0 + sc_ref[0]) + sh_ref[0]).astype(o_ref.dtype)


def norm_mod(x, g, sc, sh, out_dtype):
    nb, lr, d = x.shape
    tm = _tile(lr, 512)
    return pl.pallas_call(
        _norm_mod_kernel,
        out_shape=jax.ShapeDtypeStruct((nb, lr, d), out_dtype),
        grid=(nb, lr // tm),
        in_specs=[pl.BlockSpec((1, tm, d), lambda b, i: (b, i, 0)),
                  pl.BlockSpec((1, d), lambda b, i: (0, 0)),
                  pl.BlockSpec((1, 1, d), lambda b, i: (b, 0, 0)),
                  pl.BlockSpec((1, 1, d), lambda b, i: (b, 0, 0))],
        out_specs=pl.BlockSpec((1, tm, d), lambda b, i: (b, i, 0)),
        compiler_params=_cparams("arbitrary", "arbitrary"),
        name="norm_mod",
    )(x, g.reshape(1, d), sc, sh)


def _final_norm_kernel(x_ref, g_ref, o_ref):
    x = x_ref[0]
    y = x * lax.rsqrt(jnp.mean(x * x, axis=-1, keepdims=True) + EPS)
    o_ref[0] = y * g_ref[...]


def final_norm(x, g, b_off, nb_out):
    _, lr, d = x.shape
    tm = _tile(lr, 512)
    return pl.pallas_call(
        _final_norm_kernel,
        out_shape=jax.ShapeDtypeStruct((nb_out, lr, d), f32),
        grid=(nb_out, lr // tm),
        in_specs=[pl.BlockSpec((1, tm, d), lambda b, i: (b + b_off, i, 0)),
                  pl.BlockSpec((1, d), lambda b, i: (0, 0))],
        out_specs=pl.BlockSpec((1, tm, d), lambda b, i: (b, i, 0)),
        compiler_params=_cparams("arbitrary", "arbitrary"),
        name="final_norm",
    )(x, g.reshape(1, d))


def _inproj_kernel(h_ref, w_ref, cos_ref, sin_ref, o_ref, *, tn):
    b = pl.program_id(0)
    j = pl.program_id(2)
    acc = jnp.dot(h_ref[0], w_ref[...], preferred_element_type=f32)
    j_rope = OFF_QROPE // tn
    j_rk = OFF_RK // tn
    j_rv = OFF_RV // tn
    j_rg = OFF_RG // tn
    j_gates = OFF_GATES // tn
    is_q = j < OFF_F // tn
    is_rope = jnp.logical_and(jnp.logical_and(j >= j_rope, is_q), b >= 1)
    is_qplain = jnp.logical_and(is_q, jnp.logical_not(is_rope))
    is_rk = jnp.logical_and(j >= j_rk, j < j_rv)
    is_silu = jnp.logical_and(j >= j_rg, j < j_gates)
    is_sig = j >= j_gates
    plain = jnp.logical_not(is_q | is_rk | is_silu | is_sig)

    @pl.when(plain)
    def _():
        o_ref[0] = acc.astype(o_ref.dtype)

    @pl.when(is_qplain)
    def _():
        o_ref[0] = (acc * Q_SCALE).astype(o_ref.dtype)

    @pl.when(is_rope)
    def _():
        cos = cos_ref[...] * Q_SCALE
        sin = sin_ref[...] * Q_SCALE
        for s in range(tn // LANES):
            sl = slice(s * LANES, (s + 1) * LANES)
            o_ref[0, :, sl] = _rope128(acc[:, sl], cos, sin).astype(o_ref.dtype)

    @pl.when(is_rk)
    def _():
        o_ref[0] = (acc * (RET_DK ** -0.5)).astype(o_ref.dtype)

    @pl.when(is_silu)
    def _():
        o_ref[0] = _silu(acc).astype(o_ref.dtype)

    @pl.when(is_sig)
    def _():
        o_ref[0] = jax.nn.sigmoid(acc).astype(o_ref.dtype)


def inproj_main(h, w_main, cos128, sin128):
    nb, lr, d = h.shape
    tm = _tile(lr, 1024)
    tn = 1024
    return pl.pallas_call(
        functools.partial(_inproj_kernel, tn=tn),
        out_shape=jax.ShapeDtypeStruct((nb, lr, N_MAIN), bf16),
        grid=(nb, lr // tm, N_MAIN // tn),
        in_specs=[pl.BlockSpec((1, tm, d), lambda b, i, j: (b, i, 0)),
                  pl.BlockSpec((d, tn), lambda b, i, j: (0, j)),
                  pl.BlockSpec((tm, LANES), lambda b, i, j: (i, 0)),
                  pl.BlockSpec((tm, LANES), lambda b, i, j: (i, 0))],
        out_specs=pl.BlockSpec((1, tm, tn), lambda b, i, j: (b, i, j)),
        compiler_params=_cparams("arbitrary", "arbitrary", "arbitrary"),
        name="inproj_main",
    )(h, w_main, cos128, sin128)


def _ckr_kernel(h_ref, w_ref, g_ref, cos_ref, sin_ref, ckv_ref, kr_ref, krr_ref):
    acc = jnp.dot(h_ref[0], w_ref[...], preferred_element_type=f32)
    c = acc[:, :KV_LORA]
    y = c * lax.rsqrt(jnp.mean(c * c, axis=-1, keepdims=True) + EPS)
    ckv_ref[0] = y * g_ref[...]
    kr = acc[:, KV_LORA:]
    kr_ref[0] = kr
    krr_ref[0] = _rope128(kr, cos_ref[...], sin_ref[...]).astype(krr_ref.dtype)


def inproj_ckr(h, w_ckr, kv_g, cos128, sin128):
    nb, lr, d = h.shape
    tm = _tile(lr, 1024)
    return pl.pallas_call(
        _ckr_kernel,
        out_shape=(jax.ShapeDtypeStruct((nb, lr, KV_LORA), f32),
                   jax.ShapeDtypeStruct((nb, lr, LANES), f32),
                   jax.ShapeDtypeStruct((nb, lr, LANES), bf16)),
        grid=(nb, lr // tm),
        in_specs=[pl.BlockSpec((1, tm, d), lambda b, i: (b, i, 0)),
                  pl.BlockSpec((d, CKR_W), lambda b, i: (0, 0)),
                  pl.BlockSpec((1, KV_LORA), lambda b, i: (0, 0)),
                  pl.BlockSpec((tm, LANES), lambda b, i: (i, 0)),
                  pl.BlockSpec((tm, LANES), lambda b, i: (i, 0))],
        out_specs=(pl.BlockSpec((1, tm, KV_LORA), lambda b, i: (b, i, 0)),
                   pl.BlockSpec((1, tm, LANES), lambda b, i: (b, i, 0)),
                   pl.BlockSpec((1, tm, LANES), lambda b, i: (b, i, 0))),
        compiler_params=_cparams("arbitrary", "arbitrary"),
        name="inproj_ckr",
    )(h, w_ckr, kv_g.reshape(1, KV_LORA), cos128, sin128)


def _mm_kernel(x_ref, w_ref, o_ref):
    o_ref[...] = jnp.dot(x_ref[...], w_ref[...], preferred_element_type=f32).astype(o_ref.dtype)


def matmul(x, w, out_dtype, tm_pref=512, tn_pref=4096):
    m, k = x.shape
    _, n = w.shape
    tm = _tile(m, tm_pref)
    tn = _tile(n, tn_pref)
    return pl.pallas_call(
        _mm_kernel,
        out_shape=jax.ShapeDtypeStruct((m, n), out_dtype),
        grid=(m // tm, n // tn),
        in_specs=[pl.BlockSpec((tm, k), lambda i, j: (i, 0)),
                  pl.BlockSpec((k, tn), lambda i, j: (0, j))],
        out_specs=pl.BlockSpec((tm, tn), lambda i, j: (i, j)),
        compiler_params=_cparams("arbitrary", "arbitrary"),
        name="matmul",
    )(x, w)


def _attn_single_kernel(qn_ref, qr_ref, kn_ref, v_ref, kr_ref, o_ref, kcat_ref, vt_ref, s_ref, *, nk, tk, nparts):
    h = pl.program_id(1)
    qi = pl.program_id(2)
    lk = kcat_ref.shape[0]

    @pl.when(qi == 0)
    def _():
        kcat_ref[:, :MLA_NOPE] = kn_ref[0]
        kcat_ref[:, MLA_NOPE:] = kr_ref[0]
        for c in range(nk):
            vt_ref[:, c * tk:(c + 1) * tk] = v_ref[0, c * tk:(c + 1) * tk, :].astype(f32).T.astype(bf16)

    qr = qr_ref[0]
    lane = lax.broadcasted_iota(jnp.int32, qr.shape, 1)
    qr = jnp.where((lane // MLA_ROPE) == (h % 2), qr, jnp.zeros_like(qr))
    qcat = jnp.concatenate([qn_ref[0], qr], axis=1)
    tq = qcat.shape[0]

    bounds = [((nk * i) // nparts) * tk for i in range(nparts + 1)]
    m = jnp.full((1, tq), -jnp.inf, f32)
    pending = None
    for i in range(nparts):
        r0, r1 = bounds[i], bounds[i + 1]
        s = lax.dot_general(kcat_ref[r0:r1, :], qcat, (((1,), (1,)), ((), ())), preferred_element_type=f32)
        s_ref[r0:r1, :] = s
        if pending is not None:
            m = jnp.maximum(m, jnp.max(pending, axis=0, keepdims=True))
        pending = s
    m = jnp.maximum(m, jnp.max(pending, axis=0, keepdims=True))

    l = jnp.zeros((1, tq), f32)
    acc = jnp.zeros((MLA_V, tq), f32)
    for c in range(nk):
        p = jnp.exp2(s_ref[c * tk:(c + 1) * tk, :] - m)
        l = l + jnp.sum(p, axis=0, keepdims=True)
        acc = acc + jnp.dot(vt_ref[:, c * tk:(c + 1) * tk], p.astype(bf16), preferred_element_type=f32)
    o_ref[0] = (acc * (1.0 / l)).T.astype(o_ref.dtype)


def _attn_kernel(qn_ref, qr_ref, kn_ref, v_ref, kr_ref, o_ref, kcat_ref, vt_ref, sa_ref, sb_ref, ma_ref, mb_ref,
                 *, nk, tk, nparts):
    h = pl.program_id(1)
    i = pl.program_id(2)

    @pl.when(i == 0)
    def _():
        kcat_ref[:, :MLA_NOPE] = kn_ref[0]
        kcat_ref[:, MLA_NOPE:] = kr_ref[0]
        for c in range(nk):
            vt_ref[:, c * tk:(c + 1) * tk] = v_ref[0, c * tk:(c + 1) * tk, :].astype(f32).T.astype(bf16)
        sb_ref[...] = jnp.zeros(sb_ref.shape, f32)
        mb_ref[...] = jnp.zeros(mb_ref.shape, f32)

    def step(s_new, s_old, m_new_ref, m_old_ref):
        qr = qr_ref[0]
        lane = lax.broadcasted_iota(jnp.int32, qr.shape, 1)
        qr = jnp.where((lane // MLA_ROPE) == (h % 2), qr, jnp.zeros_like(qr))
        qcat = jnp.concatenate([qn_ref[0], qr], axis=1)
        tq = qcat.shape[0]
        m_old = m_old_ref[0:1, :]
        bounds = [(nk * g) // nparts for g in range(nparts + 1)]
        m = jnp.full((1, tq), -jnp.inf, f32)
        l = jnp.zeros((1, tq), f32)
        acc = jnp.zeros((MLA_V, tq), f32)
        for g in range(nparts):
            r0, r1 = bounds[g] * tk, bounds[g + 1] * tk
            s = lax.dot_general(kcat_ref[r0:r1, :], qcat, (((1,), (1,)), ((), ())), preferred_element_type=f32)
            s_new[r0:r1, :] = s
            m = jnp.maximum(m, jnp.max(s, axis=0, keepdims=True))
            for c in range(bounds[g], bounds[g + 1]):
                p = jnp.exp2(s_old[c * tk:(c + 1) * tk, :] - m_old)
                l = l + jnp.sum(p, axis=0, keepdims=True)
                acc = acc + jnp.dot(vt_ref[:, c * tk:(c + 1) * tk], p.astype(bf16), preferred_element_type=f32)
        m_new_ref[...] = jnp.broadcast_to(m, m_new_ref.shape)
        o_ref[0] = (acc * (1.0 / l)).T.astype(o_ref.dtype)

    @pl.when(i % 2 == 0)
    def _():
        step(sa_ref, sb_ref, ma_ref, mb_ref)

    @pl.when(i % 2 == 1)
    def _():
        step(sb_ref, sa_ref, mb_ref, ma_ref)


def _with_prev(kern, in_specs, args, prev):
    if prev is None:
        return kern, in_specs, args, {}
    n_in = len(args)

    def wrapped(*refs):
        return kern(*refs[:n_in], *refs[n_in + 1:])

    return wrapped, in_specs + [pl.BlockSpec(memory_space=pl.ANY)], args + [prev], {n_in: 0}


def attention(main, kv, krope, nb, lq, b_off, nb_total, prev):
    lk = kv.shape[1]
    tq = _tile(lq, 256)
    tk = _tile(lk, 256)
    nq = lq // tq
    qr_blk = OFF_QROPE // LANES
    nk = lk // tk
    nparts = min(4, nk)
    skewed = nq > 1
    if skewed:
        steps = nq + 1
        q_idx = lambda i: jnp.minimum(i, nq - 1)
        o_idx = lambda i: jnp.maximum(i - 1, 0)
        kern = functools.partial(_attn_kernel, nk=nk, tk=tk, nparts=nparts)
        scratch = [pltpu.VMEM((lk, MLA_NOPE + LANES), bf16), pltpu.VMEM((MLA_V, lk), bf16),
                   pltpu.VMEM((lk, tq), f32), pltpu.VMEM((lk, tq), f32),
                   pltpu.VMEM((8, tq), f32), pltpu.VMEM((8, tq), f32)]
    else:
        steps = nq
        q_idx = lambda i: i
        o_idx = lambda i: i
        kern = functools.partial(_attn_single_kernel, nk=nk, tk=tk, nparts=nparts)
        scratch = [pltpu.VMEM((lk, MLA_NOPE + LANES), bf16), pltpu.VMEM((MLA_V, lk), bf16),
                   pltpu.VMEM((lk, tq), f32)]
    in_specs = [pl.BlockSpec((1, tq, MLA_NOPE), lambda b, h, i: (b + b_off, q_idx(i), h)),
                pl.BlockSpec((1, tq, LANES), lambda b, h, i: (b + b_off, q_idx(i), qr_blk + h // 2)),
                pl.BlockSpec((1, lk, MLA_NOPE), lambda b, h, i: (b, 0, 2 * h)),
                pl.BlockSpec((1, lk, MLA_V), lambda b, h, i: (b, 0, 2 * h + 1)),
                pl.BlockSpec((1, lk, LANES), lambda b, h, i: (b, 0, 0))]
    kern, in_specs, args, aliases = _with_prev(kern, in_specs, [main, main, kv, kv, krope], prev)
    return pl.pallas_call(
        kern,
        out_shape=jax.ShapeDtypeStruct((nb_total, lq, MLA_HEADS * MLA_V), bf16),
        grid=(nb, MLA_HEADS, steps),
        in_specs=in_specs,
        out_specs=pl.BlockSpec((1, tq, MLA_V), lambda b, h, i: (b + b_off, o_idx(i), h)),
        scratch_shapes=scratch,
        input_output_aliases=aliases,
        compiler_params=_cparams("arbitrary", "arbitrary", "arbitrary"),
        name="attention",
    )(*args)


def _dft_ch_kernel(x_ref, w_ref, o_ref):
    o_ref[0, 0] = jnp.dot(x_ref[0], w_ref[...], preferred_element_type=f32).astype(o_ref.dtype)


def dft_channels(main, w_ch, nb, ls, b_off):
    tm = _tile(ls, 1024)
    f_blk = OFF_F // FNET_GC
    return pl.pallas_call(
        _dft_ch_kernel,
        out_shape=jax.ShapeDtypeStruct((nb, 2, ls, FNET_GROUPS * FNET_GC), bf16),
        grid=(nb, ls // tm, FNET_GROUPS, 2),
        in_specs=[pl.BlockSpec((1, tm, FNET_GC), lambda b, i, g, p: (b + b_off, i, f_blk + g)),
                  pl.BlockSpec((FNET_GC, FNET_GC), lambda b, i, g, p: (0, p))],
        out_specs=pl.BlockSpec((1, 1, tm, FNET_GC), lambda b, i, g, p: (b, p, i, g)),
        compiler_params=_cparams("arbitrary", "arbitrary", "arbitrary", "arbitrary"),
        name="dft_channels",
    )(main, w_ch)


def _dft_seq_kernel(l_ref, r_ref, o_ref):
    o_ref[0] = jnp.dot(l_ref[...], r_ref[0], preferred_element_type=f32).astype(o_ref.dtype)


def dft_sequence(cs, ab, b_off, nb_total, prev):
    nb, k, w = ab.shape
    ls = cs.shape[0]
    tm = _tile(ls, 512)
    tn = _tile(w, 512)
    in_specs = [pl.BlockSpec((tm, k), lambda b, j, i: (i, 0)),
                pl.BlockSpec((1, k, tn), lambda b, j, i: (b, 0, j))]
    kern, in_specs, args, aliases = _with_prev(_dft_seq_kernel, in_specs, [cs, ab], prev)
    return pl.pallas_call(
        kern,
        out_shape=jax.ShapeDtypeStruct((nb_total, ls, w), bf16),
        grid=(nb, w // tn, ls // tm),
        in_specs=in_specs,
        out_specs=pl.BlockSpec((1, tm, tn), lambda b, j, i: (b + b_off, i, j)),
        input_output_aliases=aliases,
        compiler_params=_cparams("arbitrary", "arbitrary", "arbitrary"),
        name="dft_sequence",
    )(*args)


def _dft_mats(n):
    idx = jnp.arange(n, dtype=jnp.int32)
    ang = ((idx[:, None] * idx[None, :]) % n).astype(f32) * (2.0 * math.pi / n)
    s = n ** -0.5
    return jnp.cos(ang) * s, jnp.sin(ang) * s


def _log_sigmoid(x):
    return jnp.minimum(x, 0.0) - jnp.log1p(jnp.exp(-jnp.abs(x)))


def _ret_kernel(*refs, nc, use_s0):
    if use_s0:
        dl_ref, q_ref, k_ref, v_ref, g_ref, gn_ref, s0_ref, o_ref, sfin_ref, of_ref, ob_ref = refs
    else:
        dl_ref, q_ref, k_ref, v_ref, g_ref, gn_ref, o_ref, sfin_ref, of_ref, ob_ref = refs
        s0_ref = None
    C = RET_CHUNK
    lgf = _log_sigmoid(dl_ref[0, 0])[0:1, :]
    lgb = _log_sigmoid(dl_ref[1, 0])[0:1, :]
    lgf2 = jnp.concatenate([lgf, lgf], axis=1)
    lgb2 = jnp.concatenate([lgb, lgb], axis=1)
    lgf_c = jnp.concatenate([lgf] * (C // LANES), axis=1)
    lgb_c = jnp.concatenate([lgb] * (C // LANES), axis=1)
    rows = lax.broadcasted_iota(jnp.int32, (C, C), 0)
    cols = lax.broadcasted_iota(jnp.int32, (C, C), 1)
    rel = (rows - cols).astype(f32)
    d_f = jnp.where(rel >= 0, jnp.exp(jnp.maximum(rel, 0.0) * lgf_c), 0.0)
    d_b = jnp.where(rel <= 0, jnp.exp(jnp.maximum(-rel, 0.0) * lgb_c), 0.0)
    r_v = lax.broadcasted_iota(jnp.int32, (C, RET_DV), 0).astype(f32)
    r_k = lax.broadcasted_iota(jnp.int32, (C, RET_DK), 0).astype(f32)
    qdec_f = jnp.exp((r_v + 1.0) * lgf2)
    qdec_b = jnp.exp((C - r_v) * lgb2)
    kdec_f = jnp.exp((C - 1.0 - r_k) * lgf)
    kdec_b = jnp.exp(r_k * lgb)
    cdec_f = jnp.exp(C * lgf2)
    cdec_b = jnp.exp(C * lgb2)

    def chunk(c, s, dmat, qdec, kdec, cdec):
        r = pl.multiple_of(c * C, C)
        q = q_ref[0, pl.ds(r, C), :]
        k = k_ref[0, pl.ds(r, C), :]
        v = v_ref[0, pl.ds(r, C), :]
        inner = lax.dot_general(q, k, (((1,), (1,)), ((), ())), preferred_element_type=f32) * dmat
        o = (jnp.dot(inner.astype(bf16), v, preferred_element_type=f32)
             + jnp.dot(q, s.astype(bf16), preferred_element_type=f32) * qdec)
        kd = (k.astype(f32) * kdec).T.astype(bf16)
        s_new = s * cdec + jnp.dot(kd, v, preferred_element_type=f32)
        return r, o, s_new

    def body(c, carry):
        sf, sb = carry
        r, o, sf = chunk(c, sf, d_f, qdec_f, kdec_f, cdec_f)
        of_ref[pl.ds(r, C), :] = o
        r, o, sb = chunk(nc - 1 - c, sb, d_b, qdec_b, kdec_b, cdec_b)
        ob_ref[pl.ds(r, C), :] = o
        return sf, sb

    if use_s0:
        init = (s0_ref[0, 0, 0], s0_ref[0, 1, 0])
    else:
        init = (jnp.zeros((RET_DK, RET_DV), f32), jnp.zeros((RET_DK, RET_DV), f32))
    sf, sb = lax.fori_loop(0, nc, body, init, unroll=2)
    sfin_ref[0, 0, 0] = sf
    sfin_ref[0, 1, 0] = sb

    gn = gn_ref[...]

    def norm_body(c, carry):
        r = pl.multiple_of(c * C, C)
        o = of_ref[pl.ds(r, C), :] + ob_ref[pl.ds(r, C), :]
        mu = jnp.mean(o, axis=-1, keepdims=True)
        d = o - mu
        y = d * lax.rsqrt(jnp.mean(d * d, axis=-1, keepdims=True) + GN_EPS) * gn
        o_ref[0, pl.ds(r, C), :] = (y * g_ref[0, pl.ds(r, C), :].astype(f32)).astype(o_ref.dtype)
        return carry

    lax.fori_loop(0, nc, norm_body, 0, unroll=2)


def retention(main, decay_logit, gn_g, s0, nb, ls, b_off, nb_total, prev):
    use_s0 = s0 is not None
    nc = ls // RET_CHUNK
    dl = jnp.broadcast_to(decay_logit.astype(f32)[:, :, None, None], (2, RET_HEADS, 8, LANES))
    in_specs = [pl.BlockSpec((2, 1, 8, LANES), lambda b, h: (0, h, 0, 0)),
                pl.BlockSpec((1, ls, RET_DK), lambda b, h: (b + b_off, 0, OFF_RQ // RET_DK + h)),
                pl.BlockSpec((1, ls, RET_DK), lambda b, h: (b + b_off, 0, OFF_RK // RET_DK + h)),
                pl.BlockSpec((1, ls, RET_DV), lambda b, h: (b + b_off, 0, OFF_RV // RET_DV + h)),
                pl.BlockSpec((1, ls, RET_DV), lambda b, h: (b + b_off, 0, OFF_RG // RET_DV + h)),
                pl.BlockSpec((1, RET_DV), lambda b, h: (0, h))]
    args = [dl, main, main, main, main, gn_g.reshape(1, RET_HEADS * RET_DV)]
    if use_s0:
        in_specs.append(pl.BlockSpec((1, 2, 1, RET_DK, RET_DV), lambda b, h: (b, 0, h, 0, 0)))
        args.append(s0)
    kern, in_specs, args, aliases = _with_prev(functools.partial(_ret_kernel, nc=nc, use_s0=use_s0),
                                               in_specs, args, prev)
    return pl.pallas_call(
        kern,
        out_shape=(jax.ShapeDtypeStruct((nb_total, ls, RET_HEADS * RET_DV), bf16),
                   jax.ShapeDtypeStruct((nb, 2, RET_HEADS, RET_DK, RET_DV), f32)),
        grid=(nb, RET_HEADS),
        in_specs=in_specs,
        out_specs=(pl.BlockSpec((1, ls, RET_DV), lambda b, h: (b + b_off, 0, h)),
                   pl.BlockSpec((1, 2, 1, RET_DK, RET_DV), lambda b, h: (b, 0, h, 0, 0))),
        scratch_shapes=[pltpu.VMEM((ls, RET_DV), f32), pltpu.VMEM((ls, RET_DV), f32)],
        input_output_aliases=aliases,
        compiler_params=_cparams("arbitrary", "arbitrary"),
        name="retention",
    )(*args)


def _merge_kernel(a_ref, f_ref, r_ref, wa_ref, wf_ref, wr_ref, ga_ref, gf_ref, gr_ref, o_ref):
    m = ga_ref[...].astype(f32) * jnp.dot(a_ref[...], wa_ref[...], preferred_element_type=f32)
    m += gf_ref[...].astype(f32) * jnp.dot(f_ref[...], wf_ref[...], preferred_element_type=f32)
    m += gr_ref[...].astype(f32) * jnp.dot(r_ref[...], wr_ref[...], preferred_element_type=f32)
    o_ref[...] = m.astype(o_ref.dtype)


def merge_branches(att, four, ret, wa, wf, wr, main2d):
    m, d = att.shape
    tm = _tile(m, 512)
    tn = 512
    gb = OFF_GATES // tn
    nd = d // tn
    x_spec = pl.BlockSpec((tm, d), lambda i, j: (i, 0))
    w_spec = pl.BlockSpec((d, tn), lambda i, j: (0, j))
    return pl.pallas_call(
        _merge_kernel,
        out_shape=jax.ShapeDtypeStruct((m, d), bf16),
        grid=(m // tm, nd),
        in_specs=[x_spec, x_spec, x_spec, w_spec, w_spec, w_spec,
                  pl.BlockSpec((tm, tn), lambda i, j: (i, gb + j)),
                  pl.BlockSpec((tm, tn), lambda i, j: (i, gb + nd + j)),
                  pl.BlockSpec((tm, tn), lambda i, j: (i, gb + 2 * nd + j))],
        out_specs=pl.BlockSpec((tm, tn), lambda i, j: (i, j)),
        compiler_params=_cparams("arbitrary", "arbitrary"),
        name="merge_branches",
    )(att, four, ret, wa, wf, wr, main2d, main2d, main2d)


def _proj_res_kernel(m_ref, w_ref, x_ref, g_ref, o_ref):
    y = jnp.dot(m_ref[0], w_ref[...], preferred_element_type=f32)
    o_ref[0] = x_ref[0] + g_ref[0] * y


def proj_residual(merged, w, x, gate):
    nb, lr, d = x.shape
    tm = _tile(lr, 1024)
    tn = 1024
    return pl.pallas_call(
        _proj_res_kernel,
        out_shape=jax.ShapeDtypeStruct((nb, lr, d), f32),
        grid=(nb, lr // tm, d // tn),
        in_specs=[pl.BlockSpec((1, tm, d), lambda b, i, j: (b, i, 0)),
                  pl.BlockSpec((d, tn), lambda b, i, j: (0, j)),
                  pl.BlockSpec((1, tm, tn), lambda b, i, j: (b, i, j)),
                  pl.BlockSpec((1, 1, tn), lambda b, i, j: (b, 0, j))],
        out_specs=pl.BlockSpec((1, tm, tn), lambda b, i, j: (b, i, j)),
        compiler_params=_cparams("arbitrary", "arbitrary", "arbitrary"),
        name="proj_residual",
    )(merged, w, x, gate)


def _ffn_kernel(h_ref, wg_ref, wu_ref, wd_ref, x_ref, g_ref, o_ref, acc_ref):
    j = pl.program_id(2)

    @pl.when(j == 0)
    def _():
        acc_ref[...] = jnp.zeros_like(acc_ref)

    h = h_ref[0]
    a = jnp.dot(h, wg_ref[...], preferred_element_type=f32)
    u = jnp.dot(h, wu_ref[...], preferred_element_type=f32)
    t = (_silu(a) * u).astype(bf16)
    acc_ref[...] += jnp.dot(t, wd_ref[...], preferred_element_type=f32)

    @pl.when(j == pl.num_programs(2) - 1)
    def _():
        o_ref[0] = x_ref[0] + g_ref[0] * acc_ref[...]


def ffn_dense(h, wg, wu, wd, x, gate):
    nb, lr, d = x.shape
    dff = wg.shape[1]
    tm = _tile(lr, 512)
    tf = _tile(dff, 512)
    return pl.pallas_call(
        _ffn_kernel,
        out_shape=jax.ShapeDtypeStruct((nb, lr, d), f32),
        grid=(nb, lr // tm, dff // tf),
        in_specs=[pl.BlockSpec((1, tm, d), lambda b, i, j: (b, i, 0)),
                  pl.BlockSpec((d, tf), lambda b, i, j: (0, j)),
                  pl.BlockSpec((d, tf), lambda b, i, j: (0, j)),
                  pl.BlockSpec((tf, d), lambda b, i, j: (j, 0)),
                  pl.BlockSpec((1, tm, d), lambda b, i, j: (b, i, 0)),
                  pl.BlockSpec((1, 1, d), lambda b, i, j: (b, 0, 0))],
        out_specs=pl.BlockSpec((1, tm, d), lambda b, i, j: (b, i, 0)),
        scratch_shapes=[pltpu.VMEM((tm, d), f32)],
        compiler_params=_cparams("arbitrary", "arbitrary", "arbitrary"),
        name="ffn_dense",
    )(h, wg, wu, wd, x, gate)


MOE_TM = 512
MOE_TT = 256


def _router_kernel(h_ref, w_ref, e_ref, p_ref):
    logits = jnp.dot(h_ref[...].astype(bf16), w_ref[...], preferred_element_type=f32)
    lane = lax.broadcasted_iota(jnp.int32, logits.shape, 1)
    lane_f = lane.astype(f32)
    neg = jnp.full_like(logits, -jnp.inf)
    l0 = jnp.where(lane < N_EXPERTS, logits, neg)
    v1 = jnp.max(l0, axis=1, keepdims=True)
    i1 = jnp.min(jnp.where(l0 == v1, lane_f, float(LANES)), axis=1, keepdims=True)
    l1 = jnp.where(lane_f == i1, neg, l0)
    v2 = jnp.max(l1, axis=1, keepdims=True)
    i2 = jnp.min(jnp.where(l1 == v2, lane_f, float(LANES)), axis=1, keepdims=True)
    e = jnp.exp(v2 - v1)
    den = 1.0 + e
    e_ref[...] = jnp.where(lane == 0, i1, jnp.where(lane == 1, i2, 0.0)).astype(jnp.int32)
    p_ref[...] = jnp.where(lane == 0, 1.0 / den, jnp.where(lane == 1, e / den, 0.0))


def moe_router(h2d, w_router_pad):
    n, d = h2d.shape
    tm = _tile(n, 512)
    return pl.pallas_call(
        _router_kernel,
        out_shape=(jax.ShapeDtypeStruct((n, LANES), jnp.int32), jax.ShapeDtypeStruct((n, LANES), f32)),
        grid=(n // tm,),
        in_specs=[pl.BlockSpec((tm, d), lambda i: (i, 0)),
                  pl.BlockSpec((d, LANES), lambda i: (0, 0))],
        out_specs=(pl.BlockSpec((tm, LANES), lambda i: (i, 0)), pl.BlockSpec((tm, LANES), lambda i: (i, 0))),
        compiler_params=_cparams("arbitrary"),
        name="moe_router",
    )(h2d, w_router_pad)


def _dispatch_kernel(pos_ref, h_ref, xs_in_ref, xs_ref, sem):
    del xs_in_ref
    tt = h_ref.shape[0]

    def row_copy(t, p):
        return pltpu.make_async_copy(h_ref.at[pl.ds(t, 1), :], xs_ref.at[pl.ds(p, 1), :], sem)

    def start(t, c):
        row_copy(t, pos_ref[0, 0, 2 * t]).start()
        row_copy(t, pos_ref[0, 0, 2 * t + 1]).start()
        return c

    lax.fori_loop(0, tt, start, 0)

    def wait(t, c):
        row_copy(0, 0).wait()
        row_copy(0, 0).wait()
        return c

    lax.fori_loop(0, tt, wait, 0)


def moe_dispatch(h2d, pos, p_rows):
    n, d = h2d.shape
    tt = _tile(n, MOE_TT)
    pos3 = pos.reshape(n // tt, 1, 2 * tt)
    xs0 = jnp.zeros((p_rows, d), h2d.dtype)
    return pl.pallas_call(
        _dispatch_kernel,
        out_shape=jax.ShapeDtypeStruct((p_rows, d), h2d.dtype),
        grid=(n // tt,),
        in_specs=[pl.BlockSpec((1, 1, 2 * tt), lambda i: (i, 0, 0), memory_space=pltpu.SMEM),
                  pl.BlockSpec((tt, d), lambda i: (i, 0)),
                  pl.BlockSpec(memory_space=pl.ANY)],
        out_specs=pl.BlockSpec(memory_space=pl.ANY),
        scratch_shapes=[pltpu.SemaphoreType.DMA(())],
        input_output_aliases={2: 0},
        compiler_params=_cparams("arbitrary"),
        name="moe_dispatch",
    )(pos3, h2d, xs0)


def _moe_ffn_kernel(be_ref, nu_ref, x_ref, wg_ref, wu_ref, wd_ref, y_ref, xb_ref, acc_ref):
    del be_ref
    i = pl.program_id(0)
    j = pl.program_id(1)
    active = i < nu_ref[0]

    @pl.when(jnp.logical_and(active, j == 0))
    def _():
        xb_ref[...] = x_ref[...].astype(bf16)
        acc_ref[...] = jnp.zeros_like(acc_ref)

    @pl.when(active)
    def _():
        xb = xb_ref[...]
        a = jnp.dot(xb, wg_ref[0], preferred_element_type=f32)
        u = jnp.dot(xb, wu_ref[0], preferred_element_type=f32)
        t = (_silu(a) * u).astype(bf16)
        acc_ref[...] += jnp.dot(t, wd_ref[0], preferred_element_type=f32)

    @pl.when(j == pl.num_programs(1) - 1)
    def _():
        y_ref[...] = jnp.where(active, acc_ref[...], 0.0)


def moe_ffn(xs, blk_expert, n_used, wg, wu, wd):
    p_rows, d = xs.shape
    dff = wg.shape[2]
    tm = MOE_TM
    tf = _tile(dff, 512)
    nf = dff // tf

    def jj(i, j, nu):
        return jnp.where(i < nu[0], j, nf - 1)

    grid_spec = pltpu.PrefetchScalarGridSpec(
        num_scalar_prefetch=2,
        grid=(p_rows // tm, nf),
        in_specs=[pl.BlockSpec((tm, d), lambda i, j, be, nu: (i, 0)),
                  pl.BlockSpec((1, d, tf), lambda i, j, be, nu: (be[i], 0, jj(i, j, nu))),
                  pl.BlockSpec((1, d, tf), lambda i, j, be, nu: (be[i], 0, jj(i, j, nu))),
                  pl.BlockSpec((1, tf, d), lambda i, j, be, nu: (be[i], jj(i, j, nu), 0))],
        out_specs=pl.BlockSpec((tm, d), lambda i, j, be, nu: (i, 0)),
        scratch_shapes=[pltpu.VMEM((tm, d), bf16), pltpu.VMEM((tm, d), f32)],
    )
    return pl.pallas_call(
        _moe_ffn_kernel,
        out_shape=jax.ShapeDtypeStruct((p_rows, d), f32),
        grid_spec=grid_spec,
        compiler_params=_cparams("arbitrary", "arbitrary"),
        name="moe_ffn",
    )(blk_expert, n_used, xs, wg, wu, wd)


def _combine_kernel(pos_ref, y_ref, p_ref, x_ref, g_ref, o_ref, ybuf_ref, sem):
    tt = x_ref.shape[1]

    def row_copy(p, dst):
        return pltpu.make_async_copy(y_ref.at[pl.ds(p, 1), :], ybuf_ref.at[pl.ds(dst, 1), :], sem)

    def start(t, c):
        row_copy(pos_ref[0, 0, 2 * t], t).start()
        row_copy(pos_ref[0, 0, 2 * t + 1], tt + t).start()
        return c

    lax.fori_loop(0, tt, start, 0)

    def wait(t, c):
        row_copy(0, 0).wait()
        row_copy(0, 0).wait()
        return c

    lax.fori_loop(0, tt, wait, 0)
    p = p_ref[0]
    mix = p[:, 0:1] * ybuf_ref[pl.ds(0, tt), :] + p[:, 1:2] * ybuf_ref[pl.ds(tt, tt), :]
    o_ref[0] = x_ref[0] + g_ref[0] * mix


def moe_combine(y, pos, probs, x, gate):
    nb, lr, d = x.shape
    tt = _tile(lr, MOE_TT)
    nt = lr // tt
    pos3 = pos.reshape(nb * nt, 1, 2 * tt)
    probs3 = probs.reshape(nb, lr, LANES)
    return pl.pallas_call(
        _combine_kernel,
        out_shape=jax.ShapeDtypeStruct((nb, lr, d), f32),
        grid=(nb, nt),
        in_specs=[pl.BlockSpec((1, 1, 2 * tt), lambda b, i: (b * nt + i, 0, 0), memory_space=pltpu.SMEM),
                  pl.BlockSpec(memory_space=pl.ANY),
                  pl.BlockSpec((1, tt, LANES), lambda b, i: (b, i, 0)),
                  pl.BlockSpec((1, tt, d), lambda b, i: (b, i, 0)),
                  pl.BlockSpec((1, 1, d), lambda b, i: (b, 0, 0))],
        out_specs=pl.BlockSpec((1, tt, d), lambda b, i: (b, i, 0)),
        scratch_shapes=[pltpu.VMEM((2 * tt, d), f32), pltpu.SemaphoreType.DMA(())],
        compiler_params=_cparams("arbitrary", "arbitrary"),
        name="moe_combine",
    )(pos3, y, probs3, x, gate)


def moe_layer(h, x, gate, w_router, wg, wu, wd):
    nb, lr, d = h.shape
    n = nb * lr
    h2d = h.reshape(n, d)
    w_r = jnp.zeros((d, LANES), bf16).at[:, :N_EXPERTS].set(w_router.astype(bf16))
    e_out, p_out = moe_router(h2d, w_r)
    e_flat = e_out[:, :2].reshape(-1)
    onehot = (e_flat[:, None] == jnp.arange(N_EXPERTS, dtype=jnp.int32)[None, :]).astype(jnp.int32)
    csum = jnp.cumsum(onehot, axis=0)
    counts = csum[-1]
    rank = jnp.sum(onehot * csum, axis=1) - 1
    padded = (counts + MOE_TM - 1) // MOE_TM * MOE_TM
    pad_ends = jnp.cumsum(padded)
    pad_starts = pad_ends - padded
    pos = jnp.sum(onehot * pad_starts[None, :], axis=1) + rank
    n_blocks = (2 * n) // MOE_TM + N_EXPERTS
    blk_start = jnp.arange(n_blocks, dtype=jnp.int32) * MOE_TM
    blk_expert = jnp.minimum(jnp.sum((blk_start[:, None] >= pad_ends[None, :]).astype(jnp.int32), axis=1),
                             N_EXPERTS - 1).astype(jnp.int32)
    n_used = (pad_ends[-1] // MOE_TM).astype(jnp.int32).reshape(1)
    xs = moe_dispatch(h2d, pos.astype(jnp.int32), n_blocks * MOE_TM)
    y = moe_ffn(xs, blk_expert, n_used, wg, wu, wd)
    return moe_combine(y, pos.astype(jnp.int32), p_out, x, gate)


def _rope_tables(n_tok):
    rows = n_tok // GRID_W
    r = jnp.repeat(jnp.arange(rows, dtype=f32), GRID_W)
    cl = jnp.tile(jnp.arange(GRID_W, dtype=f32), rows)
    n_freq = MLA_ROPE // 4
    inv = ROPE_BASE ** (-jnp.arange(n_freq, dtype=f32) / n_freq)
    ang = jnp.concatenate([r[:, None] * inv, cl[:, None] * inv], axis=-1)
    cos, sin = jnp.cos(ang), jnp.sin(ang)
    return jnp.tile(cos, (1, 4)), jnp.concatenate([-sin, sin, -sin, sin], axis=-1)


def _prep_in_weights(w_in_l):
    d = w_in_l.shape[0]
    nq = MLA_HEADS * (MLA_NOPE + MLA_ROPE)
    wq = w_in_l[:, :nq].reshape(d, MLA_HEADS, MLA_NOPE + MLA_ROPE)
    w_ckv = w_in_l[:, nq:nq + KV_LORA]
    w_kr = w_in_l[:, nq + KV_LORA:nq + KV_LORA + MLA_ROPE]
    rest = w_in_l[:, nq + KV_LORA + MLA_ROPE:]
    w_main = jnp.concatenate([wq[:, :, :MLA_NOPE].reshape(d, Q_NOPE_W), wq[:, :, MLA_NOPE:].reshape(d, Q_ROPE_W),
                              rest], axis=1).astype(bf16)
    w_ckr = jnp.concatenate([w_ckv, w_kr, w_kr], axis=1).astype(bf16)
    return w_main, w_ckr


def kernel(x_prompt, x_sample, c, cache_ckv, cache_krope, state_ret, c_ctx, w_ada, b_ada, norm1_g, w_in, kv_norm_g, w_ukv, ret_decay_logit, ret_gn_g, w_o_mla, w_o_fnet, w_o_ret, w_out, norm2_g, w_gate_dense, w_up_dense, w_down_dense, w_router, w_gate_moe, w_up_moe, w_down_moe, final_norm_g):
    nbc, seq, d = x_prompt.shape
    nbs, ls, _ = x_sample.shape
    depth = w_in.shape[0]
    lr = nbc * seq
    assert lr == ls, "unified row layout needs BATCH * SEQ == DEC_SEQ"
    nb = 1 + nbs
    n_rows = nb * lr

    x = jnp.concatenate([x_prompt.reshape(1, lr, d), x_sample], axis=0)
    cond = jnp.concatenate([c_ctx[None, :], c, jnp.zeros((16 - nb, d), f32)], axis=0)
    mod = adaln_all(cond, w_ada, b_ada)[:, :nb]
    cos128, sin128 = _rope_tables(ls)

    w_ch = jnp.concatenate(_dft_mats(FNET_GC), axis=1).astype(bf16)
    c_c, s_c = _dft_mats(seq)
    cs_ctx = jnp.concatenate([c_c, -s_c], axis=1).astype(bf16)
    c_s, s_s = _dft_mats(ls)
    cs_lat = jnp.concatenate([c_s, -s_s], axis=1).astype(bf16)

    ckv_list, krope_list, ret_list = [], [], []
    spare = []
    for l in range(depth):
        sh1, sc1, g1, sh2, sc2, g2 = [m.reshape(nb, 1, d) for m in jnp.split(mod[l], 6, axis=-1)]
        w_main, w_ckr = _prep_in_weights(w_in[l])
        h = norm_mod(x, norm1_g[l], sc1, sh1, bf16)
        main = inproj_main(h, w_main, cos128, sin128)
        ckv, kr, krr = inproj_ckr(h, w_ckr, kv_norm_g[l], cos128, sin128)
        w_ukv_b = w_ukv[l].astype(bf16)

        ckv_c = ckv[0].reshape(nbc, seq, KV_LORA)
        kr_c = kr[0].reshape(nbc, seq, LANES)
        kv_c = matmul(ckv_c.reshape(lr, KV_LORA).astype(bf16), w_ukv_b, bf16).reshape(nbc, seq, -1)
        main_c = main.reshape(nb * nbc, seq, N_MAIN)
        bufs = [h] + spare
        bufs += [jnp.zeros((nb, lr, d), bf16) for _ in range(3 - len(bufs))]
        bufs = [bb.reshape(nb * nbc, seq, d) for bb in bufs[:3]]
        att_c = attention(main_c, kv_c, kr_c.astype(bf16), nbc, seq, 0, nb * nbc, bufs[0])
        ab_c = dft_channels(main_c, w_ch, nbc, seq, 0).reshape(nbc, 2 * seq, FNET_GROUPS * FNET_GC)
        four_c = dft_sequence(cs_ctx, ab_c, 0, nb * nbc, bufs[1])
        ret_c, sfin_c = retention(main_c, ret_decay_logit[l], ret_gn_g[l], None, nbc, seq, 0, nb * nbc, bufs[2])
        ckv_list.append(ckv_c)
        krope_list.append(kr_c[:, :, :MLA_ROPE])
        ret_list.append(sfin_c)

        ckv_keys = jnp.concatenate([ckv[1:], cache_ckv[:, l]], axis=1).astype(bf16)
        lk = ckv_keys.shape[1]
        kv_s = matmul(ckv_keys.reshape(nbs * lk, KV_LORA), w_ukv_b, bf16).reshape(nbs, lk, -1)
        ck = cache_krope[:, l].astype(bf16)
        kr_keys = jnp.concatenate([krr[1:], jnp.concatenate([ck, ck], axis=-1)], axis=1)
        att = attention(main, kv_s, kr_keys, nbs, ls, 1, nb, att_c.reshape(nb, ls, -1)).reshape(n_rows, -1)
        ab_s = dft_channels(main, w_ch, nbs, ls, 1).reshape(nbs, 2 * ls, FNET_GROUPS * FNET_GC)
        four = dft_sequence(cs_lat, ab_s, 1, nb, four_c.reshape(nb, ls, -1)).reshape(n_rows, -1)
        ret, _ = retention(main, ret_decay_logit[l], ret_gn_g[l], state_ret[:, l], nbs, ls, 1, nb,
                           ret_c.reshape(nb, ls, -1))
        ret = ret.reshape(n_rows, -1)
        spare = [att, four]

        merged = merge_branches(att, four, ret, w_o_mla[l].astype(bf16), w_o_fnet[l].astype(bf16),
                                w_o_ret[l].astype(bf16), main.reshape(n_rows, N_MAIN))
        x = proj_residual(merged.reshape(nb, lr, d), w_out[l].astype(bf16), x, g1)

        i = l // 2
        if l % 2 == 0:
            h2 = norm_mod(x, norm2_g[l], sc2, sh2, bf16)
            x = ffn_dense(h2, w_gate_dense[i].astype(bf16), w_up_dense[i].astype(bf16),
                          w_down_dense[i].astype(bf16), x, g2)
        else:
            h2 = norm_mod(x, norm2_g[l], sc2, sh2, f32)
            x = moe_layer(h2, x, g2, w_router[i], w_gate_moe[i].astype(bf16), w_up_moe[i].astype(bf16),
                          w_down_moe[i].astype(bf16))

    y_prompt = final_norm(x, final_norm_g, 0, 1).reshape(nbc, seq, d)
    y_sample = final_norm(x, final_norm_g, 1, nbs)
    new_ckv = jnp.stack(ckv_list, axis=1)
    new_krope = jnp.stack(krope_list, axis=1)
    new_ret = jnp.stack(ret_list, axis=1)
    return (y_prompt, y_sample, new_ckv, new_krope, new_ret)
```

```python
import functools
import math

import jax
import jax.numpy as jnp
from jax import lax
from jax.experimental import pallas as pl
from jax.experimental.pallas import tpu as pltpu

f32 = jnp.float32
bf16 = jnp.bfloat16

D_MODEL = 2048
GRID_W = 64
MLA_HEADS = 16
MLA_NOPE = 128
MLA_ROPE = 64
MLA_V = 128
KV_LORA = 512
ROPE_BASE = 10000.0
FNET_GROUPS = 4
FNET_GC = 512
RET_HEADS = 8
RET_DK = 128
RET_DV = 256
RET_CHUNK = 256
N_EXPERTS = 8
EPS = 1e-6
GN_EPS = 1e-5

LANES = 128
VMEM_LIMIT_BYTES = 56 * 1024 * 1024

Q_NOPE_W = MLA_HEADS * MLA_NOPE
Q_ROPE_W = MLA_HEADS * MLA_ROPE
OFF_Q = 0
OFF_QROPE = Q_NOPE_W
OFF_F = OFF_Q + Q_NOPE_W + Q_ROPE_W
OFF_RQ = OFF_F + FNET_GROUPS * FNET_GC
OFF_RK = OFF_RQ + RET_HEADS * RET_DK
OFF_RV = OFF_RK + RET_HEADS * RET_DK
OFF_RG = OFF_RV + RET_HEADS * RET_DV
OFF_GATES = OFF_RG + RET_HEADS * RET_DV
N_MAIN = OFF_GATES + 3 * D_MODEL
CKR_W = KV_LORA + LANES
Q_SCALE = (MLA_NOPE + MLA_ROPE) ** -0.5 * math.log2(math.e)


def _cparams(*sem):
    return pltpu.CompilerParams(dimension_semantics=sem, vmem_limit_bytes=VMEM_LIMIT_BYTES)


def _tile(n, pref):
    t = min(n, pref)
    while n % t:
        t -= 8
    return t


def _silu(x):
    return x * jax.nn.sigmoid(x)


def _rope128(x, cos, sin):
    lane = lax.broadcasted_iota(jnp.int32, x.shape, 1)
    first = (lane % MLA_ROPE) < (MLA_ROPE // 2)
    swapped = jnp.where(first, pltpu.roll(x, LANES - MLA_ROPE // 2, 1), pltpu.roll(x, MLA_ROPE // 2, 1))
    return x * cos + swapped * sin


def _adaln_kernel(c_ref, w_ref, b_ref, o_ref):
    s = _silu(c_ref[...]).astype(bf16)
    o_ref[0] = jnp.dot(s, w_ref[0].astype(bf16), preferred_element_type=f32) + b_ref[0]


def adaln_all(cond, w_ada, b_ada):
    depth, d, n = w_ada.shape
    r = cond.shape[0]
    tn = _tile(n, 1024)
    return pl.pallas_call(
        _adaln_kernel,
        out_shape=jax.ShapeDtypeStruct((depth, r, n), f32),
        grid=(depth, n // tn),
        in_specs=[pl.BlockSpec((r, d), lambda l, j: (0, 0)),
                  pl.BlockSpec((1, d, tn), lambda l, j: (l, 0, j)),
                  pl.BlockSpec((1, 1, tn), lambda l, j: (l, 0, j))],
        out_specs=pl.BlockSpec((1, r, tn), lambda l, j: (l, 0, j)),
        compiler_params=_cparams("arbitrary", "arbitrary"),
        name="adaln",
    )(cond, w_ada, b_ada.reshape(depth, 1, n))


def _norm_mod_kernel(x_ref, g_ref, sc_ref, sh_ref, o_ref):
    x = x_ref[0]
    y = x * lax.rsqrt(jnp.mean(x * x, axis=-1, keepdims=True) + EPS)
    y = y * g_ref[...]
    o_ref[0] = (y * (1.0 + sc_ref[0]) + sh_ref[0]).astype(o_ref.dtype)


def norm_mod(x, g, sc, sh, out_dtype):
    nb, lr, d = x.shape
    tm = _tile(lr, 512)
    return pl.pallas_call(
        _norm_mod_kernel,
        out_shape=jax.ShapeDtypeStruct((nb, lr, d), out_dtype),
        grid=(nb, lr // tm),
        in_specs=[pl.BlockSpec((1, tm, d), lambda b, i: (b, i, 0)),
                  pl.BlockSpec((1, d), lambda b, i: (0, 0)),
                  pl.BlockSpec((1, 1, d), lambda b, i: (b, 0, 0)),
                  pl.BlockSpec((1, 1, d), lambda b, i: (b, 0, 0))],
        out_specs=pl.BlockSpec((1, tm, d), lambda b, i: (b, i, 0)),
        compiler_params=_cparams("arbitrary", "arbitrary"),
        name="norm_mod",
    )(x, g.reshape(1, d), sc, sh)


def _final_norm_kernel(x_ref, g_ref, o_ref):
    x = x_ref[0]
    y = x * lax.rsqrt(jnp.mean(x * x, axis=-1, keepdims=True) + EPS)
    o_ref[0] = y * g_ref[...]


def final_norm(x, g, b_off, nb_out):
    _, lr, d = x.shape
    tm = _tile(lr, 512)
    return pl.pallas_call(
        _final_norm_kernel,
        out_shape=jax.ShapeDtypeStruct((nb_out, lr, d), f32),
        grid=(nb_out, lr // tm),
        in_specs=[pl.BlockSpec((1, tm, d), lambda b, i: (b + b_off, i, 0)),
                  pl.BlockSpec((1, d), lambda b, i: (0, 0))],
        out_specs=pl.BlockSpec((1, tm, d), lambda b, i: (b, i, 0)),
        compiler_params=_cparams("arbitrary", "arbitrary"),
        name="final_norm",
    )(x, g.reshape(1, d))


def _inproj_kernel(h_ref, w_ref, cos_ref, sin_ref, o_ref, *, tn, nchunk):
    b = pl.program_id(0)
    j = pl.program_id(2)
    tm = h_ref.shape[1]
    rc = tm // nchunk
    j_rope = OFF_QROPE // tn
    j_rk = OFF_RK // tn
    j_rv = OFF_RV // tn
    j_rg = OFF_RG // tn
    j_gates = OFF_GATES // tn
    is_q = j < OFF_F // tn
    is_rope = jnp.logical_and(jnp.logical_and(j >= j_rope, is_q), b >= 1)
    is_qplain = jnp.logical_and(is_q, jnp.logical_not(is_rope))
    is_rk = jnp.logical_and(j >= j_rk, j < j_rv)
    is_silu = jnp.logical_and(j >= j_rg, j < j_gates)
    is_sig = j >= j_gates
    plain = jnp.logical_not(is_q | is_rk | is_silu | is_sig)

    def run(epilogue):
        pending = None
        for r in range(nchunk):
            rows = slice(r * rc, (r + 1) * rc)
            acc = jnp.dot(h_ref[0, rows, :], w_ref[...], preferred_element_type=f32)
            if pending is not None:
                prow, pacc = pending
                o_ref[0, prow, :] = epilogue(pacc, prow).astype(o_ref.dtype)
            pending = (rows, acc)
        prow, pacc = pending
        o_ref[0, prow, :] = epilogue(pacc, prow).astype(o_ref.dtype)

    @pl.when(plain)
    def _():
        run(lambda acc, rows: acc)

    @pl.when(is_qplain)
    def _():
        run(lambda acc, rows: acc * Q_SCALE)

    @pl.when(is_rope)
    def _():
        def rope(acc, rows):
            cos = cos_ref[rows, :] * Q_SCALE
            sin = sin_ref[rows, :] * Q_SCALE
            return jnp.concatenate([_rope128(acc[:, s * LANES:(s + 1) * LANES], cos, sin)
                                    for s in range(tn // LANES)], axis=1)
        run(rope)

    @pl.when(is_rk)
    def _():
        run(lambda acc, rows: acc * (RET_DK ** -0.5))

    @pl.when(is_silu)
    def _():
        run(lambda acc, rows: _silu(acc))

    @pl.when(is_sig)
    def _():
        run(lambda acc, rows: jax.nn.sigmoid(acc))


def inproj_main(h, w_main, cos128, sin128):
    nb, lr, d = h.shape
    tm = _tile(lr, 1024)
    tn = 1024
    return pl.pallas_call(
        functools.partial(_inproj_kernel, tn=tn, nchunk=4),
        out_shape=jax.ShapeDtypeStruct((nb, lr, N_MAIN), bf16),
        grid=(nb, lr // tm, N_MAIN // tn),
        in_specs=[pl.BlockSpec((1, tm, d), lambda b, i, j: (b, i, 0)),
                  pl.BlockSpec((d, tn), lambda b, i, j: (0, j)),
                  pl.BlockSpec((tm, LANES), lambda b, i, j: (i, 0)),
                  pl.BlockSpec((tm, LANES), lambda b, i, j: (i, 0))],
        out_specs=pl.BlockSpec((1, tm, tn), lambda b, i, j: (b, i, j)),
        compiler_params=_cparams("arbitrary", "arbitrary", "arbitrary"),
        name="inproj_main",
    )(h, w_main, cos128, sin128)


def _ckr_kernel(h_ref, w_ref, g_ref, cos_ref, sin_ref, ckv_ref, kr_ref, krr_ref):
    acc = jnp.dot(h_ref[0], w_ref[...], preferred_element_type=f32)
    c = acc[:, :KV_LORA]
    y = c * lax.rsqrt(jnp.mean(c * c, axis=-1, keepdims=True) + EPS)
    ckv_ref[0] = y * g_ref[...]
    kr = acc[:, KV_LORA:]
    kr_ref[0] = kr
    krr_ref[0] = _rope128(kr, cos_ref[...], sin_ref[...]).astype(krr_ref.dtype)


def inproj_ckr(h, w_ckr, kv_g, cos128, sin128):
    nb, lr, d = h.shape
    tm = _tile(lr, 1024)
    return pl.pallas_call(
        _ckr_kernel,
        out_shape=(jax.ShapeDtypeStruct((nb, lr, KV_LORA), f32),
                   jax.ShapeDtypeStruct((nb, lr, LANES), f32),
                   jax.ShapeDtypeStruct((nb, lr, LANES), bf16)),
        grid=(nb, lr // tm),
        in_specs=[pl.BlockSpec((1, tm, d), lambda b, i: (b, i, 0)),
                  pl.BlockSpec((d, CKR_W), lambda b, i: (0, 0)),
                  pl.BlockSpec((1, KV_LORA), lambda b, i: (0, 0)),
                  pl.BlockSpec((tm, LANES), lambda b, i: (i, 0)),
                  pl.BlockSpec((tm, LANES), lambda b, i: (i, 0))],
        out_specs=(pl.BlockSpec((1, tm, KV_LORA), lambda b, i: (b, i, 0)),
                   pl.BlockSpec((1, tm, LANES), lambda b, i: (b, i, 0)),
                   pl.BlockSpec((1, tm, LANES), lambda b, i: (b, i, 0))),
        compiler_params=_cparams("arbitrary", "arbitrary"),
        name="inproj_ckr",
    )(h, w_ckr, kv_g.reshape(1, KV_LORA), cos128, sin128)


def _mm_kernel(x_ref, w_ref, o_ref):
    o_ref[...] = jnp.dot(x_ref[...], w_ref[...], preferred_element_type=f32).astype(o_ref.dtype)


def matmul(x, w, out_dtype, tm_pref=512, tn_pref=4096):
    m, k = x.shape
    _, n = w.shape
    tm = _tile(m, tm_pref)
    tn = _tile(n, tn_pref)
    return pl.pallas_call(
        _mm_kernel,
        out_shape=jax.ShapeDtypeStruct((m, n), out_dtype),
        grid=(m // tm, n // tn),
        in_specs=[pl.BlockSpec((tm, k), lambda i, j: (i, 0)),
                  pl.BlockSpec((k, tn), lambda i, j: (0, j))],
        out_specs=pl.BlockSpec((tm, tn), lambda i, j: (i, j)),
        compiler_params=_cparams("arbitrary", "arbitrary"),
        name="matmul",
    )(x, w)


def _attn_single_kernel(qn_ref, qr_ref, kn_ref, v_ref, kr_ref, o_ref, kcat_ref, vt_ref, s_ref, *, nk, tk, nparts):
    h = pl.program_id(1)
    qi = pl.program_id(2)
    lk = kcat_ref.shape[0]

    @pl.when(qi == 0)
    def _():
        kcat_ref[:, :MLA_NOPE] = kn_ref[0]
        kcat_ref[:, MLA_NOPE:] = kr_ref[0]
        for c in range(nk):
            vt_ref[:, c * tk:(c + 1) * tk] = v_ref[0, c * tk:(c + 1) * tk, :].astype(f32).T.astype(bf16)

    qr = qr_ref[0]
    lane = lax.broadcasted_iota(jnp.int32, qr.shape, 1)
    qr = jnp.where((lane // MLA_ROPE) == (h % 2), qr, jnp.zeros_like(qr))
    qcat = jnp.concatenate([qn_ref[0], qr], axis=1)
    tq = qcat.shape[0]

    bounds = [((nk * i) // nparts) * tk for i in range(nparts + 1)]
    m = jnp.full((1, tq), -jnp.inf, f32)
    pending = None
    for i in range(nparts):
        r0, r1 = bounds[i], bounds[i + 1]
        s = lax.dot_general(kcat_ref[r0:r1, :], qcat, (((1,), (1,)), ((), ())), preferred_element_type=f32)
        s_ref[r0:r1, :] = s
        if pending is not None:
            m = jnp.maximum(m, jnp.max(pending, axis=0, keepdims=True))
        pending = s
    m = jnp.maximum(m, jnp.max(pending, axis=0, keepdims=True))

    l = jnp.zeros((1, tq), f32)
    acc = jnp.zeros((MLA_V, tq), f32)
    for c in range(nk):
        p = jnp.exp2(s_ref[c * tk:(c + 1) * tk, :] - m)
        l = l + jnp.sum(p, axis=0, keepdims=True)
        acc = acc + jnp.dot(vt_ref[:, c * tk:(c + 1) * tk], p.astype(bf16), preferred_element_type=f32)
    o_ref[0] = (acc * (1.0 / l)).T.astype(o_ref.dtype)


def _attn_kernel(qn_ref, qr_ref, kn_ref, v_ref, kr_ref, o_ref, kcat_ref, vt_ref, sa_ref, sb_ref, ma_ref, mb_ref,
                 *, nk, tk, nparts):
    h = pl.program_id(1)
    i = pl.program_id(2)

    @pl.when(i == 0)
    def _():
        kcat_ref[:, :MLA_NOPE] = kn_ref[0]
        kcat_ref[:, MLA_NOPE:] = kr_ref[0]
        for c in range(nk):
            vt_ref[:, c * tk:(c + 1) * tk] = v_ref[0, c * tk:(c + 1) * tk, :].astype(f32).T.astype(bf16)
        sb_ref[...] = jnp.zeros(sb_ref.shape, f32)
        mb_ref[...] = jnp.zeros(mb_ref.shape, f32)

    def step(s_new, s_old, m_new_ref, m_old_ref):
        qr = qr_ref[0]
        lane = lax.broadcasted_iota(jnp.int32, qr.shape, 1)
        qr = jnp.where((lane // MLA_ROPE) == (h % 2), qr, jnp.zeros_like(qr))
        qcat = jnp.concatenate([qn_ref[0], qr], axis=1)
        tq = qcat.shape[0]
        m_old = m_old_ref[0:1, :]
        bounds = [(nk * g) // nparts for g in range(nparts + 1)]
        m = jnp.full((1, tq), -jnp.inf, f32)
        l = jnp.zeros((1, tq), f32)
        acc = jnp.zeros((MLA_V, tq), f32)
        for g in range(nparts):
            r0, r1 = bounds[g] * tk, bounds[g + 1] * tk
            s = lax.dot_general(kcat_ref[r0:r1, :], qcat, (((1,), (1,)), ((), ())), preferred_element_type=f32)
            s_new[r0:r1, :] = s
            m = jnp.maximum(m, jnp.max(s, axis=0, keepdims=True))
            for c in range(bounds[g], bounds[g + 1]):
                p = jnp.exp2(s_old[c * tk:(c + 1) * tk, :] - m_old)
                l = l + jnp.sum(p, axis=0, keepdims=True)
                acc = acc + jnp.dot(vt_ref[:, c * tk:(c + 1) * tk], p.astype(bf16), preferred_element_type=f32)
        m_new_ref[...] = jnp.broadcast_to(m, m_new_ref.shape)
        o_ref[0] = (acc * (1.0 / l)).T.astype(o_ref.dtype)

    @pl.when(i % 2 == 0)
    def _():
        step(sa_ref, sb_ref, ma_ref, mb_ref)

    @pl.when(i % 2 == 1)
    def _():
        step(sb_ref, sa_ref, mb_ref, ma_ref)


def _with_prev(kern, in_specs, args, prev):
    if prev is None:
        return kern, in_specs, args, {}
    n_in = len(args)

    def wrapped(*refs):
        return kern(*refs[:n_in], *refs[n_in + 1:])

    return wrapped, in_specs + [pl.BlockSpec(memory_space=pl.ANY)], args + [prev], {n_in: 0}


def attention(main, kv, krope, nb, lq, b_off, nb_total, prev):
    lk = kv.shape[1]
    tq = _tile(lq, 256)
    tk = _tile(lk, 256)
    nq = lq // tq
    qr_blk = OFF_QROPE // LANES
    nk = lk // tk
    nparts = min(4, nk)
    skewed = nq > 1
    if skewed:
        steps = nq + 1
        q_idx = lambda i: jnp.minimum(i, nq - 1)
        o_idx = lambda i: jnp.maximum(i - 1, 0)
        kern = functools.partial(_attn_kernel, nk=nk, tk=tk, nparts=nparts)
        scratch = [pltpu.VMEM((lk, MLA_NOPE + LANES), bf16), pltpu.VMEM((MLA_V, lk), bf16),
                   pltpu.VMEM((lk, tq), f32), pltpu.VMEM((lk, tq), f32),
                   pltpu.VMEM((8, tq), f32), pltpu.VMEM((8, tq), f32)]
    else:
        steps = nq
        q_idx = lambda i: i
        o_idx = lambda i: i
        kern = functools.partial(_attn_single_kernel, nk=nk, tk=tk, nparts=nparts)
        scratch = [pltpu.VMEM((lk, MLA_NOPE + LANES), bf16), pltpu.VMEM((MLA_V, lk), bf16),
                   pltpu.VMEM((lk, tq), f32)]
    in_specs = [pl.BlockSpec((1, tq, MLA_NOPE), lambda b, h, i: (b + b_off, q_idx(i), h)),
                pl.BlockSpec((1, tq, LANES), lambda b, h, i: (b + b_off, q_idx(i), qr_blk + h // 2)),
                pl.BlockSpec((1, lk, MLA_NOPE), lambda b, h, i: (b, 0, 2 * h)),
                pl.BlockSpec((1, lk, MLA_V), lambda b, h, i: (b, 0, 2 * h + 1)),
                pl.BlockSpec((1, lk, LANES), lambda b, h, i: (b, 0, 0))]
    kern, in_specs, args, aliases = _with_prev(kern, in_specs, [main, main, kv, kv, krope], prev)
    return pl.pallas_call(
        kern,
        out_shape=jax.ShapeDtypeStruct((nb_total, lq, MLA_HEADS * MLA_V), bf16),
        grid=(nb, MLA_HEADS, steps),
        in_specs=in_specs,
        out_specs=pl.BlockSpec((1, tq, MLA_V), lambda b, h, i: (b + b_off, o_idx(i), h)),
        scratch_shapes=scratch,
        input_output_aliases=aliases,
        compiler_params=_cparams("arbitrary", "arbitrary", "arbitrary"),
        name="attention",
    )(*args)


def _dft_ch_kernel(xa_ref, xb_ref, w_ref, o_ref):
    half = FNET_GROUPS // 2
    for g in range(FNET_GROUPS):
        x_ref = xa_ref if g < half else xb_ref
        lo = (g % half) * FNET_GC
        y = jnp.dot(x_ref[0, :, lo:lo + FNET_GC], w_ref[...], preferred_element_type=f32)
        o_ref[0, 0, :, g * FNET_GC:(g + 1) * FNET_GC] = y[:, :FNET_GC].astype(o_ref.dtype)
        o_ref[0, 1, :, g * FNET_GC:(g + 1) * FNET_GC] = y[:, FNET_GC:].astype(o_ref.dtype)


def dft_channels(main, w_ch, nb, ls, b_off):
    tm = _tile(ls, 512)
    wb = FNET_GROUPS * FNET_GC // 2
    f_blk = OFF_F // wb
    return pl.pallas_call(
        _dft_ch_kernel,
        out_shape=jax.ShapeDtypeStruct((nb, 2, ls, FNET_GROUPS * FNET_GC), bf16),
        grid=(nb, ls // tm),
        in_specs=[pl.BlockSpec((1, tm, wb), lambda b, i: (b + b_off, i, f_blk)),
                  pl.BlockSpec((1, tm, wb), lambda b, i: (b + b_off, i, f_blk + 1)),
                  pl.BlockSpec((FNET_GC, 2 * FNET_GC), lambda b, i: (0, 0))],
        out_specs=pl.BlockSpec((1, 2, tm, FNET_GROUPS * FNET_GC), lambda b, i: (b, 0, i, 0)),
        compiler_params=_cparams("arbitrary", "arbitrary"),
        name="dft_channels",
    )(main, main, w_ch)


def _dft_seq_kernel(l_ref, r_ref, o_ref):
    o_ref[0] = jnp.dot(l_ref[...], r_ref[0], preferred_element_type=f32).astype(o_ref.dtype)


def dft_sequence(cs, ab, b_off, nb_total, prev):
    nb, k, w = ab.shape
    ls = cs.shape[0]
    tm = _tile(ls, 512)
    tn = _tile(w, 512)
    in_specs = [pl.BlockSpec((tm, k), lambda b, j, i: (i, 0)),
                pl.BlockSpec((1, k, tn), lambda b, j, i: (b, 0, j))]
    kern, in_specs, args, aliases = _with_prev(_dft_seq_kernel, in_specs, [cs, ab], prev)
    return pl.pallas_call(
        kern,
        out_shape=jax.ShapeDtypeStruct((nb_total, ls, w), bf16),
        grid=(nb, w // tn, ls // tm),
        in_specs=in_specs,
        out_specs=pl.BlockSpec((1, tm, tn), lambda b, j, i: (b + b_off, i, j)),
        input_output_aliases=aliases,
        compiler_params=_cparams("arbitrary", "arbitrary", "arbitrary"),
        name="dft_sequence",
    )(*args)


def _dft_mats(n):
    idx = jnp.arange(n, dtype=jnp.int32)
    ang = ((idx[:, None] * idx[None, :]) % n).astype(f32) * (2.0 * math.pi / n)
    s = n ** -0.5
    return jnp.cos(ang) * s, jnp.sin(ang) * s


def _log_sigmoid(x):
    return jnp.minimum(x, 0.0) - jnp.log1p(jnp.exp(-jnp.abs(x)))


def _ret_kernel(*refs, nc, use_s0):
    if use_s0:
        dl_ref, q_ref, k_ref, v_ref, g_ref, gn_ref, s0_ref, o_ref, sfin_ref, of_ref, ob_ref = refs
    else:
        dl_ref, q_ref, k_ref, v_ref, g_ref, gn_ref, o_ref, sfin_ref, of_ref, ob_ref = refs
        s0_ref = None
    C = RET_CHUNK
    lgf = _log_sigmoid(dl_ref[0, 0])[0:1, :]
    lgb = _log_sigmoid(dl_ref[1, 0])[0:1, :]
    lgf2 = jnp.concatenate([lgf, lgf], axis=1)
    lgb2 = jnp.concatenate([lgb, lgb], axis=1)
    lgf_c = jnp.concatenate([lgf] * (C // LANES), axis=1)
    lgb_c = jnp.concatenate([lgb] * (C // LANES), axis=1)
    rows = lax.broadcasted_iota(jnp.int32, (C, C), 0)
    cols = lax.broadcasted_iota(jnp.int32, (C, C), 1)
    rel = (rows - cols).astype(f32)
    d_f = jnp.where(rel >= 0, jnp.exp(jnp.maximum(rel, 0.0) * lgf_c), 0.0)
    d_b = jnp.where(rel <= 0, jnp.exp(jnp.maximum(-rel, 0.0) * lgb_c), 0.0)
    r_v = lax.broadcasted_iota(jnp.int32, (C, RET_DV), 0).astype(f32)
    r_k = lax.broadcasted_iota(jnp.int32, (C, RET_DK), 0).astype(f32)
    qdec_f = jnp.exp((r_v + 1.0) * lgf2)
    qdec_b = jnp.exp((C - r_v) * lgb2)
    kdec_f = jnp.exp((C - 1.0 - r_k) * lgf)
    kdec_b = jnp.exp(r_k * lgb)
    cdec_f = jnp.exp(C * lgf2)
    cdec_b = jnp.exp(C * lgb2)

    def chunk(c, s, dmat, qdec, kdec, cdec):
        r = pl.multiple_of(c * C, C)
        q = q_ref[0, pl.ds(r, C), :]
        k = k_ref[0, pl.ds(r, C), :]
        v = v_ref[0, pl.ds(r, C), :]
        inner = lax.dot_general(q, k, (((1,), (1,)), ((), ())), preferred_element_type=f32) * dmat
        o = (jnp.dot(inner.astype(bf16), v, preferred_element_type=f32)
             + jnp.dot(q, s.astype(bf16), preferred_element_type=f32) * qdec)
        kd = (k.astype(f32) * kdec).T.astype(bf16)
        s_new = s * cdec + jnp.dot(kd, v, preferred_element_type=f32)
        return r, o, s_new

    def body(c, carry):
        sf, sb = carry
        r, o, sf = chunk(c, sf, d_f, qdec_f, kdec_f, cdec_f)
        of_ref[pl.ds(r, C), :] = o
        r, o, sb = chunk(nc - 1 - c, sb, d_b, qdec_b, kdec_b, cdec_b)
        ob_ref[pl.ds(r, C), :] = o
        return sf, sb

    if use_s0:
        init = (s0_ref[0, 0, 0], s0_ref[0, 1, 0])
    else:
        init = (jnp.zeros((RET_DK, RET_DV), f32), jnp.zeros((RET_DK, RET_DV), f32))
    sf, sb = lax.fori_loop(0, nc, body, init, unroll=2)
    sfin_ref[0, 0, 0] = sf
    sfin_ref[0, 1, 0] = sb

    gn = gn_ref[...]

    def norm_body(c, carry):
        r = pl.multiple_of(c * C, C)
        o = of_ref[pl.ds(r, C), :] + ob_ref[pl.ds(r, C), :]
        mu = jnp.mean(o, axis=-1, keepdims=True)
        d = o - mu
        y = d * lax.rsqrt(jnp.mean(d * d, axis=-1, keepdims=True) + GN_EPS) * gn
        o_ref[0, pl.ds(r, C), :] = (y * g_ref[0, pl.ds(r, C), :].astype(f32)).astype(o_ref.dtype)
        return carry

    lax.fori_loop(0, nc, norm_body, 0, unroll=2)


def retention(main, decay_logit, gn_g, s0, nb, ls, b_off, nb_total, prev):
    use_s0 = s0 is not None
    nc = ls // RET_CHUNK
    dl = jnp.broadcast_to(decay_logit.astype(f32)[:, :, None, None], (2, RET_HEADS, 8, LANES))
    in_specs = [pl.BlockSpec((2, 1, 8, LANES), lambda b, h: (0, h, 0, 0)),
                pl.BlockSpec((1, ls, RET_DK), lambda b, h: (b + b_off, 0, OFF_RQ // RET_DK + h)),
                pl.BlockSpec((1, ls, RET_DK), lambda b, h: (b + b_off, 0, OFF_RK // RET_DK + h)),
                pl.BlockSpec((1, ls, RET_DV), lambda b, h: (b + b_off, 0, OFF_RV // RET_DV + h)),
                pl.BlockSpec((1, ls, RET_DV), lambda b, h: (b + b_off, 0, OFF_RG // RET_DV + h)),
                pl.BlockSpec((1, RET_DV), lambda b, h: (0, h))]
    args = [dl, main, main, main, main, gn_g.reshape(1, RET_HEADS * RET_DV)]
    if use_s0:
        in_specs.append(pl.BlockSpec((1, 2, 1, RET_DK, RET_DV), lambda b, h: (b, 0, h, 0, 0)))
        args.append(s0)
    kern, in_specs, args, aliases = _with_prev(functools.partial(_ret_kernel, nc=nc, use_s0=use_s0),
                                               in_specs, args, prev)
    return pl.pallas_call(
        kern,
        out_shape=(jax.ShapeDtypeStruct((nb_total, ls, RET_HEADS * RET_DV), bf16),
                   jax.ShapeDtypeStruct((nb, 2, RET_HEADS, RET_DK, RET_DV), f32)),
        grid=(nb, RET_HEADS),
        in_specs=in_specs,
        out_specs=(pl.BlockSpec((1, ls, RET_DV), lambda b, h: (b + b_off, 0, h)),
                   pl.BlockSpec((1, 2, 1, RET_DK, RET_DV), lambda b, h: (b, 0, h, 0, 0))),
        scratch_shapes=[pltpu.VMEM((ls, RET_DV), f32), pltpu.VMEM((ls, RET_DV), f32)],
        input_output_aliases=aliases,
        compiler_params=_cparams("arbitrary", "arbitrary"),
        name="retention",
    )(*args)


def _merge_kernel(a_ref, f_ref, r_ref, wa_ref, wf_ref, wr_ref, ga_ref, gf_ref, gr_ref, o_ref):
    m = ga_ref[...].astype(f32) * jnp.dot(a_ref[...], wa_ref[...], preferred_element_type=f32)
    m += gf_ref[...].astype(f32) * jnp.dot(f_ref[...], wf_ref[...], preferred_element_type=f32)
    m += gr_ref[...].astype(f32) * jnp.dot(r_ref[...], wr_ref[...], preferred_element_type=f32)
    o_ref[...] = m.astype(o_ref.dtype)


def merge_branches(att, four, ret, wa, wf, wr, main2d):
    m, d = att.shape
    tm = _tile(m, 512)
    tn = 512
    gb = OFF_GATES // tn
    nd = d // tn
    x_spec = pl.BlockSpec((tm, d), lambda i, j: (i, 0))
    w_spec = pl.BlockSpec((d, tn), lambda i, j: (0, j))
    return pl.pallas_call(
        _merge_kernel,
        out_shape=jax.ShapeDtypeStruct((m, d), bf16),
        grid=(m // tm, nd),
        in_specs=[x_spec, x_spec, x_spec, w_spec, w_spec, w_spec,
                  pl.BlockSpec((tm, tn), lambda i, j: (i, gb + j)),
                  pl.BlockSpec((tm, tn), lambda i, j: (i, gb + nd + j)),
                  pl.BlockSpec((tm, tn), lambda i, j: (i, gb + 2 * nd + j))],
        out_specs=pl.BlockSpec((tm, tn), lambda i, j: (i, j)),
        compiler_params=_cparams("arbitrary", "arbitrary"),
        name="merge_branches",
    )(att, four, ret, wa, wf, wr, main2d, main2d, main2d)


def _proj_res_kernel(m_ref, w_ref, x_ref, g_ref, o_ref):
    y = jnp.dot(m_ref[0], w_ref[...], preferred_element_type=f32)
    o_ref[0] = x_ref[0] + g_ref[0] * y


def proj_residual(merged, w, x, gate):
    nb, lr, d = x.shape
    tm = _tile(lr, 1024)
    tn = 1024
    return pl.pallas_call(
        _proj_res_kernel,
        out_shape=jax.ShapeDtypeStruct((nb, lr, d), f32),
        grid=(nb, lr // tm, d // tn),
        in_specs=[pl.BlockSpec((1, tm, d), lambda b, i, j: (b, i, 0)),
                  pl.BlockSpec((d, tn), lambda b, i, j: (0, j)),
                  pl.BlockSpec((1, tm, tn), lambda b, i, j: (b, i, j)),
                  pl.BlockSpec((1, 1, tn), lambda b, i, j: (b, 0, j))],
        out_specs=pl.BlockSpec((1, tm, tn), lambda b, i, j: (b, i, j)),
        compiler_params=_cparams("arbitrary", "arbitrary", "arbitrary"),
        name="proj_residual",
    )(merged, w, x, gate)


def _ffn_kernel(h_ref, wg_ref, wu_ref, wd_ref, x_ref, g_ref, o_ref, acc_ref):
    j = pl.program_id(2)

    @pl.when(j == 0)
    def _():
        acc_ref[...] = jnp.zeros_like(acc_ref)

    h = h_ref[0]
    a = jnp.dot(h, wg_ref[...], preferred_element_type=f32)
    u = jnp.dot(h, wu_ref[...], preferred_element_type=f32)
    t = (_silu(a) * u).astype(bf16)
    acc_ref[...] += jnp.dot(t, wd_ref[...], preferred_element_type=f32)

    @pl.when(j == pl.num_programs(2) - 1)
    def _():
        o_ref[0] = x_ref[0] + g_ref[0] * acc_ref[...]


def ffn_dense(h, wg, wu, wd, x, gate):
    nb, lr, d = x.shape
    dff = wg.shape[1]
    tm = _tile(lr, 512)
    tf = _tile(dff, 512)
    return pl.pallas_call(
        _ffn_kernel,
        out_shape=jax.ShapeDtypeStruct((nb, lr, d), f32),
        grid=(nb, lr // tm, dff // tf),
        in_specs=[pl.BlockSpec((1, tm, d), lambda b, i, j: (b, i, 0)),
                  pl.BlockSpec((d, tf), lambda b, i, j: (0, j)),
                  pl.BlockSpec((d, tf), lambda b, i, j: (0, j)),
                  pl.BlockSpec((tf, d), lambda b, i, j: (j, 0)),
                  pl.BlockSpec((1, tm, d), lambda b, i, j: (b, i, 0)),
                  pl.BlockSpec((1, 1, d), lambda b, i, j: (b, 0, 0))],
        out_specs=pl.BlockSpec((1, tm, d), lambda b, i, j: (b, i, 0)),
        scratch_shapes=[pltpu.VMEM((tm, d), f32)],
        compiler_params=_cparams("arbitrary", "arbitrary", "arbitrary"),
        name="ffn_dense",
    )(h, wg, wu, wd, x, gate)


MOE_TM = 512
MOE_TT = 256


def _router_kernel(h_ref, w_ref, e_ref, p_ref):
    logits = jnp.dot(h_ref[...].astype(bf16), w_ref[...], preferred_element_type=f32)
    lane = lax.broadcasted_iota(jnp.int32, logits.shape, 1)
    lane_f = lane.astype(f32)
    neg = jnp.full_like(logits, -jnp.inf)
    l0 = jnp.where(lane < N_EXPERTS, logits, neg)
    v1 = jnp.max(l0, axis=1, keepdims=True)
    i1 = jnp.min(jnp.where(l0 == v1, lane_f, float(LANES)), axis=1, keepdims=True)
    l1 = jnp.where(lane_f == i1, neg, l0)
    v2 = jnp.max(l1, axis=1, keepdims=True)
    i2 = jnp.min(jnp.where(l1 == v2, lane_f, float(LANES)), axis=1, keepdims=True)
    e = jnp.exp(v2 - v1)
    den = 1.0 + e
    e_ref[...] = jnp.where(lane == 0, i1, jnp.where(lane == 1, i2, 0.0)).astype(jnp.int32)
    p_ref[...] = jnp.where(lane == 0, 1.0 / den, jnp.where(lane == 1, e / den, 0.0))


def moe_router(h2d, w_router_pad):
    n, d = h2d.shape
    tm = _tile(n, 512)
    return pl.pallas_call(
        _router_kernel,
        out_shape=(jax.ShapeDtypeStruct((n, LANES), jnp.int32), jax.ShapeDtypeStruct((n, LANES), f32)),
        grid=(n // tm,),
        in_specs=[pl.BlockSpec((tm, d), lambda i: (i, 0)),
                  pl.BlockSpec((d, LANES), lambda i: (0, 0))],
        out_specs=(pl.BlockSpec((tm, LANES), lambda i: (i, 0)), pl.BlockSpec((tm, LANES), lambda i: (i, 0))),
        compiler_params=_cparams("arbitrary"),
        name="moe_router",
    )(h2d, w_router_pad)


def _dispatch_kernel(pos_ref, h_ref, xs_in_ref, xs_ref, sem):
    del xs_in_ref
    tt = h_ref.shape[0]

    def row_copy(t, p):
        return pltpu.make_async_copy(h_ref.at[pl.ds(t, 1), :], xs_ref.at[pl.ds(p, 1), :], sem)

    def start(t, c):
        row_copy(t, pos_ref[0, 0, 2 * t]).start()
        row_copy(t, pos_ref[0, 0, 2 * t + 1]).start()
        return c

    lax.fori_loop(0, tt, start, 0)

    def wait(t, c):
        row_copy(0, 0).wait()
        row_copy(0, 0).wait()
        return c

    lax.fori_loop(0, tt, wait, 0)


def moe_dispatch(h2d, pos, p_rows):
    n, d = h2d.shape
    tt = _tile(n, MOE_TT)
    pos3 = pos.reshape(n // tt, 1, 2 * tt)
    xs0 = jnp.zeros((p_rows, d), h2d.dtype)
    return pl.pallas_call(
        _dispatch_kernel,
        out_shape=jax.ShapeDtypeStruct((p_rows, d), h2d.dtype),
        grid=(n // tt,),
        in_specs=[pl.BlockSpec((1, 1, 2 * tt), lambda i: (i, 0, 0), memory_space=pltpu.SMEM),
                  pl.BlockSpec((tt, d), lambda i: (i, 0)),
                  pl.BlockSpec(memory_space=pl.ANY)],
        out_specs=pl.BlockSpec(memory_space=pl.ANY),
        scratch_shapes=[pltpu.SemaphoreType.DMA(())],
        input_output_aliases={2: 0},
        compiler_params=_cparams("arbitrary"),
        name="moe_dispatch",
    )(pos3, h2d, xs0)


def _moe_ffn_kernel(be_ref, nu_ref, x_ref, wg_ref, wu_ref, wd_ref, y_ref, xb_ref, acc_ref):
    del be_ref
    i = pl.program_id(0)
    j = pl.program_id(1)
    active = i < nu_ref[0]

    @pl.when(jnp.logical_and(active, j == 0))
    def _():
        xb_ref[...] = x_ref[...].astype(bf16)
        acc_ref[...] = jnp.zeros_like(acc_ref)

    @pl.when(active)
    def _():
        xb = xb_ref[...]
        a = jnp.dot(xb, wg_ref[0], preferred_element_type=f32)
        u = jnp.dot(xb, wu_ref[0], preferred_element_type=f32)
        t = (_silu(a) * u).astype(bf16)
        acc_ref[...] += jnp.dot(t, wd_ref[0], preferred_element_type=f32)

    @pl.when(j == pl.num_programs(1) - 1)
    def _():
        y_ref[...] = jnp.where(active, acc_ref[...], 0.0)


def moe_ffn(xs, blk_expert, n_used, wg, wu, wd):
    p_rows, d = xs.shape
    dff = wg.shape[2]
    tm = MOE_TM
    tf = _tile(dff, 512)
    nf = dff // tf

    def jj(i, j, nu):
        return jnp.where(i < nu[0], j, nf - 1)

    grid_spec = pltpu.PrefetchScalarGridSpec(
        num_scalar_prefetch=2,
        grid=(p_rows // tm, nf),
        in_specs=[pl.BlockSpec((tm, d), lambda i, j, be, nu: (i, 0)),
                  pl.BlockSpec((1, d, tf), lambda i, j, be, nu: (be[i], 0, jj(i, j, nu))),
                  pl.BlockSpec((1, d, tf), lambda i, j, be, nu: (be[i], 0, jj(i, j, nu))),
                  pl.BlockSpec((1, tf, d), lambda i, j, be, nu: (be[i], jj(i, j, nu), 0))],
        out_specs=pl.BlockSpec((tm, d), lambda i, j, be, nu: (i, 0)),
        scratch_shapes=[pltpu.VMEM((tm, d), bf16), pltpu.VMEM((tm, d), f32)],
    )
    return pl.pallas_call(
        _moe_ffn_kernel,
        out_shape=jax.ShapeDtypeStruct((p_rows, d), f32),
        grid_spec=grid_spec,
        compiler_params=_cparams("arbitrary", "arbitrary"),
        name="moe_ffn",
    )(blk_expert, n_used, xs, wg, wu, wd)


def _combine_kernel(pos_ref, y_ref, p_ref, x_ref, g_ref, o_ref, ybuf_ref, sem):
    tt = x_ref.shape[1]

    def row_copy(p, dst):
        return pltpu.make_async_copy(y_ref.at[pl.ds(p, 1), :], ybuf_ref.at[pl.ds(dst, 1), :], sem)

    def start(t, c):
        row_copy(pos_ref[0, 0, 2 * t], t).start()
        row_copy(pos_ref[0, 0, 2 * t + 1], tt + t).start()
        return c

    lax.fori_loop(0, tt, start, 0)

    def wait(t, c):
        row_copy(0, 0).wait()
        row_copy(0, 0).wait()
        return c

    lax.fori_loop(0, tt, wait, 0)
    p = p_ref[0]
    mix = p[:, 0:1] * ybuf_ref[pl.ds(0, tt), :] + p[:, 1:2] * ybuf_ref[pl.ds(tt, tt), :]
    o_ref[0] = x_ref[0] + g_ref[0] * mix


def moe_combine(y, pos, probs, x, gate):
    nb, lr, d = x.shape
    tt = _tile(lr, MOE_TT)
    nt = lr // tt
    pos3 = pos.reshape(nb * nt, 1, 2 * tt)
    probs3 = probs.reshape(nb, lr, LANES)
    return pl.pallas_call(
        _combine_kernel,
        out_shape=jax.ShapeDtypeStruct((nb, lr, d), f32),
        grid=(nb, nt),
        in_specs=[pl.BlockSpec((1, 1, 2 * tt), lambda b, i: (b * nt + i, 0, 0), memory_space=pltpu.SMEM),
                  pl.BlockSpec(memory_space=pl.ANY),
                  pl.BlockSpec((1, tt, LANES), lambda b, i: (b, i, 0)),
                  pl.BlockSpec((1, tt, d), lambda b, i: (b, i, 0)),
                  pl.BlockSpec((1, 1, d), lambda b, i: (b, 0, 0))],
        out_specs=pl.BlockSpec((1, tt, d), lambda b, i: (b, i, 0)),
        scratch_shapes=[pltpu.VMEM((2 * tt, d), f32), pltpu.SemaphoreType.DMA(())],
        compiler_params=_cparams("arbitrary", "arbitrary"),
        name="moe_combine",
    )(pos3, y, probs3, x, gate)


def moe_layer(h, x, gate, w_router, wg, wu, wd):
    nb, lr, d = h.shape
    n = nb * lr
    h2d = h.reshape(n, d)
    w_r = jnp.zeros((d, LANES), bf16).at[:, :N_EXPERTS].set(w_router.astype(bf16))
    e_out, p_out = moe_router(h2d, w_r)
    e_flat = e_out[:, :2].reshape(-1)
    onehot = (e_flat[:, None] == jnp.arange(N_EXPERTS, dtype=jnp.int32)[None, :]).astype(jnp.int32)
    csum = jnp.cumsum(onehot, axis=0)
    counts = csum[-1]
    rank = jnp.sum(onehot * csum, axis=1) - 1
    padded = (counts + MOE_TM - 1) // MOE_TM * MOE_TM
    pad_ends = jnp.cumsum(padded)
    pad_starts = pad_ends - padded
    pos = jnp.sum(onehot * pad_starts[None, :], axis=1) + rank
    n_blocks = (2 * n) // MOE_TM + N_EXPERTS
    blk_start = jnp.arange(n_blocks, dtype=jnp.int32) * MOE_TM
    blk_expert = jnp.minimum(jnp.sum((blk_start[:, None] >= pad_ends[None, :]).astype(jnp.int32), axis=1),
                             N_EXPERTS - 1).astype(jnp.int32)
    n_used = (pad_ends[-1] // MOE_TM).astype(jnp.int32).reshape(1)
    xs = moe_dispatch(h2d, pos.astype(jnp.int32), n_blocks * MOE_TM)
    y = moe_ffn(xs, blk_expert, n_used, wg, wu, wd)
    return moe_combine(y, pos.astype(jnp.int32), p_out, x, gate)


def _rope_tables(n_tok):
    rows = n_tok // GRID_W
    r = jnp.repeat(jnp.arange(rows, dtype=f32), GRID_W)
    cl = jnp.tile(jnp.arange(GRID_W, dtype=f32), rows)
    n_freq = MLA_ROPE // 4
    inv = ROPE_BASE ** (-jnp.arange(n_freq, dtype=f32) / n_freq)
    ang = jnp.concatenate([r[:, None] * inv, cl[:, None] * inv], axis=-1)
    cos, sin = jnp.cos(ang), jnp.sin(ang)
    return jnp.tile(cos, (1, 4)), jnp.concatenate([-sin, sin, -sin, sin], axis=-1)


def _prep_in_weights(w_in_l):
    d = w_in_l.shape[0]
    nq = MLA_HEADS * (MLA_NOPE + MLA_ROPE)
    wq = w_in_l[:, :nq].reshape(d, MLA_HEADS, MLA_NOPE + MLA_ROPE)
    w_ckv = w_in_l[:, nq:nq + KV_LORA]
    w_kr = w_in_l[:, nq + KV_LORA:nq + KV_LORA + MLA_ROPE]
    rest = w_in_l[:, nq + KV_LORA + MLA_ROPE:]
    w_main = jnp.concatenate([wq[:, :, :MLA_NOPE].reshape(d, Q_NOPE_W), wq[:, :, MLA_NOPE:].reshape(d, Q_ROPE_W),
                              rest], axis=1).astype(bf16)
    w_ckr = jnp.concatenate([w_ckv, w_kr, w_kr], axis=1).astype(bf16)
    return w_main, w_ckr


def kernel(x_prompt, x_sample, c, cache_ckv, cache_krope, state_ret, c_ctx, w_ada, b_ada, norm1_g, w_in, kv_norm_g, w_ukv, ret_decay_logit, ret_gn_g, w_o_mla, w_o_fnet, w_o_ret, w_out, norm2_g, w_gate_dense, w_up_dense, w_down_dense, w_router, w_gate_moe, w_up_moe, w_down_moe, final_norm_g):
    nbc, seq, d = x_prompt.shape
    nbs, ls, _ = x_sample.shape
    depth = w_in.shape[0]
    lr = nbc * seq
    assert lr == ls, "unified row layout needs BATCH * SEQ == DEC_SEQ"
    nb = 1 + nbs
    n_rows = nb * lr

    x = jnp.concatenate([x_prompt.reshape(1, lr, d), x_sample], axis=0)
    cond = jnp.concatenate([c_ctx[None, :], c, jnp.zeros((16 - nb, d), f32)], axis=0)
    mod = adaln_all(cond, w_ada, b_ada)[:, :nb]
    cos128, sin128 = _rope_tables(ls)

    w_ch = jnp.concatenate(_dft_mats(FNET_GC), axis=1).astype(bf16)
    c_c, s_c = _dft_mats(seq)
    cs_ctx = jnp.concatenate([c_c, -s_c], axis=1).astype(bf16)
    c_s, s_s = _dft_mats(ls)
    cs_lat = jnp.concatenate([c_s, -s_s], axis=1).astype(bf16)

    ckv_list, krope_list, ret_list = [], [], []
    spare = []
    for l in range(depth):
        sh1, sc1, g1, sh2, sc2, g2 = [m.reshape(nb, 1, d) for m in jnp.split(mod[l], 6, axis=-1)]
        w_main, w_ckr = _prep_in_weights(w_in[l])
        h = norm_mod(x, norm1_g[l], sc1, sh1, bf16)
        main = inproj_main(h, w_main, cos128, sin128)
        ckv, kr, krr = inproj_ckr(h, w_ckr, kv_norm_g[l], cos128, sin128)
        w_ukv_b = w_ukv[l].astype(bf16)

        ckv_c = ckv[0].reshape(nbc, seq, KV_LORA)
        kr_c = kr[0].reshape(nbc, seq, LANES)
        kv_c = matmul(ckv_c.reshape(lr, KV_LORA).astype(bf16), w_ukv_b, bf16).reshape(nbc, seq, -1)
        main_c = main.reshape(nb * nbc, seq, N_MAIN)
        bufs = [h] + spare
        bufs += [jnp.zeros((nb, lr, d), bf16) for _ in range(3 - len(bufs))]
        bufs = [bb.reshape(nb * nbc, seq, d) for bb in bufs[:3]]
        att_c = attention(main_c, kv_c, kr_c.astype(bf16), nbc, seq, 0, nb * nbc, bufs[0])
        ab_c = dft_channels(main_c, w_ch, nbc, seq, 0).reshape(nbc, 2 * seq, FNET_GROUPS * FNET_GC)
        four_c = dft_sequence(cs_ctx, ab_c, 0, nb * nbc, bufs[1])
        ret_c, sfin_c = retention(main_c, ret_decay_logit[l], ret_gn_g[l], None, nbc, seq, 0, nb * nbc, bufs[2])
        ckv_list.append(ckv_c)
        krope_list.append(kr_c[:, :, :MLA_ROPE])
        ret_list.append(sfin_c)

        ckv_keys = jnp.concatenate([ckv[1:], cache_ckv[:, l]], axis=1).astype(bf16)
        lk = ckv_keys.shape[1]
        kv_s = matmul(ckv_keys.reshape(nbs * lk, KV_LORA), w_ukv_b, bf16).reshape(nbs, lk, -1)
        ck = cache_krope[:, l].astype(bf16)
        kr_keys = jnp.concatenate([krr[1:], jnp.concatenate([ck, ck], axis=-1)], axis=1)
        att = attention(main, kv_s, kr_keys, nbs, ls, 1, nb, att_c.reshape(nb, ls, -1)).reshape(n_rows, -1)
        ab_s = dft_channels(main, w_ch, nbs, ls, 1).reshape(nbs, 2 * ls, FNET_GROUPS * FNET_GC)
        four = dft_sequence(cs_lat, ab_s, 1, nb, four_c.reshape(nb, ls, -1)).reshape(n_rows, -1)
        ret, _ = retention(main, ret_decay_logit[l], ret_gn_g[l], state_ret[:, l], nbs, ls, 1, nb,
                           ret_c.reshape(nb, ls, -1))
        ret = ret.reshape(n_rows, -1)
        spare = [att, four]

        merged = merge_branches(att, four, ret, w_o_mla[l].astype(bf16), w_o_fnet[l].astype(bf16),
                                w_o_ret[l].astype(bf16), main.reshape(n_rows, N_MAIN))
        x = proj_residual(merged.reshape(nb, lr, d), w_out[l].astype(bf16), x, g1)

        i = l // 2
        if l % 2 == 0:
            h2 = norm_mod(x, norm2_g[l], sc2, sh2, bf16)
            x = ffn_dense(h2, w_gate_dense[i].astype(bf16), w_up_dense[i].astype(bf16),
                          w_down_dense[i].astype(bf16), x, g2)
        else:
            h2 = norm_mod(x, norm2_g[l], sc2, sh2, f32)
            x = moe_layer(h2, x, g2, w_router[i], w_gate_moe[i].astype(bf16), w_up_moe[i].astype(bf16),
                          w_down_moe[i].astype(bf16))

    y_prompt = final_norm(x, final_norm_g, 0, 1).reshape(nbc, seq, d)
    y_sample = final_norm(x, final_norm_g, 1, nbs)
    new_ckv = jnp.stack(ckv_list, axis=1)
    new_krope = jnp.stack(krope_list, axis=1)
    new_ret = jnp.stack(ret_list, axis=1)
    return (y_prompt, y_sample, new_ckv, new_krope, new_ret)
```

```python
import functools
import math

import jax
import jax.numpy as jnp
from jax import lax
from jax.experimental import pallas as pl
from jax.experimental.pallas import tpu as pltpu

f32 = jnp.float32
bf16 = jnp.bfloat16

D_MODEL = 2048
GRID_W = 64
MLA_HEADS = 16
MLA_NOPE = 128
MLA_ROPE = 64
MLA_V = 128
KV_LORA = 512
ROPE_BASE = 10000.0
FNET_GROUPS = 4
FNET_GC = 512
RET_HEADS = 8
RET_DK = 128
RET_DV = 256
RET_CHUNK = 256
N_EXPERTS = 8
EPS = 1e-6
GN_EPS = 1e-5

LANES = 128
VMEM_LIMIT_BYTES = 56 * 1024 * 1024

Q_NOPE_W = MLA_HEADS * MLA_NOPE
Q_ROPE_W = MLA_HEADS * MLA_ROPE
OFF_Q = 0
OFF_QROPE = Q_NOPE_W
OFF_F = OFF_Q + Q_NOPE_W + Q_ROPE_W
OFF_RQ = OFF_F + FNET_GROUPS * FNET_GC
OFF_RK = OFF_RQ + RET_HEADS * RET_DK
OFF_RV = OFF_RK + RET_HEADS * RET_DK
OFF_RG = OFF_RV + RET_HEADS * RET_DV
OFF_GATES = OFF_RG + RET_HEADS * RET_DV
N_MAIN = OFF_GATES + 3 * D_MODEL
CKR_W = KV_LORA + LANES
Q_SCALE = (MLA_NOPE + MLA_ROPE) ** -0.5 * math.log2(math.e)


def _cparams(*sem):
    return pltpu.CompilerParams(dimension_semantics=sem, vmem_limit_bytes=VMEM_LIMIT_BYTES)


def _tile(n, pref):
    t = min(n, pref)
    while n % t:
        t -= 8
    return t


def _silu(x):
    return x * jax.nn.sigmoid(x)


def _rope128(x, cos, sin):
    lane = lax.broadcasted_iota(jnp.int32, x.shape, 1)
    first = (lane % MLA_ROPE) < (MLA_ROPE // 2)
    swapped = jnp.where(first, pltpu.roll(x, LANES - MLA_ROPE // 2, 1), pltpu.roll(x, MLA_ROPE // 2, 1))
    return x * cos + swapped * sin


def _adaln_kernel(c_ref, w_ref, b_ref, o_ref):
    s = _silu(c_ref[...]).astype(bf16)
    o_ref[0] = jnp.dot(s, w_ref[0].astype(bf16), preferred_element_type=f32) + b_ref[0]


def adaln_all(cond, w_ada, b_ada):
    depth, d, n = w_ada.shape
    r = cond.shape[0]
    tn = _tile(n, 1024)
    return pl.pallas_call(
        _adaln_kernel,
        out_shape=jax.ShapeDtypeStruct((depth, r, n), f32),
        grid=(depth, n // tn),
        in_specs=[pl.BlockSpec((r, d), lambda l, j: (0, 0)),
                  pl.BlockSpec((1, d, tn), lambda l, j: (l, 0, j)),
                  pl.BlockSpec((1, 1, tn), lambda l, j: (l, 0, j))],
        out_specs=pl.BlockSpec((1, r, tn), lambda l, j: (l, 0, j)),
        compiler_params=_cparams("arbitrary", "arbitrary"),
        name="adaln",
    )(cond, w_ada, b_ada.reshape(depth, 1, n))


def _norm_mod_kernel(x_ref, g_ref, sc_ref, sh_ref, o_ref):
    x = x_ref[0]
    y = x * lax.rsqrt(jnp.mean(x * x, axis=-1, keepdims=True) + EPS)
    y = y * g_ref[...]
    o_ref[0] = (y * (1.0 + sc_ref[0]) + sh_ref[0]).astype(o_ref.dtype)


def norm_mod(x, g, sc, sh, out_dtype):
    nb, lr, d = x.shape
    tm = _tile(lr, 512)
    return pl.pallas_call(
        _norm_mod_kernel,
        out_shape=jax.ShapeDtypeStruct((nb, lr, d), out_dtype),
        grid=(nb, lr // tm),
        in_specs=[pl.BlockSpec((1, tm, d), lambda b, i: (b, i, 0)),
                  pl.BlockSpec((1, d), lambda b, i: (0, 0)),
                  pl.BlockSpec((1, 1, d), lambda b, i: (b, 0, 0)),
                  pl.BlockSpec((1, 1, d), lambda b, i: (b, 0, 0))],
        out_specs=pl.BlockSpec((1, tm, d), lambda b, i: (b, i, 0)),
        compiler_params=_cparams("arbitrary", "arbitrary"),
        name="norm_mod",
    )(x, g.reshape(1, d), sc, sh)


def _final_norm_kernel(x_ref, g_ref, o_ref):
    x = x_ref[0]
    y = x * lax.rsqrt(jnp.mean(x * x, axis=-1, keepdims=True) + EPS)
    o_ref[0] = y * g_ref[...]


def final_norm(x, g, b_off, nb_out):
    _, lr, d = x.shape
    tm = _tile(lr, 512)
    return pl.pallas_call(
        _final_norm_kernel,
        out_shape=jax.ShapeDtypeStruct((nb_out, lr, d), f32),
        grid=(nb_out, lr // tm),
        in_specs=[pl.BlockSpec((1, tm, d), lambda b, i: (b + b_off, i, 0)),
                  pl.BlockSpec((1, d), lambda b, i: (0, 0))],
        out_specs=pl.BlockSpec((1, tm, d), lambda b, i: (b, i, 0)),
        compiler_params=_cparams("arbitrary", "arbitrary"),
        name="final_norm",
    )(x, g.reshape(1, d))


def _inproj_kernel(h_ref, w_ref, cos_ref, sin_ref, o_ref, *, tn, nchunk):
    b = pl.program_id(0)
    j = pl.program_id(2)
    tm = h_ref.shape[1]
    rc = tm // nchunk
    j_rope = OFF_QROPE // tn
    j_rk = OFF_RK // tn
    j_rv = OFF_RV // tn
    j_rg = OFF_RG // tn
    j_gates = OFF_GATES // tn
    is_q = j < OFF_F // tn
    is_rope = jnp.logical_and(jnp.logical_and(j >= j_rope, is_q), b >= 1)
    is_qplain = jnp.logical_and(is_q, jnp.logical_not(is_rope))
    is_rk = jnp.logical_and(j >= j_rk, j < j_rv)
    is_silu = jnp.logical_and(j >= j_rg, j < j_gates)
    is_sig = j >= j_gates
    plain = jnp.logical_not(is_q | is_rk | is_silu | is_sig)

    def run(epilogue):
        pending = None
        for r in range(nchunk):
            rows = slice(r * rc, (r + 1) * rc)
            acc = jnp.dot(h_ref[0, rows, :], w_ref[...], preferred_element_type=f32)
            if pending is not None:
                prow, pacc = pending
                o_ref[0, prow, :] = epilogue(pacc, prow).astype(o_ref.dtype)
            pending = (rows, acc)
        prow, pacc = pending
        o_ref[0, prow, :] = epilogue(pacc, prow).astype(o_ref.dtype)

    @pl.when(plain)
    def _():
        run(lambda acc, rows: acc)

    @pl.when(is_qplain)
    def _():
        run(lambda acc, rows: acc * Q_SCALE)

    @pl.when(is_rope)
    def _():
        def rope(acc, rows):
            cos = cos_ref[rows, :] * Q_SCALE
            sin = sin_ref[rows, :] * Q_SCALE
            return jnp.concatenate([_rope128(acc[:, s * LANES:(s + 1) * LANES], cos, sin)
                                    for s in range(tn // LANES)], axis=1)
        run(rope)

    @pl.when(is_rk)
    def _():
        run(lambda acc, rows: acc * (RET_DK ** -0.5))

    @pl.when(is_silu)
    def _():
        run(lambda acc, rows: _silu(acc))

    @pl.when(is_sig)
    def _():
        run(lambda acc, rows: jax.nn.sigmoid(acc))


def inproj_main(h, w_main, cos128, sin128):
    nb, lr, d = h.shape
    tm = _tile(lr, 1024)
    tn = 1024
    return pl.pallas_call(
        functools.partial(_inproj_kernel, tn=tn, nchunk=4),
        out_shape=jax.ShapeDtypeStruct((nb, lr, N_MAIN), bf16),
        grid=(nb, lr // tm, N_MAIN // tn),
        in_specs=[pl.BlockSpec((1, tm, d), lambda b, i, j: (b, i, 0)),
                  pl.BlockSpec((d, tn), lambda b, i, j: (0, j)),
                  pl.BlockSpec((tm, LANES), lambda b, i, j: (i, 0)),
                  pl.BlockSpec((tm, LANES), lambda b, i, j: (i, 0))],
        out_specs=pl.BlockSpec((1, tm, tn), lambda b, i, j: (b, i, j)),
        compiler_params=_cparams("arbitrary", "arbitrary", "arbitrary"),
        name="inproj_main",
    )(h, w_main, cos128, sin128)


def _ckr_kernel(h_ref, w_ref, g_ref, cos_ref, sin_ref, ckv_ref, kr_ref, krr_ref):
    acc = jnp.dot(h_ref[0], w_ref[...], preferred_element_type=f32)
    c = acc[:, :KV_LORA]
    y = c * lax.rsqrt(jnp.mean(c * c, axis=-1, keepdims=True) + EPS)
    ckv_ref[0] = y * g_ref[...]
    kr = acc[:, KV_LORA:]
    kr_ref[0] = kr
    krr_ref[0] = _rope128(kr, cos_ref[...], sin_ref[...]).astype(krr_ref.dtype)


def inproj_ckr(h, w_ckr, kv_g, cos128, sin128):
    nb, lr, d = h.shape
    tm = _tile(lr, 1024)
    return pl.pallas_call(
        _ckr_kernel,
        out_shape=(jax.ShapeDtypeStruct((nb, lr, KV_LORA), f32),
                   jax.ShapeDtypeStruct((nb, lr, LANES), f32),
                   jax.ShapeDtypeStruct((nb, lr, LANES), bf16)),
        grid=(nb, lr // tm),
        in_specs=[pl.BlockSpec((1, tm, d), lambda b, i: (b, i, 0)),
                  pl.BlockSpec((d, CKR_W), lambda b, i: (0, 0)),
                  pl.BlockSpec((1, KV_LORA), lambda b, i: (0, 0)),
                  pl.BlockSpec((tm, LANES), lambda b, i: (i, 0)),
                  pl.BlockSpec((tm, LANES), lambda b, i: (i, 0))],
        out_specs=(pl.BlockSpec((1, tm, KV_LORA), lambda b, i: (b, i, 0)),
                   pl.BlockSpec((1, tm, LANES), lambda b, i: (b, i, 0)),
                   pl.BlockSpec((1, tm, LANES), lambda b, i: (b, i, 0))),
        compiler_params=_cparams("arbitrary", "arbitrary"),
        name="inproj_ckr",
    )(h, w_ckr, kv_g.reshape(1, KV_LORA), cos128, sin128)


def _mm_kernel(x_ref, w_ref, o_ref):
    o_ref[...] = jnp.dot(x_ref[...], w_ref[...], preferred_element_type=f32).astype(o_ref.dtype)


def matmul(x, w, out_dtype, tm_pref=512, tn_pref=4096):
    m, k = x.shape
    _, n = w.shape
    tm = _tile(m, tm_pref)
    tn = _tile(n, tn_pref)
    return pl.pallas_call(
        _mm_kernel,
        out_shape=jax.ShapeDtypeStruct((m, n), out_dtype),
        grid=(m // tm, n // tn),
        in_specs=[pl.BlockSpec((tm, k), lambda i, j: (i, 0)),
                  pl.BlockSpec((k, tn), lambda i, j: (0, j))],
        out_specs=pl.BlockSpec((tm, tn), lambda i, j: (i, j)),
        compiler_params=_cparams("arbitrary", "arbitrary"),
        name="matmul",
    )(x, w)


def _attn_single_kernel(qn_ref, qr_ref, kn_ref, v_ref, kr_ref, o_ref, kcat_ref, vt_ref, s_ref, *, nk, tk, nparts):
    h = pl.program_id(1)
    qi = pl.program_id(2)
    lk = kcat_ref.shape[0]

    @pl.when(qi == 0)
    def _():
        kcat_ref[:, :MLA_NOPE] = kn_ref[0]
        kcat_ref[:, MLA_NOPE:] = kr_ref[0]
        for c in range(nk):
            vt_ref[:, c * tk:(c + 1) * tk] = v_ref[0, c * tk:(c + 1) * tk, :].astype(f32).T.astype(bf16)

    qr = qr_ref[0]
    lane = lax.broadcasted_iota(jnp.int32, qr.shape, 1)
    qr = jnp.where((lane // MLA_ROPE) == (h % 2), qr, jnp.zeros_like(qr))
    qcat = jnp.concatenate([qn_ref[0], qr], axis=1)
    tq = qcat.shape[0]

    bounds = [((nk * i) // nparts) * tk for i in range(nparts + 1)]
    m = jnp.full((1, tq), -jnp.inf, f32)
    pending = None
    for i in range(nparts):
        r0, r1 = bounds[i], bounds[i + 1]
        s = lax.dot_general(kcat_ref[r0:r1, :], qcat, (((1,), (1,)), ((), ())), preferred_element_type=f32)
        s_ref[r0:r1, :] = s
        if pending is not None:
            m = jnp.maximum(m, jnp.max(pending, axis=0, keepdims=True))
        pending = s
    m = jnp.maximum(m, jnp.max(pending, axis=0, keepdims=True))

    l = jnp.zeros((1, tq), f32)
    acc = jnp.zeros((MLA_V, tq), f32)
    for c in range(nk):
        p = jnp.exp2(s_ref[c * tk:(c + 1) * tk, :] - m)
        l = l + jnp.sum(p, axis=0, keepdims=True)
        acc = acc + jnp.dot(vt_ref[:, c * tk:(c + 1) * tk], p.astype(bf16), preferred_element_type=f32)
    o_ref[0] = (acc * (1.0 / l)).T.astype(o_ref.dtype)


def _attn_kernel(qn_ref, qr_ref, kn_ref, v_ref, kr_ref, o_ref, kcat_ref, vt_ref, sa_ref, sb_ref, ma_ref, mb_ref,
                 *, nk, tk, nparts):
    h = pl.program_id(1)
    i = pl.program_id(2)

    @pl.when(i == 0)
    def _():
        kcat_ref[:, :MLA_NOPE] = kn_ref[0]
        kcat_ref[:, MLA_NOPE:] = kr_ref[0]
        for c in range(nk):
            vt_ref[:, c * tk:(c + 1) * tk] = v_ref[0, c * tk:(c + 1) * tk, :].astype(f32).T.astype(bf16)
        sb_ref[...] = jnp.zeros(sb_ref.shape, f32)
        mb_ref[...] = jnp.zeros(mb_ref.shape, f32)

    def step(s_new, s_old, m_new_ref, m_old_ref):
        qr = qr_ref[0]
        lane = lax.broadcasted_iota(jnp.int32, qr.shape, 1)
        qr = jnp.where((lane // MLA_ROPE) == (h % 2), qr, jnp.zeros_like(qr))
        qcat = jnp.concatenate([qn_ref[0], qr], axis=1)
        tq = qcat.shape[0]
        m_old = m_old_ref[0:1, :]
        bounds = [(nk * g) // nparts for g in range(nparts + 1)]
        m = jnp.full((1, tq), -jnp.inf, f32)
        l = jnp.zeros((1, tq), f32)
        acc = jnp.zeros((MLA_V, tq), f32)
        for g in range(nparts):
            r0, r1 = bounds[g] * tk, bounds[g + 1] * tk
            s = lax.dot_general(kcat_ref[r0:r1, :], qcat, (((1,), (1,)), ((), ())), preferred_element_type=f32)
            s_new[r0:r1, :] = s
            m = jnp.maximum(m, jnp.max(s, axis=0, keepdims=True))
            for c in range(bounds[g], bounds[g + 1]):
                p = jnp.exp2(s_old[c * tk:(c + 1) * tk, :] - m_old)
                l = l + jnp.sum(p, axis=0, keepdims=True)
                acc = acc + jnp.dot(vt_ref[:, c * tk:(c + 1) * tk], p.astype(bf16), preferred_element_type=f32)
        m_new_ref[...] = jnp.broadcast_to(m, m_new_ref.shape)
        o_ref[0] = (acc * (1.0 / l)).T.astype(o_ref.dtype)

    @pl.when(i % 2 == 0)
    def _():
        step(sa_ref, sb_ref, ma_ref, mb_ref)

    @pl.when(i % 2 == 1)
    def _():
        step(sb_ref, sa_ref, mb_ref, ma_ref)


def _with_prev(kern, in_specs, args, prev):
    if prev is None:
        return kern, in_specs, args, {}
    n_in = len(args)

    def wrapped(*refs):
        return kern(*refs[:n_in], *refs[n_in + 1:])

    return wrapped, in_specs + [pl.BlockSpec(memory_space=pl.ANY)], args + [prev], {n_in: 0}


def attention(main, kv, krope, nb, lq, b_off, nb_total, prev):
    lk = kv.shape[1]
    tq = _tile(lq, 256)
    tk = _tile(lk, 256)
    nq = lq // tq
    qr_blk = OFF_QROPE // LANES
    nk = lk // tk
    nparts = min(4, nk)
    skewed = nq > 1
    if skewed:
        steps = nq + 1
        q_idx = lambda i: jnp.minimum(i, nq - 1)
        o_idx = lambda i: jnp.maximum(i - 1, 0)
        kern = functools.partial(_attn_kernel, nk=nk, tk=tk, nparts=nparts)
        scratch = [pltpu.VMEM((lk, MLA_NOPE + LANES), bf16), pltpu.VMEM((MLA_V, lk), bf16),
                   pltpu.VMEM((lk, tq), f32), pltpu.VMEM((lk, tq), f32),
                   pltpu.VMEM((8, tq), f32), pltpu.VMEM((8, tq), f32)]
    else:
        steps = nq
        q_idx = lambda i: i
        o_idx = lambda i: i
        kern = functools.partial(_attn_single_kernel, nk=nk, tk=tk, nparts=nparts)
        scratch = [pltpu.VMEM((lk, MLA_NOPE + LANES), bf16), pltpu.VMEM((MLA_V, lk), bf16),
                   pltpu.VMEM((lk, tq), f32)]
    in_specs = [pl.BlockSpec((1, tq, MLA_NOPE), lambda b, h, i: (b + b_off, q_idx(i), h)),
                pl.BlockSpec((1, tq, LANES), lambda b, h, i: (b + b_off, q_idx(i), qr_blk + h // 2)),
                pl.BlockSpec((1, lk, MLA_NOPE), lambda b, h, i: (b, 0, 2 * h)),
                pl.BlockSpec((1, lk, MLA_V), lambda b, h, i: (b, 0, 2 * h + 1)),
                pl.BlockSpec((1, lk, LANES), lambda b, h, i: (b, 0, 0))]
    kern, in_specs, args, aliases = _with_prev(kern, in_specs, [main, main, kv, kv, krope], prev)
    return pl.pallas_call(
        kern,
        out_shape=jax.ShapeDtypeStruct((nb_total, lq, MLA_HEADS * MLA_V), bf16),
        grid=(nb, MLA_HEADS, steps),
        in_specs=in_specs,
        out_specs=pl.BlockSpec((1, tq, MLA_V), lambda b, h, i: (b + b_off, o_idx(i), h)),
        scratch_shapes=scratch,
        input_output_aliases=aliases,
        compiler_params=_cparams("arbitrary", "arbitrary", "arbitrary"),
        name="attention",
    )(*args)


def _dft_ch_kernel(xa_ref, xb_ref, w_ref, o_ref):
    half = FNET_GROUPS // 2
    for g in range(FNET_GROUPS):
        x_ref = xa_ref if g < half else xb_ref
        lo = (g % half) * FNET_GC
        y = jnp.dot(x_ref[0, :, lo:lo + FNET_GC], w_ref[...], preferred_element_type=f32)
        o_ref[0, 0, :, g * FNET_GC:(g + 1) * FNET_GC] = y[:, :FNET_GC].astype(o_ref.dtype)
        o_ref[0, 1, :, g * FNET_GC:(g + 1) * FNET_GC] = y[:, FNET_GC:].astype(o_ref.dtype)


def dft_channels(main, w_ch, nb, ls, b_off):
    tm = _tile(ls, 512)
    wb = FNET_GROUPS * FNET_GC // 2
    f_blk = OFF_F // wb
    return pl.pallas_call(
        _dft_ch_kernel,
        out_shape=jax.ShapeDtypeStruct((nb, 2, ls, FNET_GROUPS * FNET_GC), bf16),
        grid=(nb, ls // tm),
        in_specs=[pl.BlockSpec((1, tm, wb), lambda b, i: (b + b_off, i, f_blk)),
                  pl.BlockSpec((1, tm, wb), lambda b, i: (b + b_off, i, f_blk + 1)),
                  pl.BlockSpec((FNET_GC, 2 * FNET_GC), lambda b, i: (0, 0))],
        out_specs=pl.BlockSpec((1, 2, tm, FNET_GROUPS * FNET_GC), lambda b, i: (b, 0, i, 0)),
        compiler_params=_cparams("arbitrary", "arbitrary"),
        name="dft_channels",
    )(main, main, w_ch)


def _dft_seq_kernel(l_ref, r_ref, o_ref):
    o_ref[0] = jnp.dot(l_ref[...], r_ref[0], preferred_element_type=f32).astype(o_ref.dtype)


def dft_sequence(cs, ab, b_off, nb_total, prev):
    nb, k, w = ab.shape
    ls = cs.shape[0]
    tm = _tile(ls, 512)
    tn = _tile(w, 512)
    in_specs = [pl.BlockSpec((tm, k), lambda b, j, i: (i, 0)),
                pl.BlockSpec((1, k, tn), lambda b, j, i: (b, 0, j))]
    kern, in_specs, args, aliases = _with_prev(_dft_seq_kernel, in_specs, [cs, ab], prev)
    return pl.pallas_call(
        kern,
        out_shape=jax.ShapeDtypeStruct((nb_total, ls, w), bf16),
        grid=(nb, w // tn, ls // tm),
        in_specs=in_specs,
        out_specs=pl.BlockSpec((1, tm, tn), lambda b, j, i: (b + b_off, i, j)),
        input_output_aliases=aliases,
        compiler_params=_cparams("arbitrary", "arbitrary", "arbitrary"),
        name="dft_sequence",
    )(*args)


def _dft_ch_sym_kernel(xa_ref, xb_ref, ra_ref, rb_ref, w_ref, o_ref):
    half = FNET_GROUPS // 2
    for g in range(FNET_GROUPS):
        x_ref, r_ref = (xa_ref, ra_ref) if g < half else (xb_ref, rb_ref)
        lo = (g % half) * FNET_GC
        x = x_ref[0, :, lo:lo + FNET_GC].astype(f32)
        r = r_ref[0, :, lo:lo + FNET_GC].astype(f32)
        cols = slice(g * FNET_GC, (g + 1) * FNET_GC)
        o_ref[0, 0, :, cols] = jnp.dot((x + r).astype(bf16), w_ref[:, :FNET_GC],
                                       preferred_element_type=f32).astype(o_ref.dtype)
        o_ref[0, 1, :, cols] = jnp.dot((x - r).astype(bf16), w_ref[:, FNET_GC:],
                                       preferred_element_type=f32).astype(o_ref.dtype)


def dft_channels_sym(main, xrev, w_ch, nb, ls, b_off):
    lh = ls // 2
    tm = _tile(lh, 512)
    wb = FNET_GROUPS * FNET_GC // 2
    f_blk = OFF_F // wb
    return pl.pallas_call(
        _dft_ch_sym_kernel,
        out_shape=jax.ShapeDtypeStruct((nb, 2, lh, FNET_GROUPS * FNET_GC), bf16),
        grid=(nb, lh // tm),
        in_specs=[pl.BlockSpec((1, tm, wb), lambda b, i: (b + b_off, i, f_blk)),
                  pl.BlockSpec((1, tm, wb), lambda b, i: (b + b_off, i, f_blk + 1)),
                  pl.BlockSpec((1, tm, wb), lambda b, i: (b, i, 0)),
                  pl.BlockSpec((1, tm, wb), lambda b, i: (b, i, 1)),
                  pl.BlockSpec((FNET_GC, 2 * FNET_GC), lambda b, i: (0, 0))],
        out_specs=pl.BlockSpec((1, 2, tm, FNET_GROUPS * FNET_GC), lambda b, i: (b, 0, i, 0)),
        compiler_params=_cparams("arbitrary", "arbitrary"),
        name="dft_channels_sym",
    )(main, main, xrev, xrev, w_ch)


def _dft_seq_sym_kernel(l_ref, r_ref, a_ref, o_ref, *, mid_scale):
    y = jnp.dot(l_ref[...], r_ref[0], preferred_element_type=f32)
    rows = lax.broadcasted_iota(jnp.int32, y.shape, 0)
    sign = jnp.where(rows % 2 == 0, mid_scale, -mid_scale)
    o_ref[0] = (y + sign * a_ref[0, 0:1, :]).astype(o_ref.dtype)


def dft_sequence_sym(cs, ab, a_mid, b_off, nb_total, prev):
    nb, k, w = ab.shape
    ls = cs.shape[0]
    tm = _tile(ls, 512)
    tn = _tile(w, 512)
    in_specs = [pl.BlockSpec((tm, k), lambda b, j, i: (i, 0)),
                pl.BlockSpec((1, k, tn), lambda b, j, i: (b, 0, j)),
                pl.BlockSpec((1, 8, tn), lambda b, j, i: (b, 0, j))]
    kern, in_specs, args, aliases = _with_prev(functools.partial(_dft_seq_sym_kernel, mid_scale=ls ** -0.5),
                                               in_specs, [cs, ab, a_mid], prev)
    return pl.pallas_call(
        kern,
        out_shape=jax.ShapeDtypeStruct((nb_total, ls, w), bf16),
        grid=(nb, w // tn, ls // tm),
        in_specs=in_specs,
        out_specs=pl.BlockSpec((1, tm, tn), lambda b, j, i: (b + b_off, i, j)),
        input_output_aliases=aliases,
        compiler_params=_cparams("arbitrary", "arbitrary", "arbitrary"),
        name="dft_sequence_sym",
    )(*args)


def _dft_mats(n):
    idx = jnp.arange(n, dtype=jnp.int32)
    ang = ((idx[:, None] * idx[None, :]) % n).astype(f32) * (2.0 * math.pi / n)
    s = n ** -0.5
    return jnp.cos(ang) * s, jnp.sin(ang) * s


def _log_sigmoid(x):
    return jnp.minimum(x, 0.0) - jnp.log1p(jnp.exp(-jnp.abs(x)))


def _ret_kernel(*refs, nc, use_s0):
    if use_s0:
        dl_ref, q_ref, k_ref, v_ref, g_ref, gn_ref, s0_ref, o_ref, sfin_ref, of_ref, ob_ref = refs
    else:
        dl_ref, q_ref, k_ref, v_ref, g_ref, gn_ref, o_ref, sfin_ref, of_ref, ob_ref = refs
        s0_ref = None
    C = RET_CHUNK
    lgf = _log_sigmoid(dl_ref[0, 0])[0:1, :]
    lgb = _log_sigmoid(dl_ref[1, 0])[0:1, :]
    lgf2 = jnp.concatenate([lgf, lgf], axis=1)
    lgb2 = jnp.concatenate([lgb, lgb], axis=1)
    lgf_c = jnp.concatenate([lgf] * (C // LANES), axis=1)
    lgb_c = jnp.concatenate([lgb] * (C // LANES), axis=1)
    rows = lax.broadcasted_iota(jnp.int32, (C, C), 0)
    cols = lax.broadcasted_iota(jnp.int32, (C, C), 1)
    rel = (rows - cols).astype(f32)
    d_f = jnp.where(rel >= 0, jnp.exp(jnp.maximum(rel, 0.0) * lgf_c), 0.0)
    d_b = jnp.where(rel <= 0, jnp.exp(jnp.maximum(-rel, 0.0) * lgb_c), 0.0)
    r_v = lax.broadcasted_iota(jnp.int32, (C, RET_DV), 0).astype(f32)
    r_k = lax.broadcasted_iota(jnp.int32, (C, RET_DK), 0).astype(f32)
    qdec_f = jnp.exp((r_v + 1.0) * lgf2)
    qdec_b = jnp.exp((C - r_v) * lgb2)
    kdec_f = jnp.exp((C - 1.0 - r_k) * lgf)
    kdec_b = jnp.exp(r_k * lgb)
    cdec_f = jnp.exp(C * lgf2)
    cdec_b = jnp.exp(C * lgb2)

    def chunk(c, s, dmat, qdec, kdec, cdec):
        r = pl.multiple_of(c * C, C)
        q = q_ref[0, pl.ds(r, C), :]
        k = k_ref[0, pl.ds(r, C), :]
        v = v_ref[0, pl.ds(r, C), :]
        inner = lax.dot_general(q, k, (((1,), (1,)), ((), ())), preferred_element_type=f32) * dmat
        o = (jnp.dot(inner.astype(bf16), v, preferred_element_type=f32)
             + jnp.dot(q, s.astype(bf16), preferred_element_type=f32) * qdec)
        kd = (k.astype(f32) * kdec).T.astype(bf16)
        s_new = s * cdec + jnp.dot(kd, v, preferred_element_type=f32)
        return r, o, s_new

    def body(c, carry):
        sf, sb = carry
        r, o, sf = chunk(c, sf, d_f, qdec_f, kdec_f, cdec_f)
        of_ref[pl.ds(r, C), :] = o
        r, o, sb = chunk(nc - 1 - c, sb, d_b, qdec_b, kdec_b, cdec_b)
        ob_ref[pl.ds(r, C), :] = o
        return sf, sb

    if use_s0:
        init = (s0_ref[0, 0, 0], s0_ref[0, 1, 0])
    else:
        init = (jnp.zeros((RET_DK, RET_DV), f32), jnp.zeros((RET_DK, RET_DV), f32))
    sf, sb = lax.fori_loop(0, nc, body, init, unroll=2)
    sfin_ref[0, 0, 0] = sf
    sfin_ref[0, 1, 0] = sb

    gn = gn_ref[...]

    def norm_body(c, carry):
        r = pl.multiple_of(c * C, C)
        o = of_ref[pl.ds(r, C), :] + ob_ref[pl.ds(r, C), :]
        mu = jnp.mean(o, axis=-1, keepdims=True)
        d = o - mu
        y = d * lax.rsqrt(jnp.mean(d * d, axis=-1, keepdims=True) + GN_EPS) * gn
        o_ref[0, pl.ds(r, C), :] = (y * g_ref[0, pl.ds(r, C), :].astype(f32)).astype(o_ref.dtype)
        return carry

    lax.fori_loop(0, nc, norm_body, 0, unroll=2)


def retention(main, decay_logit, gn_g, s0, nb, ls, b_off, nb_total, prev):
    use_s0 = s0 is not None
    nc = ls // RET_CHUNK
    dl = jnp.broadcast_to(decay_logit.astype(f32)[:, :, None, None], (2, RET_HEADS, 8, LANES))
    in_specs = [pl.BlockSpec((2, 1, 8, LANES), lambda b, h: (0, h, 0, 0)),
                pl.BlockSpec((1, ls, RET_DK), lambda b, h: (b + b_off, 0, OFF_RQ // RET_DK + h)),
                pl.BlockSpec((1, ls, RET_DK), lambda b, h: (b + b_off, 0, OFF_RK // RET_DK + h)),
                pl.BlockSpec((1, ls, RET_DV), lambda b, h: (b + b_off, 0, OFF_RV // RET_DV + h)),
                pl.BlockSpec((1, ls, RET_DV), lambda b, h: (b + b_off, 0, OFF_RG // RET_DV + h)),
                pl.BlockSpec((1, RET_DV), lambda b, h: (0, h))]
    args = [dl, main, main, main, main, gn_g.reshape(1, RET_HEADS * RET_DV)]
    if use_s0:
        in_specs.append(pl.BlockSpec((1, 2, 1, RET_DK, RET_DV), lambda b, h: (b, 0, h, 0, 0)))
        args.append(s0)
    kern, in_specs, args, aliases = _with_prev(functools.partial(_ret_kernel, nc=nc, use_s0=use_s0),
                                               in_specs, args, prev)
    return pl.pallas_call(
        kern,
        out_shape=(jax.ShapeDtypeStruct((nb_total, ls, RET_HEADS * RET_DV), bf16),
                   jax.ShapeDtypeStruct((nb, 2, RET_HEADS, RET_DK, RET_DV), f32)),
        grid=(nb, RET_HEADS),
        in_specs=in_specs,
        out_specs=(pl.BlockSpec((1, ls, RET_DV), lambda b, h: (b + b_off, 0, h)),
                   pl.BlockSpec((1, 2, 1, RET_DK, RET_DV), lambda b, h: (b, 0, h, 0, 0))),
        scratch_shapes=[pltpu.VMEM((ls, RET_DV), f32), pltpu.VMEM((ls, RET_DV), f32)],
        input_output_aliases=aliases,
        compiler_params=_cparams("arbitrary", "arbitrary"),
        name="retention",
    )(*args)


def _merge_kernel(a_ref, f_ref, r_ref, wa_ref, wf_ref, wr_ref, ga_ref, gf_ref, gr_ref, o_ref):
    m = ga_ref[...].astype(f32) * jnp.dot(a_ref[...], wa_ref[...], preferred_element_type=f32)
    m += gf_ref[...].astype(f32) * jnp.dot(f_ref[...], wf_ref[...], preferred_element_type=f32)
    m += gr_ref[...].astype(f32) * jnp.dot(r_ref[...], wr_ref[...], preferred_element_type=f32)
    o_ref[...] = m.astype(o_ref.dtype)


def merge_branches(att, four, ret, wa, wf, wr, main2d):
    m, d = att.shape
    tm = _tile(m, 512)
    tn = 512
    gb = OFF_GATES // tn
    nd = d // tn
    x_spec = pl.BlockSpec((tm, d), lambda i, j: (i, 0))
    w_spec = pl.BlockSpec((d, tn), lambda i, j: (0, j))
    return pl.pallas_call(
        _merge_kernel,
        out_shape=jax.ShapeDtypeStruct((m, d), bf16),
        grid=(m // tm, nd),
        in_specs=[x_spec, x_spec, x_spec, w_spec, w_spec, w_spec,
                  pl.BlockSpec((tm, tn), lambda i, j: (i, gb + j)),
                  pl.BlockSpec((tm, tn), lambda i, j: (i, gb + nd + j)),
                  pl.BlockSpec((tm, tn), lambda i, j: (i, gb + 2 * nd + j))],
        out_specs=pl.BlockSpec((tm, tn), lambda i, j: (i, j)),
        compiler_params=_cparams("arbitrary", "arbitrary"),
        name="merge_branches",
    )(att, four, ret, wa, wf, wr, main2d, main2d, main2d)


def _proj_res_kernel(m_ref, w_ref, x_ref, g_ref, o_ref):
    y = jnp.dot(m_ref[0], w_ref[...], preferred_element_type=f32)
    o_ref[0] = x_ref[0] + g_ref[0] * y


def proj_residual(merged, w, x, gate):
    nb, lr, d = x.shape
    tm = _tile(lr, 1024)
    tn = 1024
    return pl.pallas_call(
        _proj_res_kernel,
        out_shape=jax.ShapeDtypeStruct((nb, lr, d), f32),
        grid=(nb, lr // tm, d // tn),
        in_specs=[pl.BlockSpec((1, tm, d), lambda b, i, j: (b, i, 0)),
                  pl.BlockSpec((d, tn), lambda b, i, j: (0, j)),
                  pl.BlockSpec((1, tm, tn), lambda b, i, j: (b, i, j)),
                  pl.BlockSpec((1, 1, tn), lambda b, i, j: (b, 0, j))],
        out_specs=pl.BlockSpec((1, tm, tn), lambda b, i, j: (b, i, j)),
        compiler_params=_cparams("arbitrary", "arbitrary", "arbitrary"),
        name="proj_residual",
    )(merged, w, x, gate)


def _ffn_kernel(h_ref, wg_ref, wu_ref, wd_ref, x_ref, g_ref, o_ref, acc_ref):
    j = pl.program_id(2)

    @pl.when(j == 0)
    def _():
        acc_ref[...] = jnp.zeros_like(acc_ref)

    h = h_ref[0]
    a = jnp.dot(h, wg_ref[...], preferred_element_type=f32)
    u = jnp.dot(h, wu_ref[...], preferred_element_type=f32)
    t = (_silu(a) * u).astype(bf16)
    acc_ref[...] += jnp.dot(t, wd_ref[...], preferred_element_type=f32)

    @pl.when(j == pl.num_programs(2) - 1)
    def _():
        o_ref[0] = x_ref[0] + g_ref[0] * acc_ref[...]


def ffn_dense(h, wg, wu, wd, x, gate):
    nb, lr, d = x.shape
    dff = wg.shape[1]
    tm = _tile(lr, 512)
    tf = _tile(dff, 512)
    return pl.pallas_call(
        _ffn_kernel,
        out_shape=jax.ShapeDtypeStruct((nb, lr, d), f32),
        grid=(nb, lr // tm, dff // tf),
        in_specs=[pl.BlockSpec((1, tm, d), lambda b, i, j: (b, i, 0)),
                  pl.BlockSpec((d, tf), lambda b, i, j: (0, j)),
                  pl.BlockSpec((d, tf), lambda b, i, j: (0, j)),
                  pl.BlockSpec((tf, d), lambda b, i, j: (j, 0)),
                  pl.BlockSpec((1, tm, d), lambda b, i, j: (b, i, 0)),
                  pl.BlockSpec((1, 1, d), lambda b, i, j: (b, 0, 0))],
        out_specs=pl.BlockSpec((1, tm, d), lambda b, i, j: (b, i, 0)),
        scratch_shapes=[pltpu.VMEM((tm, d), f32)],
        compiler_params=_cparams("arbitrary", "arbitrary", "arbitrary"),
        name="ffn_dense",
    )(h, wg, wu, wd, x, gate)


MOE_TM = 512
MOE_TT = 256


def _router_kernel(h_ref, w_ref, e_ref, p_ref):
    logits = jnp.dot(h_ref[...].astype(bf16), w_ref[...], preferred_element_type=f32)
    lane = lax.broadcasted_iota(jnp.int32, logits.shape, 1)
    lane_f = lane.astype(f32)
    neg = jnp.full_like(logits, -jnp.inf)
    l0 = jnp.where(lane < N_EXPERTS, logits, neg)
    v1 = jnp.max(l0, axis=1, keepdims=True)
    i1 = jnp.min(jnp.where(l0 == v1, lane_f, float(LANES)), axis=1, keepdims=True)
    l1 = jnp.where(lane_f == i1, neg, l0)
    v2 = jnp.max(l1, axis=1, keepdims=True)
    i2 = jnp.min(jnp.where(l1 == v2, lane_f, float(LANES)), axis=1, keepdims=True)
    e = jnp.exp(v2 - v1)
    den = 1.0 + e
    e_ref[...] = jnp.where(lane == 0, i1, jnp.where(lane == 1, i2, 0.0)).astype(jnp.int32)
    p_ref[...] = jnp.where(lane == 0, 1.0 / den, jnp.where(lane == 1, e / den, 0.0))


def moe_router(h2d, w_router_pad):
    n, d = h2d.shape
    tm = _tile(n, 512)
    return pl.pallas_call(
        _router_kernel,
        out_shape=(jax.ShapeDtypeStruct((n, LANES), jnp.int32), jax.ShapeDtypeStruct((n, LANES), f32)),
        grid=(n // tm,),
        in_specs=[pl.BlockSpec((tm, d), lambda i: (i, 0)),
                  pl.BlockSpec((d, LANES), lambda i: (0, 0))],
        out_specs=(pl.BlockSpec((tm, LANES), lambda i: (i, 0)), pl.BlockSpec((tm, LANES), lambda i: (i, 0))),
        compiler_params=_cparams("arbitrary"),
        name="moe_router",
    )(h2d, w_router_pad)


def _dispatch_kernel(pos_ref, h_ref, xs_in_ref, xs_ref, sem):
    del xs_in_ref
    tt = h_ref.shape[0]

    def row_copy(t, p):
        return pltpu.make_async_copy(h_ref.at[pl.ds(t, 1), :], xs_ref.at[pl.ds(p, 1), :], sem)

    def start(t, c):
        row_copy(t, pos_ref[0, 0, 2 * t]).start()
        row_copy(t, pos_ref[0, 0, 2 * t + 1]).start()
        return c

    lax.fori_loop(0, tt, start, 0, unroll=8)
    tile_copy = pltpu.make_async_copy(h_ref, xs_ref.at[pl.ds(0, tt), :], sem)
    tile_copy.wait()
    tile_copy.wait()


def moe_dispatch(h2d, pos, p_rows):
    n, d = h2d.shape
    tt = _tile(n, MOE_TT)
    pos3 = pos.reshape(n // tt, 1, 2 * tt)
    xs0 = jnp.zeros((p_rows, d), h2d.dtype)
    return pl.pallas_call(
        _dispatch_kernel,
        out_shape=jax.ShapeDtypeStruct((p_rows, d), h2d.dtype),
        grid=(n // tt,),
        in_specs=[pl.BlockSpec((1, 1, 2 * tt), lambda i: (i, 0, 0), memory_space=pltpu.SMEM),
                  pl.BlockSpec((tt, d), lambda i: (i, 0)),
                  pl.BlockSpec(memory_space=pl.ANY)],
        out_specs=pl.BlockSpec(memory_space=pl.ANY),
        scratch_shapes=[pltpu.SemaphoreType.DMA(())],
        input_output_aliases={2: 0},
        compiler_params=_cparams("arbitrary"),
        name="moe_dispatch",
    )(pos3, h2d, xs0)


def _moe_ffn_kernel(be_ref, nu_ref, x_ref, wg_ref, wu_ref, wd_ref, y_ref, xb_ref, acc_ref):
    del be_ref
    i = pl.program_id(0)
    j = pl.program_id(1)
    active = i < nu_ref[0]

    @pl.when(jnp.logical_and(active, j == 0))
    def _():
        xb_ref[...] = x_ref[...].astype(bf16)
        acc_ref[...] = jnp.zeros_like(acc_ref)

    @pl.when(active)
    def _():
        xb = xb_ref[...]
        a = jnp.dot(xb, wg_ref[0], preferred_element_type=f32)
        u = jnp.dot(xb, wu_ref[0], preferred_element_type=f32)
        t = (_silu(a) * u).astype(bf16)
        acc_ref[...] += jnp.dot(t, wd_ref[0], preferred_element_type=f32)

    @pl.when(j == pl.num_programs(1) - 1)
    def _():
        y_ref[...] = jnp.where(active, acc_ref[...], 0.0)


def moe_ffn(xs, blk_expert, n_used, wg, wu, wd):
    p_rows, d = xs.shape
    dff = wg.shape[2]
    tm = MOE_TM
    tf = _tile(dff, 512)
    nf = dff // tf

    def jj(i, j, nu):
        return jnp.where(i < nu[0], j, nf - 1)

    grid_spec = pltpu.PrefetchScalarGridSpec(
        num_scalar_prefetch=2,
        grid=(p_rows // tm, nf),
        in_specs=[pl.BlockSpec((tm, d), lambda i, j, be, nu: (i, 0)),
                  pl.BlockSpec((1, d, tf), lambda i, j, be, nu: (be[i], 0, jj(i, j, nu))),
                  pl.BlockSpec((1, d, tf), lambda i, j, be, nu: (be[i], 0, jj(i, j, nu))),
                  pl.BlockSpec((1, tf, d), lambda i, j, be, nu: (be[i], jj(i, j, nu), 0))],
        out_specs=pl.BlockSpec((tm, d), lambda i, j, be, nu: (i, 0)),
        scratch_shapes=[pltpu.VMEM((tm, d), bf16), pltpu.VMEM((tm, d), f32)],
    )
    return pl.pallas_call(
        _moe_ffn_kernel,
        out_shape=jax.ShapeDtypeStruct((p_rows, d), f32),
        grid_spec=grid_spec,
        compiler_params=_cparams("arbitrary", "arbitrary"),
        name="moe_ffn",
    )(blk_expert, n_used, xs, wg, wu, wd)


def _combine_kernel(pos_ref, y_ref, p_ref, x_ref, g_ref, o_ref, ybuf_ref, sem):
    tt = x_ref.shape[1]

    def row_copy(p, dst):
        return pltpu.make_async_copy(y_ref.at[pl.ds(p, 1), :], ybuf_ref.at[pl.ds(dst, 1), :], sem)

    def start(t, c):
        row_copy(pos_ref[0, 0, 2 * t], t).start()
        row_copy(pos_ref[0, 0, 2 * t + 1], tt + t).start()
        return c

    lax.fori_loop(0, tt, start, 0, unroll=8)
    pltpu.make_async_copy(y_ref.at[pl.ds(0, 2 * tt), :], ybuf_ref, sem).wait()
    p = p_ref[0]
    mix = p[:, 0:1] * ybuf_ref[pl.ds(0, tt), :] + p[:, 1:2] * ybuf_ref[pl.ds(tt, tt), :]
    o_ref[0] = x_ref[0] + g_ref[0] * mix


def moe_combine(y, pos, probs, x, gate):
    nb, lr, d = x.shape
    tt = _tile(lr, MOE_TT)
    nt = lr // tt
    pos3 = pos.reshape(nb * nt, 1, 2 * tt)
    probs3 = probs.reshape(nb, lr, LANES)
    return pl.pallas_call(
        _combine_kernel,
        out_shape=jax.ShapeDtypeStruct((nb, lr, d), f32),
        grid=(nb, nt),
        in_specs=[pl.BlockSpec((1, 1, 2 * tt), lambda b, i: (b * nt + i, 0, 0), memory_space=pltpu.SMEM),
                  pl.BlockSpec(memory_space=pl.ANY),
                  pl.BlockSpec((1, tt, LANES), lambda b, i: (b, i, 0)),
                  pl.BlockSpec((1, tt, d), lambda b, i: (b, i, 0)),
                  pl.BlockSpec((1, 1, d), lambda b, i: (b, 0, 0))],
        out_specs=pl.BlockSpec((1, tt, d), lambda b, i: (b, i, 0)),
        scratch_shapes=[pltpu.VMEM((2 * tt, d), f32), pltpu.SemaphoreType.DMA(())],
        compiler_params=_cparams("arbitrary", "arbitrary"),
        name="moe_combine",
    )(pos3, y, probs3, x, gate)


def moe_layer(h, x, gate, w_router, wg, wu, wd):
    nb, lr, d = h.shape
    n = nb * lr
    h2d = h.reshape(n, d)
    w_r = jnp.zeros((d, LANES), bf16).at[:, :N_EXPERTS].set(w_router.astype(bf16))
    e_out, p_out = moe_router(h2d, w_r)
    e_flat = e_out[:, :2].reshape(-1)
    onehot = (e_flat[:, None] == jnp.arange(N_EXPERTS, dtype=jnp.int32)[None, :]).astype(jnp.int32)
    csum = jnp.cumsum(onehot, axis=0)
    counts = csum[-1]
    rank = jnp.sum(onehot * csum, axis=1) - 1
    padded = (counts + MOE_TM - 1) // MOE_TM * MOE_TM
    pad_ends = jnp.cumsum(padded)
    pad_starts = pad_ends - padded
    pos = jnp.sum(onehot * pad_starts[None, :], axis=1) + rank
    n_blocks = (2 * n) // MOE_TM + N_EXPERTS
    blk_start = jnp.arange(n_blocks, dtype=jnp.int32) * MOE_TM
    blk_expert = jnp.minimum(jnp.sum((blk_start[:, None] >= pad_ends[None, :]).astype(jnp.int32), axis=1),
                             N_EXPERTS - 1).astype(jnp.int32)
    n_used = (pad_ends[-1] // MOE_TM).astype(jnp.int32).reshape(1)
    xs = moe_dispatch(h2d, pos.astype(jnp.int32), n_blocks * MOE_TM)
    y = moe_ffn(xs, blk_expert, n_used, wg, wu, wd)
    return moe_combine(y, pos.astype(jnp.int32), p_out, x, gate)


def _rope_tables(n_tok):
    rows = n_tok // GRID_W
    r = jnp.repeat(jnp.arange(rows, dtype=f32), GRID_W)
    cl = jnp.tile(jnp.arange(GRID_W, dtype=f32), rows)
    n_freq = MLA_ROPE // 4
    inv = ROPE_BASE ** (-jnp.arange(n_freq, dtype=f32) / n_freq)
    ang = jnp.concatenate([r[:, None] * inv, cl[:, None] * inv], axis=-1)
    cos, sin = jnp.cos(ang), jnp.sin(ang)
    return jnp.tile(cos, (1, 4)), jnp.concatenate([-sin, sin, -sin, sin], axis=-1)


def _prep_in_weights(w_in_l):
    d = w_in_l.shape[0]
    nq = MLA_HEADS * (MLA_NOPE + MLA_ROPE)
    wq = w_in_l[:, :nq].reshape(d, MLA_HEADS, MLA_NOPE + MLA_ROPE)
    w_ckv = w_in_l[:, nq:nq + KV_LORA]
    w_kr = w_in_l[:, nq + KV_LORA:nq + KV_LORA + MLA_ROPE]
    rest = w_in_l[:, nq + KV_LORA + MLA_ROPE:]
    w_main = jnp.concatenate([wq[:, :, :MLA_NOPE].reshape(d, Q_NOPE_W), wq[:, :, MLA_NOPE:].reshape(d, Q_ROPE_W),
                              rest], axis=1).astype(bf16)
    w_ckr = jnp.concatenate([w_ckv, w_kr, w_kr], axis=1).astype(bf16)
    return w_main, w_ckr


def kernel(x_prompt, x_sample, c, cache_ckv, cache_krope, state_ret, c_ctx, w_ada, b_ada, norm1_g, w_in, kv_norm_g, w_ukv, ret_decay_logit, ret_gn_g, w_o_mla, w_o_fnet, w_o_ret, w_out, norm2_g, w_gate_dense, w_up_dense, w_down_dense, w_router, w_gate_moe, w_up_moe, w_down_moe, final_norm_g):
    nbc, seq, d = x_prompt.shape
    nbs, ls, _ = x_sample.shape
    depth = w_in.shape[0]
    lr = nbc * seq
    assert lr == ls, "unified row layout needs BATCH * SEQ == DEC_SEQ"
    nb = 1 + nbs
    n_rows = nb * lr

    x = jnp.concatenate([x_prompt.reshape(1, lr, d), x_sample], axis=0)
    cond = jnp.concatenate([c_ctx[None, :], c, jnp.zeros((16 - nb, d), f32)], axis=0)
    mod = adaln_all(cond, w_ada, b_ada)[:, :nb]
    cos128, sin128 = _rope_tables(ls)

    w_ch = jnp.concatenate(_dft_mats(FNET_GC), axis=1).astype(bf16)
    c_c, s_c = _dft_mats(seq)
    cs_ctx = jnp.concatenate([c_c, -s_c], axis=1).astype(bf16)
    c_s, s_s = _dft_mats(ls)
    lh = ls // 2
    c_half = c_s[:, :lh] * jnp.where(jnp.arange(lh) == 0, 0.5, 1.0)[None, :]
    cs_lat = jnp.concatenate([c_half, -s_s[:, :lh]], axis=1).astype(bf16)

    ckv_list, krope_list, ret_list = [], [], []
    spare = []
    for l in range(depth):
        sh1, sc1, g1, sh2, sc2, g2 = [m.reshape(nb, 1, d) for m in jnp.split(mod[l], 6, axis=-1)]
        w_main, w_ckr = _prep_in_weights(w_in[l])
        h = norm_mod(x, norm1_g[l], sc1, sh1, bf16)
        main = inproj_main(h, w_main, cos128, sin128)
        ckv, kr, krr = inproj_ckr(h, w_ckr, kv_norm_g[l], cos128, sin128)
        w_ukv_b = w_ukv[l].astype(bf16)

        ckv_c = ckv[0].reshape(nbc, seq, KV_LORA)
        kr_c = kr[0].reshape(nbc, seq, LANES)
        kv_c = matmul(ckv_c.reshape(lr, KV_LORA).astype(bf16), w_ukv_b, bf16).reshape(nbc, seq, -1)
        main_c = main.reshape(nb * nbc, seq, N_MAIN)
        bufs = [h] + spare
        bufs += [jnp.zeros((nb, lr, d), bf16) for _ in range(3 - len(bufs))]
        bufs = [bb.reshape(nb * nbc, seq, d) for bb in bufs[:3]]
        att_c = attention(main_c, kv_c, kr_c.astype(bf16), nbc, seq, 0, nb * nbc, bufs[0])
        ab_c = dft_channels(main_c, w_ch, nbc, seq, 0).reshape(nbc, 2 * seq, FNET_GROUPS * FNET_GC)
        four_c = dft_sequence(cs_ctx, ab_c, 0, nb * nbc, bufs[1])
        ret_c, sfin_c = retention(main_c, ret_decay_logit[l], ret_gn_g[l], None, nbc, seq, 0, nb * nbc, bufs[2])
        ckv_list.append(ckv_c)
        krope_list.append(kr_c[:, :, :MLA_ROPE])
        ret_list.append(sfin_c)

        ckv_keys = jnp.concatenate([ckv[1:], cache_ckv[:, l]], axis=1).astype(bf16)
        lk = ckv_keys.shape[1]
        kv_s = matmul(ckv_keys.reshape(nbs * lk, KV_LORA), w_ukv_b, bf16).reshape(nbs, lk, -1)
        ck = cache_krope[:, l].astype(bf16)
        kr_keys = jnp.concatenate([krr[1:], jnp.concatenate([ck, ck], axis=-1)], axis=1)
        att = attention(main, kv_s, kr_keys, nbs, ls, 1, nb, att_c.reshape(nb, ls, -1)).reshape(n_rows, -1)
        f_lat = main[1:, :, OFF_F:OFF_F + FNET_GROUPS * FNET_GC]
        f_mirror = jnp.concatenate([f_lat[:, :1], jnp.flip(f_lat[:, lh + 1:], axis=1)], axis=1)
        ab_s = dft_channels_sym(main, f_mirror, w_ch, nbs, ls, 1).reshape(nbs, ls, FNET_GROUPS * FNET_GC)
        a_mid = matmul(f_lat[:, lh:lh + 8].reshape(nbs * 8 * FNET_GROUPS, FNET_GC), w_ch[:, :FNET_GC], f32)
        a_mid = a_mid.reshape(nbs, 8, FNET_GROUPS * FNET_GC)
        four = dft_sequence_sym(cs_lat, ab_s, a_mid, 1, nb, four_c.reshape(nb, ls, -1)).reshape(n_rows, -1)
        ret, _ = retention(main, ret_decay_logit[l], ret_gn_g[l], state_ret[:, l], nbs, ls, 1, nb,
                           ret_c.reshape(nb, ls, -1))
        ret = ret.reshape(n_rows, -1)
        spare = [att, four]

        merged = merge_branches(att, four, ret, w_o_mla[l].astype(bf16), w_o_fnet[l].astype(bf16),
                                w_o_ret[l].astype(bf16), main.reshape(n_rows, N_MAIN))
        x = proj_residual(merged.reshape(nb, lr, d), w_out[l].astype(bf16), x, g1)

        i = l // 2
        if l % 2 == 0:
            h2 = norm_mod(x, norm2_g[l], sc2, sh2, bf16)
            x = ffn_dense(h2, w_gate_dense[i].astype(bf16), w_up_dense[i].astype(bf16),
                          w_down_dense[i].astype(bf16), x, g2)
        else:
            h2 = norm_mod(x, norm2_g[l], sc2, sh2, f32)
            x = moe_layer(h2, x, g2, w_router[i], w_gate_moe[i].astype(bf16), w_up_moe[i].astype(bf16),
                          w_down_moe[i].astype(bf16))

    y_prompt = final_norm(x, final_norm_g, 0, 1).reshape(nbc, seq, d)
    y_sample = final_norm(x, final_norm_g, 1, nbs)
    new_ckv = jnp.stack(ckv_list, axis=1)
    new_krope = jnp.stack(krope_list, axis=1)
    new_ret = jnp.stack(ret_list, axis=1)
    return (y_prompt, y_sample, new_ckv, new_krope, new_ret)
```

```python
import functools
import math

import jax
import jax.numpy as jnp
from jax import lax
from jax.experimental import pallas as pl
from jax.experimental.pallas import tpu as pltpu

f32 = jnp.float32
bf16 = jnp.bfloat16

D_MODEL = 2048
GRID_W = 64
MLA_HEADS = 16
MLA_NOPE = 128
MLA_ROPE = 64
MLA_V = 128
KV_LORA = 512
ROPE_BASE = 10000.0
FNET_GROUPS = 4
FNET_GC = 512
RET_HEADS = 8
RET_DK = 128
RET_DV = 256
RET_CHUNK = 256
N_EXPERTS = 8
EPS = 1e-6
GN_EPS = 1e-5

LANES = 128
VMEM_LIMIT_BYTES = 56 * 1024 * 1024

Q_NOPE_W = MLA_HEADS * MLA_NOPE
Q_ROPE_W = MLA_HEADS * MLA_ROPE
OFF_Q = 0
OFF_QROPE = Q_NOPE_W
OFF_F = OFF_Q + Q_NOPE_W + Q_ROPE_W
OFF_RQ = OFF_F + FNET_GROUPS * FNET_GC
OFF_RK = OFF_RQ + RET_HEADS * RET_DK
OFF_RV = OFF_RK + RET_HEADS * RET_DK
OFF_RG = OFF_RV + RET_HEADS * RET_DV
OFF_GATES = OFF_RG + RET_HEADS * RET_DV
N_MAIN = OFF_GATES + 3 * D_MODEL
CKR_W = KV_LORA + LANES
Q_SCALE = (MLA_NOPE + MLA_ROPE) ** -0.5 * math.log2(math.e)


def _cparams(*sem):
    return pltpu.CompilerParams(dimension_semantics=sem, vmem_limit_bytes=VMEM_LIMIT_BYTES)


def _tile(n, pref):
    t = min(n, pref)
    while n % t:
        t -= 8
    return t


def _silu(x):
    return x * jax.nn.sigmoid(x)


def _rope128(x, cos, sin):
    lane = lax.broadcasted_iota(jnp.int32, x.shape, 1)
    first = (lane % MLA_ROPE) < (MLA_ROPE // 2)
    swapped = jnp.where(first, pltpu.roll(x, LANES - MLA_ROPE // 2, 1), pltpu.roll(x, MLA_ROPE // 2, 1))
    return x * cos + swapped * sin


def _adaln_kernel(c_ref, w_ref, b_ref, o_ref):
    s = _silu(c_ref[...]).astype(bf16)
    o_ref[0] = jnp.dot(s, w_ref[0].astype(bf16), preferred_element_type=f32) + b_ref[0]


def adaln_all(cond, w_ada, b_ada):
    depth, d, n = w_ada.shape
    r = cond.shape[0]
    tn = _tile(n, 1024)
    return pl.pallas_call(
        _adaln_kernel,
        out_shape=jax.ShapeDtypeStruct((depth, r, n), f32),
        grid=(depth, n // tn),
        in_specs=[pl.BlockSpec((r, d), lambda l, j: (0, 0)),
                  pl.BlockSpec((1, d, tn), lambda l, j: (l, 0, j)),
                  pl.BlockSpec((1, 1, tn), lambda l, j: (l, 0, j))],
        out_specs=pl.BlockSpec((1, r, tn), lambda l, j: (l, 0, j)),
        compiler_params=_cparams("arbitrary", "arbitrary"),
        name="adaln",
    )(cond, w_ada, b_ada.reshape(depth, 1, n))


def _norm_mod_kernel(x_ref, g_ref, sc_ref, sh_ref, o_ref):
    x = x_ref[0]
    y = x * lax.rsqrt(jnp.mean(x * x, axis=-1, keepdims=True) + EPS)
    y = y * g_ref[...]
    o_ref[0] = (y * (1.0 + sc_ref[0]) + sh_ref[0]).astype(o_ref.dtype)


def norm_mod(x, g, sc, sh, out_dtype):
    nb, lr, d = x.shape
    tm = _tile(lr, 512)
    return pl.pallas_call(
        _norm_mod_kernel,
        out_shape=jax.ShapeDtypeStruct((nb, lr, d), out_dtype),
        grid=(nb, lr // tm),
        in_specs=[pl.BlockSpec((1, tm, d), lambda b, i: (b, i, 0)),
                  pl.BlockSpec((1, d), lambda b, i: (0, 0)),
                  pl.BlockSpec((1, 1, d), lambda b, i: (b, 0, 0)),
                  pl.BlockSpec((1, 1, d), lambda b, i: (b, 0, 0))],
        out_specs=pl.BlockSpec((1, tm, d), lambda b, i: (b, i, 0)),
        compiler_params=_cparams("arbitrary", "arbitrary"),
        name="norm_mod",
    )(x, g.reshape(1, d), sc, sh)


def _final_norm_kernel(x_ref, g_ref, o_ref):
    x = x_ref[0]
    y = x * lax.rsqrt(jnp.mean(x * x, axis=-1, keepdims=True) + EPS)
    o_ref[0] = y * g_ref[...]


def final_norm(x, g, b_off, nb_out):
    _, lr, d = x.shape
    tm = _tile(lr, 512)
    return pl.pallas_call(
        _final_norm_kernel,
        out_shape=jax.ShapeDtypeStruct((nb_out, lr, d), f32),
        grid=(nb_out, lr // tm),
        in_specs=[pl.BlockSpec((1, tm, d), lambda b, i: (b + b_off, i, 0)),
                  pl.BlockSpec((1, d), lambda b, i: (0, 0))],
        out_specs=pl.BlockSpec((1, tm, d), lambda b, i: (b, i, 0)),
        compiler_params=_cparams("arbitrary", "arbitrary"),
        name="final_norm",
    )(x, g.reshape(1, d))


def _inproj_kernel(h_ref, w_ref, cos_ref, sin_ref, o_ref, *, tn, nchunk):
    b = pl.program_id(0)
    j = pl.program_id(2)
    tm = h_ref.shape[1]
    rc = tm // nchunk
    j_rope = OFF_QROPE // tn
    j_rk = OFF_RK // tn
    j_rv = OFF_RV // tn
    j_rg = OFF_RG // tn
    j_gates = OFF_GATES // tn
    is_q = j < OFF_F // tn
    is_rope = jnp.logical_and(jnp.logical_and(j >= j_rope, is_q), b >= 1)
    is_qplain = jnp.logical_and(is_q, jnp.logical_not(is_rope))
    is_rk = jnp.logical_and(j >= j_rk, j < j_rv)
    is_silu = jnp.logical_and(j >= j_rg, j < j_gates)
    is_sig = j >= j_gates
    plain = jnp.logical_not(is_q | is_rk | is_silu | is_sig)

    def run(epilogue):
        pending = None
        for r in range(nchunk):
            rows = slice(r * rc, (r + 1) * rc)
            acc = jnp.dot(h_ref[0, rows, :], w_ref[...], preferred_element_type=f32)
            if pending is not None:
                prow, pacc = pending
                o_ref[0, prow, :] = epilogue(pacc, prow).astype(o_ref.dtype)
            pending = (rows, acc)
        prow, pacc = pending
        o_ref[0, prow, :] = epilogue(pacc, prow).astype(o_ref.dtype)

    @pl.when(plain)
    def _():
        run(lambda acc, rows: acc)

    @pl.when(is_qplain)
    def _():
        run(lambda acc, rows: acc * Q_SCALE)

    @pl.when(is_rope)
    def _():
        def rope(acc, rows):
            cos = cos_ref[rows, :] * Q_SCALE
            sin = sin_ref[rows, :] * Q_SCALE
            return jnp.concatenate([_rope128(acc[:, s * LANES:(s + 1) * LANES], cos, sin)
                                    for s in range(tn // LANES)], axis=1)
        run(rope)

    @pl.when(is_rk)
    def _():
        run(lambda acc, rows: acc * (RET_DK ** -0.5))

    @pl.when(is_silu)
    def _():
        run(lambda acc, rows: _silu(acc))

    @pl.when(is_sig)
    def _():
        run(lambda acc, rows: jax.nn.sigmoid(acc))


def inproj_main(h, w_main, cos128, sin128):
    nb, lr, d = h.shape
    tm = _tile(lr, 1024)
    tn = 1024
    return pl.pallas_call(
        functools.partial(_inproj_kernel, tn=tn, nchunk=4),
        out_shape=jax.ShapeDtypeStruct((nb, lr, N_MAIN), bf16),
        grid=(nb, lr // tm, N_MAIN // tn),
        in_specs=[pl.BlockSpec((1, tm, d), lambda b, i, j: (b, i, 0)),
                  pl.BlockSpec((d, tn), lambda b, i, j: (0, j)),
                  pl.BlockSpec((tm, LANES), lambda b, i, j: (i, 0)),
                  pl.BlockSpec((tm, LANES), lambda b, i, j: (i, 0))],
        out_specs=pl.BlockSpec((1, tm, tn), lambda b, i, j: (b, i, j)),
        compiler_params=_cparams("arbitrary", "arbitrary", "arbitrary"),
        name="inproj_main",
    )(h, w_main, cos128, sin128)


def _ckr_kernel(h_ref, w_ref, g_ref, cos_ref, sin_ref, ckv_ref, kr_ref, krr_ref):
    acc = jnp.dot(h_ref[0], w_ref[...], preferred_element_type=f32)
    c = acc[:, :KV_LORA]
    y = c * lax.rsqrt(jnp.mean(c * c, axis=-1, keepdims=True) + EPS)
    ckv_ref[0] = y * g_ref[...]
    kr = acc[:, KV_LORA:]
    kr_ref[0] = kr
    krr_ref[0] = _rope128(kr, cos_ref[...], sin_ref[...]).astype(krr_ref.dtype)


def inproj_ckr(h, w_ckr, kv_g, cos128, sin128):
    nb, lr, d = h.shape
    tm = _tile(lr, 1024)
    return pl.pallas_call(
        _ckr_kernel,
        out_shape=(jax.ShapeDtypeStruct((nb, lr, KV_LORA), f32),
                   jax.ShapeDtypeStruct((nb, lr, LANES), f32),
                   jax.ShapeDtypeStruct((nb, lr, LANES), bf16)),
        grid=(nb, lr // tm),
        in_specs=[pl.BlockSpec((1, tm, d), lambda b, i: (b, i, 0)),
                  pl.BlockSpec((d, CKR_W), lambda b, i: (0, 0)),
                  pl.BlockSpec((1, KV_LORA), lambda b, i: (0, 0)),
                  pl.BlockSpec((tm, LANES), lambda b, i: (i, 0)),
                  pl.BlockSpec((tm, LANES), lambda b, i: (i, 0))],
        out_specs=(pl.BlockSpec((1, tm, KV_LORA), lambda b, i: (b, i, 0)),
                   pl.BlockSpec((1, tm, LANES), lambda b, i: (b, i, 0)),
                   pl.BlockSpec((1, tm, LANES), lambda b, i: (b, i, 0))),
        compiler_params=_cparams("arbitrary", "arbitrary"),
        name="inproj_ckr",
    )(h, w_ckr, kv_g.reshape(1, KV_LORA), cos128, sin128)


def _mm_kernel(x_ref, w_ref, o_ref):
    o_ref[...] = jnp.dot(x_ref[...], w_ref[...], preferred_element_type=f32).astype(o_ref.dtype)


def matmul(x, w, out_dtype, tm_pref=512, tn_pref=4096):
    m, k = x.shape
    _, n = w.shape
    tm = _tile(m, tm_pref)
    tn = _tile(n, tn_pref)
    return pl.pallas_call(
        _mm_kernel,
        out_shape=jax.ShapeDtypeStruct((m, n), out_dtype),
        grid=(m // tm, n // tn),
        in_specs=[pl.BlockSpec((tm, k), lambda i, j: (i, 0)),
                  pl.BlockSpec((k, tn), lambda i, j: (0, j))],
        out_specs=pl.BlockSpec((tm, tn), lambda i, j: (i, j)),
        compiler_params=_cparams("arbitrary", "arbitrary"),
        name="matmul",
    )(x, w)


def _attn_single_kernel(qn_ref, qr_ref, kn_ref, v_ref, kr_ref, o_ref, kcat_ref, vt_ref, s_ref, *, nk, tk, nparts):
    h = pl.program_id(1)
    qi = pl.program_id(2)
    lk = kcat_ref.shape[0]

    @pl.when(qi == 0)
    def _():
        kcat_ref[:, :MLA_NOPE] = kn_ref[0]
        kcat_ref[:, MLA_NOPE:] = kr_ref[0]
        for c in range(nk):
            vt_ref[:, c * tk:(c + 1) * tk] = v_ref[0, c * tk:(c + 1) * tk, :].astype(f32).T.astype(bf16)

    qr = qr_ref[0]
    lane = lax.broadcasted_iota(jnp.int32, qr.shape, 1)
    qr = jnp.where((lane // MLA_ROPE) == (h % 2), qr, jnp.zeros_like(qr))
    qcat = jnp.concatenate([qn_ref[0], qr], axis=1)
    tq = qcat.shape[0]

    bounds = [((nk * i) // nparts) * tk for i in range(nparts + 1)]
    m = jnp.full((1, tq), -jnp.inf, f32)
    pending = None
    for i in range(nparts):
        r0, r1 = bounds[i], bounds[i + 1]
        s = lax.dot_general(kcat_ref[r0:r1, :], qcat, (((1,), (1,)), ((), ())), preferred_element_type=f32)
        s_ref[r0:r1, :] = s
        if pending is not None:
            m = jnp.maximum(m, jnp.max(pending, axis=0, keepdims=True))
        pending = s
    m = jnp.maximum(m, jnp.max(pending, axis=0, keepdims=True))

    l = jnp.zeros((1, tq), f32)
    acc = jnp.zeros((MLA_V, tq), f32)
    for c in range(nk):
        p = jnp.exp2(s_ref[c * tk:(c + 1) * tk, :] - m)
        l = l + jnp.sum(p, axis=0, keepdims=True)
        acc = acc + jnp.dot(vt_ref[:, c * tk:(c + 1) * tk], p.astype(bf16), preferred_element_type=f32)
    o_ref[0] = (acc * (1.0 / l)).T.astype(o_ref.dtype)


def _attn_kernel(qn_ref, qr_ref, kn_ref, v_ref, kr_ref, o_ref, kcat_ref, vt_ref, sa_ref, sb_ref, ma_ref, mb_ref,
                 *, nk, tk, nparts):
    h = pl.program_id(1)
    i = pl.program_id(2)

    @pl.when(i == 0)
    def _():
        kcat_ref[:, :MLA_NOPE] = kn_ref[0]
        kcat_ref[:, MLA_NOPE:] = kr_ref[0]
        for c in range(nk):
            vt_ref[:, c * tk:(c + 1) * tk] = v_ref[0, c * tk:(c + 1) * tk, :].astype(f32).T.astype(bf16)
        sb_ref[...] = jnp.zeros(sb_ref.shape, f32)
        mb_ref[...] = jnp.zeros(mb_ref.shape, f32)

    def step(s_new, s_old, m_new_ref, m_old_ref):
        qr = qr_ref[0]
        lane = lax.broadcasted_iota(jnp.int32, qr.shape, 1)
        qr = jnp.where((lane // MLA_ROPE) == (h % 2), qr, jnp.zeros_like(qr))
        qcat = jnp.concatenate([qn_ref[0], qr], axis=1)
        tq = qcat.shape[0]
        m_old = m_old_ref[0:1, :]
        bounds = [(nk * g) // nparts for g in range(nparts + 1)]
        m = jnp.full((1, tq), -jnp.inf, f32)
        l = jnp.zeros((1, tq), f32)
        acc = jnp.zeros((MLA_V, tq), f32)
        for g in range(nparts):
            r0, r1 = bounds[g] * tk, bounds[g + 1] * tk
            s = lax.dot_general(kcat_ref[r0:r1, :], qcat, (((1,), (1,)), ((), ())), preferred_element_type=f32)
            s_new[r0:r1, :] = s
            m = jnp.maximum(m, jnp.max(s, axis=0, keepdims=True))
            for c in range(bounds[g], bounds[g + 1]):
                p = jnp.exp2(s_old[c * tk:(c + 1) * tk, :] - m_old)
                l = l + jnp.sum(p, axis=0, keepdims=True)
                acc = acc + jnp.dot(vt_ref[:, c * tk:(c + 1) * tk], p.astype(bf16), preferred_element_type=f32)
        m_new_ref[...] = jnp.broadcast_to(m, m_new_ref.shape)
        o_ref[0] = (acc * (1.0 / l)).T.astype(o_ref.dtype)

    @pl.when(i % 2 == 0)
    def _():
        step(sa_ref, sb_ref, ma_ref, mb_ref)

    @pl.when(i % 2 == 1)
    def _():
        step(sb_ref, sa_ref, mb_ref, ma_ref)


def _with_prev(kern, in_specs, args, prev):
    if prev is None:
        return kern, in_specs, args, {}
    n_in = len(args)

    def wrapped(*refs):
        return kern(*refs[:n_in], *refs[n_in + 1:])

    return wrapped, in_specs + [pl.BlockSpec(memory_space=pl.ANY)], args + [prev], {n_in: 0}


def attention(main, kv, krope, nb, lq, b_off, nb_total, prev):
    lk = kv.shape[1]
    tq = _tile(lq, 256)
    tk = _tile(lk, 256)
    nq = lq // tq
    qr_blk = OFF_QROPE // LANES
    nk = lk // tk
    nparts = min(4, nk)
    skewed = nq > 1
    if skewed:
        steps = nq + 1
        q_idx = lambda i: jnp.minimum(i, nq - 1)
        o_idx = lambda i: jnp.maximum(i - 1, 0)
        kern = functools.partial(_attn_kernel, nk=nk, tk=tk, nparts=nparts)
        scratch = [pltpu.VMEM((lk, MLA_NOPE + LANES), bf16), pltpu.VMEM((MLA_V, lk), bf16),
                   pltpu.VMEM((lk, tq), f32), pltpu.VMEM((lk, tq), f32),
                   pltpu.VMEM((8, tq), f32), pltpu.VMEM((8, tq), f32)]
    else:
        steps = nq
        q_idx = lambda i: i
        o_idx = lambda i: i
        kern = functools.partial(_attn_single_kernel, nk=nk, tk=tk, nparts=nparts)
        scratch = [pltpu.VMEM((lk, MLA_NOPE + LANES), bf16), pltpu.VMEM((MLA_V, lk), bf16),
                   pltpu.VMEM((lk, tq), f32)]
    in_specs = [pl.BlockSpec((1, tq, MLA_NOPE), lambda b, h, i: (b + b_off, q_idx(i), h)),
                pl.BlockSpec((1, tq, LANES), lambda b, h, i: (b + b_off, q_idx(i), qr_blk + h // 2)),
                pl.BlockSpec((1, lk, MLA_NOPE), lambda b, h, i: (b, 0, 2 * h)),
                pl.BlockSpec((1, lk, MLA_V), lambda b, h, i: (b, 0, 2 * h + 1)),
                pl.BlockSpec((1, lk, LANES), lambda b, h, i: (b, 0, 0))]
    kern, in_specs, args, aliases = _with_prev(kern, in_specs, [main, main, kv, kv, krope], prev)
    return pl.pallas_call(
        kern,
        out_shape=jax.ShapeDtypeStruct((nb_total, lq, MLA_HEADS * MLA_V), bf16),
        grid=(nb, MLA_HEADS, steps),
        in_specs=in_specs,
        out_specs=pl.BlockSpec((1, tq, MLA_V), lambda b, h, i: (b + b_off, o_idx(i), h)),
        scratch_shapes=scratch,
        input_output_aliases=aliases,
        compiler_params=_cparams("arbitrary", "arbitrary", "arbitrary"),
        name="attention",
    )(*args)


def _dft_ch_kernel(xa_ref, xb_ref, w_ref, o_ref):
    half = FNET_GROUPS // 2
    for g in range(FNET_GROUPS):
        x_ref = xa_ref if g < half else xb_ref
        lo = (g % half) * FNET_GC
        y = jnp.dot(x_ref[0, :, lo:lo + FNET_GC], w_ref[...], preferred_element_type=f32)
        o_ref[0, 0, :, g * FNET_GC:(g + 1) * FNET_GC] = y[:, :FNET_GC].astype(o_ref.dtype)
        o_ref[0, 1, :, g * FNET_GC:(g + 1) * FNET_GC] = y[:, FNET_GC:].astype(o_ref.dtype)


def dft_channels(main, w_ch, nb, ls, b_off):
    tm = _tile(ls, 512)
    wb = FNET_GROUPS * FNET_GC // 2
    f_blk = OFF_F // wb
    return pl.pallas_call(
        _dft_ch_kernel,
        out_shape=jax.ShapeDtypeStruct((nb, 2, ls, FNET_GROUPS * FNET_GC), bf16),
        grid=(nb, ls // tm),
        in_specs=[pl.BlockSpec((1, tm, wb), lambda b, i: (b + b_off, i, f_blk)),
                  pl.BlockSpec((1, tm, wb), lambda b, i: (b + b_off, i, f_blk + 1)),
                  pl.BlockSpec((FNET_GC, 2 * FNET_GC), lambda b, i: (0, 0))],
        out_specs=pl.BlockSpec((1, 2, tm, FNET_GROUPS * FNET_GC), lambda b, i: (b, 0, i, 0)),
        compiler_params=_cparams("arbitrary", "arbitrary"),
        name="dft_channels",
    )(main, main, w_ch)


def _dft_seq_kernel(l_ref, r_ref, o_ref):
    o_ref[0] = jnp.dot(l_ref[...], r_ref[0], preferred_element_type=f32).astype(o_ref.dtype)


def dft_sequence(cs, ab, b_off, nb_total, prev):
    nb, k, w = ab.shape
    ls = cs.shape[0]
    tm = _tile(ls, 512)
    tn = _tile(w, 512)
    in_specs = [pl.BlockSpec((tm, k), lambda b, j, i: (i, 0)),
                pl.BlockSpec((1, k, tn), lambda b, j, i: (b, 0, j))]
    kern, in_specs, args, aliases = _with_prev(_dft_seq_kernel, in_specs, [cs, ab], prev)
    return pl.pallas_call(
        kern,
        out_shape=jax.ShapeDtypeStruct((nb_total, ls, w), bf16),
        grid=(nb, w // tn, ls // tm),
        in_specs=in_specs,
        out_specs=pl.BlockSpec((1, tm, tn), lambda b, j, i: (b + b_off, i, j)),
        input_output_aliases=aliases,
        compiler_params=_cparams("arbitrary", "arbitrary", "arbitrary"),
        name="dft_sequence",
    )(*args)


def _dft_ch_sym_kernel(xa_ref, xb_ref, ma_ref, mb_ref, ea_ref, eb_ref, j_ref, w_ref, o_ref):
    half = FNET_GROUPS // 2
    tm = xa_ref.shape[1]
    first = lax.broadcasted_iota(jnp.int32, (tm, FNET_GC), 0) == 0
    for g in range(FNET_GROUPS):
        x_ref, m_ref, e_ref = (xa_ref, ma_ref, ea_ref) if g < half else (xb_ref, mb_ref, eb_ref)
        lo = (g % half) * FNET_GC
        x = x_ref[0, :, lo:lo + FNET_GC].astype(f32)
        r = jnp.dot(j_ref[...], m_ref[0, :, lo:lo + FNET_GC], preferred_element_type=f32)
        r = jnp.where(first, e_ref[0, 0:1, lo:lo + FNET_GC].astype(f32), r)
        cols = slice(g * FNET_GC, (g + 1) * FNET_GC)
        o_ref[0, 0, :, cols] = jnp.dot((x + r).astype(bf16), w_ref[:, :FNET_GC],
                                       preferred_element_type=f32).astype(o_ref.dtype)
        o_ref[0, 1, :, cols] = jnp.dot((x - r).astype(bf16), w_ref[:, FNET_GC:],
                                       preferred_element_type=f32).astype(o_ref.dtype)


def dft_channels_sym(main, w_ch, nb, ls, b_off):
    lh = ls // 2
    tm = _tile(lh, 512)
    nt = ls // tm
    wb = FNET_GROUPS * FNET_GC // 2
    f_blk = OFF_F // wb
    jj = jnp.arange(tm, dtype=jnp.int32)
    jmat = ((jj[:, None] + jj[None, :]) == tm).astype(bf16)
    eb = tm // 8
    x_spec = lambda c: pl.BlockSpec((1, tm, wb), lambda b, i: (b + b_off, i, f_blk + c))
    m_spec = lambda c: pl.BlockSpec((1, tm, wb), lambda b, i: (b + b_off, nt - 1 - i, f_blk + c))
    e_spec = lambda c: pl.BlockSpec((1, 8, wb), lambda b, i: (b + b_off, ((nt - i) % nt) * eb, f_blk + c))
    return pl.pallas_call(
        _dft_ch_sym_kernel,
        out_shape=jax.ShapeDtypeStruct((nb, 2, lh, FNET_GROUPS * FNET_GC), bf16),
        grid=(nb, lh // tm),
        in_specs=[x_spec(0), x_spec(1), m_spec(0), m_spec(1), e_spec(0), e_spec(1),
                  pl.BlockSpec((tm, tm), lambda b, i: (0, 0)),
                  pl.BlockSpec((FNET_GC, 2 * FNET_GC), lambda b, i: (0, 0))],
        out_specs=pl.BlockSpec((1, 2, tm, FNET_GROUPS * FNET_GC), lambda b, i: (b, 0, i, 0)),
        compiler_params=_cparams("arbitrary", "arbitrary"),
        name="dft_channels_sym",
    )(main, main, main, main, main, main, jmat, w_ch)


def _dft_seq_sym_kernel(l_ref, r_ref, a_ref, o_ref, *, mid_scale):
    y = jnp.dot(l_ref[...], r_ref[0], preferred_element_type=f32)
    rows = lax.broadcasted_iota(jnp.int32, y.shape, 0)
    sign = jnp.where(rows % 2 == 0, mid_scale, -mid_scale)
    o_ref[0] = (y + sign * a_ref[0, 0:1, :]).astype(o_ref.dtype)


def dft_sequence_sym(cs, ab, a_mid, b_off, nb_total, prev):
    nb, k, w = ab.shape
    ls = cs.shape[0]
    tm = _tile(ls, 512)
    tn = _tile(w, 512)
    in_specs = [pl.BlockSpec((tm, k), lambda b, j, i: (i, 0)),
                pl.BlockSpec((1, k, tn), lambda b, j, i: (b, 0, j)),
                pl.BlockSpec((1, 8, tn), lambda b, j, i: (b, 0, j))]
    kern, in_specs, args, aliases = _with_prev(functools.partial(_dft_seq_sym_kernel, mid_scale=ls ** -0.5),
                                               in_specs, [cs, ab, a_mid], prev)
    return pl.pallas_call(
        kern,
        out_shape=jax.ShapeDtypeStruct((nb_total, ls, w), bf16),
        grid=(nb, w // tn, ls // tm),
        in_specs=in_specs,
        out_specs=pl.BlockSpec((1, tm, tn), lambda b, j, i: (b + b_off, i, j)),
        input_output_aliases=aliases,
        compiler_params=_cparams("arbitrary", "arbitrary", "arbitrary"),
        name="dft_sequence_sym",
    )(*args)


def _dft_mats(n):
    idx = jnp.arange(n, dtype=jnp.int32)
    ang = ((idx[:, None] * idx[None, :]) % n).astype(f32) * (2.0 * math.pi / n)
    s = n ** -0.5
    return jnp.cos(ang) * s, jnp.sin(ang) * s


def _log_sigmoid(x):
    return jnp.minimum(x, 0.0) - jnp.log1p(jnp.exp(-jnp.abs(x)))


def _ret_kernel(*refs, nc, use_s0):
    if use_s0:
        dl_ref, q_ref, k_ref, v_ref, g_ref, gn_ref, s0_ref, o_ref, sfin_ref, of_ref, ob_ref = refs
    else:
        dl_ref, q_ref, k_ref, v_ref, g_ref, gn_ref, o_ref, sfin_ref, of_ref, ob_ref = refs
        s0_ref = None
    C = RET_CHUNK
    lgf = _log_sigmoid(dl_ref[0, 0])[0:1, :]
    lgb = _log_sigmoid(dl_ref[1, 0])[0:1, :]
    lgf2 = jnp.concatenate([lgf, lgf], axis=1)
    lgb2 = jnp.concatenate([lgb, lgb], axis=1)
    lgf_c = jnp.concatenate([lgf] * (C // LANES), axis=1)
    lgb_c = jnp.concatenate([lgb] * (C // LANES), axis=1)
    rows = lax.broadcasted_iota(jnp.int32, (C, C), 0)
    cols = lax.broadcasted_iota(jnp.int32, (C, C), 1)
    rel = (rows - cols).astype(f32)
    d_f = jnp.where(rel >= 0, jnp.exp(jnp.maximum(rel, 0.0) * lgf_c), 0.0)
    d_b = jnp.where(rel <= 0, jnp.exp(jnp.maximum(-rel, 0.0) * lgb_c), 0.0)
    r_v = lax.broadcasted_iota(jnp.int32, (C, RET_DV), 0).astype(f32)
    r_k = lax.broadcasted_iota(jnp.int32, (C, RET_DK), 0).astype(f32)
    qdec_f = jnp.exp((r_v + 1.0) * lgf2)
    qdec_b = jnp.exp((C - r_v) * lgb2)
    kdec_f = jnp.exp((C - 1.0 - r_k) * lgf)
    kdec_b = jnp.exp(r_k * lgb)
    cdec_f = jnp.exp(C * lgf2)
    cdec_b = jnp.exp(C * lgb2)

    def chunk(c, s, dmat, qdec, kdec, cdec):
        r = pl.multiple_of(c * C, C)
        q = q_ref[0, pl.ds(r, C), :]
        k = k_ref[0, pl.ds(r, C), :]
        v = v_ref[0, pl.ds(r, C), :]
        inner = lax.dot_general(q, k, (((1,), (1,)), ((), ())), preferred_element_type=f32) * dmat
        o = (jnp.dot(inner.astype(bf16), v, preferred_element_type=f32)
             + jnp.dot(q, s.astype(bf16), preferred_element_type=f32) * qdec)
        kd = (k.astype(f32) * kdec).T.astype(bf16)
        s_new = s * cdec + jnp.dot(kd, v, preferred_element_type=f32)
        return r, o, s_new

    def body(c, carry):
        sf, sb = carry
        r, o, sf = chunk(c, sf, d_f, qdec_f, kdec_f, cdec_f)
        of_ref[pl.ds(r, C), :] = o
        r, o, sb = chunk(nc - 1 - c, sb, d_b, qdec_b, kdec_b, cdec_b)
        ob_ref[pl.ds(r, C), :] = o
        return sf, sb

    if use_s0:
        init = (s0_ref[0, 0, 0], s0_ref[0, 1, 0])
    else:
        init = (jnp.zeros((RET_DK, RET_DV), f32), jnp.zeros((RET_DK, RET_DV), f32))
    sf, sb = lax.fori_loop(0, nc, body, init, unroll=2)
    sfin_ref[0, 0, 0] = sf
    sfin_ref[0, 1, 0] = sb

    gn = gn_ref[...]

    def norm_body(c, carry):
        r = pl.multiple_of(c * C, C)
        o = of_ref[pl.ds(r, C), :] + ob_ref[pl.ds(r, C), :]
        mu = jnp.mean(o, axis=-1, keepdims=True)
        d = o - mu
        y = d * lax.rsqrt(jnp.mean(d * d, axis=-1, keepdims=True) + GN_EPS) * gn
        o_ref[0, pl.ds(r, C), :] = (y * g_ref[0, pl.ds(r, C), :].astype(f32)).astype(o_ref.dtype)
        return carry

    lax.fori_loop(0, nc, norm_body, 0, unroll=2)


def retention(main, decay_logit, gn_g, s0, nb, ls, b_off, nb_total, prev):
    use_s0 = s0 is not None
    nc = ls // RET_CHUNK
    dl = jnp.broadcast_to(decay_logit.astype(f32)[:, :, None, None], (2, RET_HEADS, 8, LANES))
    in_specs = [pl.BlockSpec((2, 1, 8, LANES), lambda b, h: (0, h, 0, 0)),
                pl.BlockSpec((1, ls, RET_DK), lambda b, h: (b + b_off, 0, OFF_RQ // RET_DK + h)),
                pl.BlockSpec((1, ls, RET_DK), lambda b, h: (b + b_off, 0, OFF_RK // RET_DK + h)),
                pl.BlockSpec((1, ls, RET_DV), lambda b, h: (b + b_off, 0, OFF_RV // RET_DV + h)),
                pl.BlockSpec((1, ls, RET_DV), lambda b, h: (b + b_off, 0, OFF_RG // RET_DV + h)),
                pl.BlockSpec((1, RET_DV), lambda b, h: (0, h))]
    args = [dl, main, main, main, main, gn_g.reshape(1, RET_HEADS * RET_DV)]
    if use_s0:
        in_specs.append(pl.BlockSpec((1, 2, 1, RET_DK, RET_DV), lambda b, h: (b, 0, h, 0, 0)))
        args.append(s0)
    kern, in_specs, args, aliases = _with_prev(functools.partial(_ret_kernel, nc=nc, use_s0=use_s0),
                                               in_specs, args, prev)
    return pl.pallas_call(
        kern,
        out_shape=(jax.ShapeDtypeStruct((nb_total, ls, RET_HEADS * RET_DV), bf16),
                   jax.ShapeDtypeStruct((nb, 2, RET_HEADS, RET_DK, RET_DV), f32)),
        grid=(nb, RET_HEADS),
        in_specs=in_specs,
        out_specs=(pl.BlockSpec((1, ls, RET_DV), lambda b, h: (b + b_off, 0, h)),
                   pl.BlockSpec((1, 2, 1, RET_DK, RET_DV), lambda b, h: (b, 0, h, 0, 0))),
        scratch_shapes=[pltpu.VMEM((ls, RET_DV), f32), pltpu.VMEM((ls, RET_DV), f32)],
        input_output_aliases=aliases,
        compiler_params=_cparams("arbitrary", "arbitrary"),
        name="retention",
    )(*args)


def _merge_kernel(a_ref, f_ref, r_ref, wa_ref, wf_ref, wr_ref, ga_ref, gf_ref, gr_ref, o_ref):
    m = ga_ref[...].astype(f32) * jnp.dot(a_ref[...], wa_ref[...], preferred_element_type=f32)
    m += gf_ref[...].astype(f32) * jnp.dot(f_ref[...], wf_ref[...], preferred_element_type=f32)
    m += gr_ref[...].astype(f32) * jnp.dot(r_ref[...], wr_ref[...], preferred_element_type=f32)
    o_ref[...] = m.astype(o_ref.dtype)


def merge_branches(att, four, ret, wa, wf, wr, main2d):
    m, d = att.shape
    tm = _tile(m, 512)
    tn = 512
    gb = OFF_GATES // tn
    nd = d // tn
    x_spec = pl.BlockSpec((tm, d), lambda i, j: (i, 0))
    w_spec = pl.BlockSpec((d, tn), lambda i, j: (0, j))
    return pl.pallas_call(
        _merge_kernel,
        out_shape=jax.ShapeDtypeStruct((m, d), bf16),
        grid=(m // tm, nd),
        in_specs=[x_spec, x_spec, x_spec, w_spec, w_spec, w_spec,
                  pl.BlockSpec((tm, tn), lambda i, j: (i, gb + j)),
                  pl.BlockSpec((tm, tn), lambda i, j: (i, gb + nd + j)),
                  pl.BlockSpec((tm, tn), lambda i, j: (i, gb + 2 * nd + j))],
        out_specs=pl.BlockSpec((tm, tn), lambda i, j: (i, j)),
        compiler_params=_cparams("arbitrary", "arbitrary"),
        name="merge_branches",
    )(att, four, ret, wa, wf, wr, main2d, main2d, main2d)


def _proj_res_kernel(m_ref, w_ref, x_ref, g_ref, o_ref):
    y = jnp.dot(m_ref[0], w_ref[...], preferred_element_type=f32)
    o_ref[0] = x_ref[0] + g_ref[0] * y


def proj_residual(merged, w, x, gate):
    nb, lr, d = x.shape
    tm = _tile(lr, 1024)
    tn = 1024
    return pl.pallas_call(
        _proj_res_kernel,
        out_shape=jax.ShapeDtypeStruct((nb, lr, d), f32),
        grid=(nb, lr // tm, d // tn),
        in_specs=[pl.BlockSpec((1, tm, d), lambda b, i, j: (b, i, 0)),
                  pl.BlockSpec((d, tn), lambda b, i, j: (0, j)),
                  pl.BlockSpec((1, tm, tn), lambda b, i, j: (b, i, j)),
                  pl.BlockSpec((1, 1, tn), lambda b, i, j: (b, 0, j))],
        out_specs=pl.BlockSpec((1, tm, tn), lambda b, i, j: (b, i, j)),
        compiler_params=_cparams("arbitrary", "arbitrary", "arbitrary"),
        name="proj_residual",
    )(merged, w, x, gate)


def _ffn_kernel(h_ref, wg_ref, wu_ref, wd_ref, x_ref, g_ref, o_ref, acc_ref):
    j = pl.program_id(2)

    @pl.when(j == 0)
    def _():
        acc_ref[...] = jnp.zeros_like(acc_ref)

    h = h_ref[0]
    a = jnp.dot(h, wg_ref[...], preferred_element_type=f32)
    u = jnp.dot(h, wu_ref[...], preferred_element_type=f32)
    t = (_silu(a) * u).astype(bf16)
    acc_ref[...] += jnp.dot(t, wd_ref[...], preferred_element_type=f32)

    @pl.when(j == pl.num_programs(2) - 1)
    def _():
        o_ref[0] = x_ref[0] + g_ref[0] * acc_ref[...]


def ffn_dense(h, wg, wu, wd, x, gate):
    nb, lr, d = x.shape
    dff = wg.shape[1]
    tm = _tile(lr, 512)
    tf = _tile(dff, 512)
    return pl.pallas_call(
        _ffn_kernel,
        out_shape=jax.ShapeDtypeStruct((nb, lr, d), f32),
        grid=(nb, lr // tm, dff // tf),
        in_specs=[pl.BlockSpec((1, tm, d), lambda b, i, j: (b, i, 0)),
                  pl.BlockSpec((d, tf), lambda b, i, j: (0, j)),
                  pl.BlockSpec((d, tf), lambda b, i, j: (0, j)),
                  pl.BlockSpec((tf, d), lambda b, i, j: (j, 0)),
                  pl.BlockSpec((1, tm, d), lambda b, i, j: (b, i, 0)),
                  pl.BlockSpec((1, 1, d), lambda b, i, j: (b, 0, 0))],
        out_specs=pl.BlockSpec((1, tm, d), lambda b, i, j: (b, i, 0)),
        scratch_shapes=[pltpu.VMEM((tm, d), f32)],
        compiler_params=_cparams("arbitrary", "arbitrary", "arbitrary"),
        name="ffn_dense",
    )(h, wg, wu, wd, x, gate)


MOE_TM = 512
MOE_TT = 256


def _router_kernel(h_ref, w_ref, e_ref, p_ref):
    logits = jnp.dot(h_ref[...].astype(bf16), w_ref[...], preferred_element_type=f32)
    lane = lax.broadcasted_iota(jnp.int32, logits.shape, 1)
    lane_f = lane.astype(f32)
    neg = jnp.full_like(logits, -jnp.inf)
    l0 = jnp.where(lane < N_EXPERTS, logits, neg)
    v1 = jnp.max(l0, axis=1, keepdims=True)
    i1 = jnp.min(jnp.where(l0 == v1, lane_f, float(LANES)), axis=1, keepdims=True)
    l1 = jnp.where(lane_f == i1, neg, l0)
    v2 = jnp.max(l1, axis=1, keepdims=True)
    i2 = jnp.min(jnp.where(l1 == v2, lane_f, float(LANES)), axis=1, keepdims=True)
    e = jnp.exp(v2 - v1)
    den = 1.0 + e
    e_ref[...] = jnp.where(lane == 0, i1, jnp.where(lane == 1, i2, 0.0)).astype(jnp.int32)
    p_ref[...] = jnp.where(lane == 0, 1.0 / den, jnp.where(lane == 1, e / den, 0.0))


def moe_router(h2d, w_router_pad):
    n, d = h2d.shape
    tm = _tile(n, 512)
    return pl.pallas_call(
        _router_kernel,
        out_shape=(jax.ShapeDtypeStruct((n, LANES), jnp.int32), jax.ShapeDtypeStruct((n, LANES), f32)),
        grid=(n // tm,),
        in_specs=[pl.BlockSpec((tm, d), lambda i: (i, 0)),
                  pl.BlockSpec((d, LANES), lambda i: (0, 0))],
        out_specs=(pl.BlockSpec((tm, LANES), lambda i: (i, 0)), pl.BlockSpec((tm, LANES), lambda i: (i, 0))),
        compiler_params=_cparams("arbitrary"),
        name="moe_router",
    )(h2d, w_router_pad)


def _dispatch_kernel(pos_ref, h_ref, xs_in_ref, xs_ref, sem):
    del xs_in_ref
    tt = h_ref.shape[0]

    def row_copy(t, p):
        return pltpu.make_async_copy(h_ref.at[pl.ds(t, 1), :], xs_ref.at[pl.ds(p, 1), :], sem)

    def start(t, c):
        row_copy(t, pos_ref[0, 0, 2 * t]).start()
        row_copy(t, pos_ref[0, 0, 2 * t + 1]).start()
        return c

    lax.fori_loop(0, tt, start, 0, unroll=8)
    tile_copy = pltpu.make_async_copy(h_ref, xs_ref.at[pl.ds(0, tt), :], sem)
    tile_copy.wait()
    tile_copy.wait()


def moe_dispatch(h2d, pos, p_rows):
    n, d = h2d.shape
    tt = _tile(n, MOE_TT)
    pos3 = pos.reshape(n // tt, 1, 2 * tt)
    xs0 = jnp.zeros((p_rows, d), h2d.dtype)
    return pl.pallas_call(
        _dispatch_kernel,
        out_shape=jax.ShapeDtypeStruct((p_rows, d), h2d.dtype),
        grid=(n // tt,),
        in_specs=[pl.BlockSpec((1, 1, 2 * tt), lambda i: (i, 0, 0), memory_space=pltpu.SMEM),
                  pl.BlockSpec((tt, d), lambda i: (i, 0)),
                  pl.BlockSpec(memory_space=pl.ANY)],
        out_specs=pl.BlockSpec(memory_space=pl.ANY),
        scratch_shapes=[pltpu.SemaphoreType.DMA(())],
        input_output_aliases={2: 0},
        compiler_params=_cparams("arbitrary"),
        name="moe_dispatch",
    )(pos3, h2d, xs0)


def _moe_ffn_kernel(be_ref, nu_ref, x_ref, wg_ref, wu_ref, wd_ref, y_ref, xb_ref, acc_ref):
    del be_ref
    i = pl.program_id(0)
    j = pl.program_id(1)
    active = i < nu_ref[0]

    @pl.when(jnp.logical_and(active, j == 0))
    def _():
        xb_ref[...] = x_ref[...].astype(bf16)
        acc_ref[...] = jnp.zeros_like(acc_ref)

    @pl.when(active)
    def _():
        xb = xb_ref[...]
        a = jnp.dot(xb, wg_ref[0], preferred_element_type=f32)
        u = jnp.dot(xb, wu_ref[0], preferred_element_type=f32)
        t = (_silu(a) * u).astype(bf16)
        acc_ref[...] += jnp.dot(t, wd_ref[0], preferred_element_type=f32)

    @pl.when(j == pl.num_programs(1) - 1)
    def _():
        y_ref[...] = jnp.where(active, acc_ref[...], 0.0)


def moe_ffn(xs, blk_expert, n_used, wg, wu, wd):
    p_rows, d = xs.shape
    dff = wg.shape[2]
    tm = MOE_TM
    tf = _tile(dff, 512)
    nf = dff // tf

    def jj(i, j, nu):
        return jnp.where(i < nu[0], j, nf - 1)

    grid_spec = pltpu.PrefetchScalarGridSpec(
        num_scalar_prefetch=2,
        grid=(p_rows // tm, nf),
        in_specs=[pl.BlockSpec((tm, d), lambda i, j, be, nu: (i, 0)),
                  pl.BlockSpec((1, d, tf), lambda i, j, be, nu: (be[i], 0, jj(i, j, nu))),
                  pl.BlockSpec((1, d, tf), lambda i, j, be, nu: (be[i], 0, jj(i, j, nu))),
                  pl.BlockSpec((1, tf, d), lambda i, j, be, nu: (be[i], jj(i, j, nu), 0))],
        out_specs=pl.BlockSpec((tm, d), lambda i, j, be, nu: (i, 0)),
        scratch_shapes=[pltpu.VMEM((tm, d), bf16), pltpu.VMEM((tm, d), f32)],
    )
    return pl.pallas_call(
        _moe_ffn_kernel,
        out_shape=jax.ShapeDtypeStruct((p_rows, d), f32),
        grid_spec=grid_spec,
        compiler_params=_cparams("arbitrary", "arbitrary"),
        name="moe_ffn",
    )(blk_expert, n_used, xs, wg, wu, wd)


def _combine_kernel(pos_ref, y_ref, p_ref, x_ref, g_ref, o_ref, ybuf_ref, sem):
    tt = x_ref.shape[1]

    def row_copy(p, dst):
        return pltpu.make_async_copy(y_ref.at[pl.ds(p, 1), :], ybuf_ref.at[pl.ds(dst, 1), :], sem)

    def start(t, c):
        row_copy(pos_ref[0, 0, 2 * t], t).start()
        row_copy(pos_ref[0, 0, 2 * t + 1], tt + t).start()
        return c

    lax.fori_loop(0, tt, start, 0, unroll=8)
    pltpu.make_async_copy(y_ref.at[pl.ds(0, 2 * tt), :], ybuf_ref, sem).wait()
    p = p_ref[0]
    mix = p[:, 0:1] * ybuf_ref[pl.ds(0, tt), :] + p[:, 1:2] * ybuf_ref[pl.ds(tt, tt), :]
    o_ref[0] = x_ref[0] + g_ref[0] * mix


def moe_combine(y, pos, probs, x, gate):
    nb, lr, d = x.shape
    tt = _tile(lr, MOE_TT)
    nt = lr // tt
    pos3 = pos.reshape(nb * nt, 1, 2 * tt)
    probs3 = probs.reshape(nb, lr, LANES)
    return pl.pallas_call(
        _combine_kernel,
        out_shape=jax.ShapeDtypeStruct((nb, lr, d), f32),
        grid=(nb, nt),
        in_specs=[pl.BlockSpec((1, 1, 2 * tt), lambda b, i: (b * nt + i, 0, 0), memory_space=pltpu.SMEM),
                  pl.BlockSpec(memory_space=pl.ANY),
                  pl.BlockSpec((1, tt, LANES), lambda b, i: (b, i, 0)),
                  pl.BlockSpec((1, tt, d), lambda b, i: (b, i, 0)),
                  pl.BlockSpec((1, 1, d), lambda b, i: (b, 0, 0))],
        out_specs=pl.BlockSpec((1, tt, d), lambda b, i: (b, i, 0)),
        scratch_shapes=[pltpu.VMEM((2 * tt, d), f32), pltpu.SemaphoreType.DMA(())],
        compiler_params=_cparams("arbitrary", "arbitrary"),
        name="moe_combine",
    )(pos3, y, probs3, x, gate)


def moe_layer(h, x, gate, w_router, wg, wu, wd):
    nb, lr, d = h.shape
    n = nb * lr
    h2d = h.reshape(n, d)
    w_r = jnp.zeros((d, LANES), bf16).at[:, :N_EXPERTS].set(w_router.astype(bf16))
    e_out, p_out = moe_router(h2d, w_r)
    e_flat = e_out[:, :2].reshape(-1)
    onehot = (e_flat[:, None] == jnp.arange(N_EXPERTS, dtype=jnp.int32)[None, :]).astype(jnp.int32)
    csum = jnp.cumsum(onehot, axis=0)
    counts = csum[-1]
    rank = jnp.sum(onehot * csum, axis=1) - 1
    padded = (counts + MOE_TM - 1) // MOE_TM * MOE_TM
    pad_ends = jnp.cumsum(padded)
    pad_starts = pad_ends - padded
    pos = jnp.sum(onehot * pad_starts[None, :], axis=1) + rank
    n_blocks = (2 * n) // MOE_TM + N_EXPERTS
    blk_start = jnp.arange(n_blocks, dtype=jnp.int32) * MOE_TM
    blk_expert = jnp.minimum(jnp.sum((blk_start[:, None] >= pad_ends[None, :]).astype(jnp.int32), axis=1),
                             N_EXPERTS - 1).astype(jnp.int32)
    n_used = (pad_ends[-1] // MOE_TM).astype(jnp.int32).reshape(1)
    xs = moe_dispatch(h2d, pos.astype(jnp.int32), n_blocks * MOE_TM)
    y = moe_ffn(xs, blk_expert, n_used, wg, wu, wd)
    return moe_combine(y, pos.astype(jnp.int32), p_out, x, gate)


def _rope_tables(n_tok):
    rows = n_tok // GRID_W
    r = jnp.repeat(jnp.arange(rows, dtype=f32), GRID_W)
    cl = jnp.tile(jnp.arange(GRID_W, dtype=f32), rows)
    n_freq = MLA_ROPE // 4
    inv = ROPE_BASE ** (-jnp.arange(n_freq, dtype=f32) / n_freq)
    ang = jnp.concatenate([r[:, None] * inv, cl[:, None] * inv], axis=-1)
    cos, sin = jnp.cos(ang), jnp.sin(ang)
    return jnp.tile(cos, (1, 4)), jnp.concatenate([-sin, sin, -sin, sin], axis=-1)


def _prep_in_weights(w_in_l):
    d = w_in_l.shape[0]
    nq = MLA_HEADS * (MLA_NOPE + MLA_ROPE)
    wq = w_in_l[:, :nq].reshape(d, MLA_HEADS, MLA_NOPE + MLA_ROPE)
    w_ckv = w_in_l[:, nq:nq + KV_LORA]
    w_kr = w_in_l[:, nq + KV_LORA:nq + KV_LORA + MLA_ROPE]
    rest = w_in_l[:, nq + KV_LORA + MLA_ROPE:]
    w_main = jnp.concatenate([wq[:, :, :MLA_NOPE].reshape(d, Q_NOPE_W), wq[:, :, MLA_NOPE:].reshape(d, Q_ROPE_W),
                              rest], axis=1).astype(bf16)
    w_ckr = jnp.concatenate([w_ckv, w_kr, w_kr], axis=1).astype(bf16)
    return w_main, w_ckr


def kernel(x_prompt, x_sample, c, cache_ckv, cache_krope, state_ret, c_ctx, w_ada, b_ada, norm1_g, w_in, kv_norm_g, w_ukv, ret_decay_logit, ret_gn_g, w_o_mla, w_o_fnet, w_o_ret, w_out, norm2_g, w_gate_dense, w_up_dense, w_down_dense, w_router, w_gate_moe, w_up_moe, w_down_moe, final_norm_g):
    nbc, seq, d = x_prompt.shape
    nbs, ls, _ = x_sample.shape
    depth = w_in.shape[0]
    lr = nbc * seq
    assert lr == ls, "unified row layout needs BATCH * SEQ == DEC_SEQ"
    nb = 1 + nbs
    n_rows = nb * lr

    x = jnp.concatenate([x_prompt.reshape(1, lr, d), x_sample], axis=0)
    cond = jnp.concatenate([c_ctx[None, :], c, jnp.zeros((16 - nb, d), f32)], axis=0)
    mod = adaln_all(cond, w_ada, b_ada)[:, :nb]
    cos128, sin128 = _rope_tables(ls)

    w_ch = jnp.concatenate(_dft_mats(FNET_GC), axis=1).astype(bf16)
    c_c, s_c = _dft_mats(seq)
    cs_ctx = jnp.concatenate([c_c, -s_c], axis=1).astype(bf16)
    c_s, s_s = _dft_mats(ls)
    lh = ls // 2
    c_half = c_s[:, :lh] * jnp.where(jnp.arange(lh) == 0, 0.5, 1.0)[None, :]
    cs_lat = jnp.concatenate([c_half, -s_s[:, :lh]], axis=1).astype(bf16)

    ckv_list, krope_list, ret_list = [], [], []
    spare = []
    for l in range(depth):
        sh1, sc1, g1, sh2, sc2, g2 = [m.reshape(nb, 1, d) for m in jnp.split(mod[l], 6, axis=-1)]
        w_main, w_ckr = _prep_in_weights(w_in[l])
        h = norm_mod(x, norm1_g[l], sc1, sh1, bf16)
        main = inproj_main(h, w_main, cos128, sin128)
        ckv, kr, krr = inproj_ckr(h, w_ckr, kv_norm_g[l], cos128, sin128)
        w_ukv_b = w_ukv[l].astype(bf16)

        ckv_c = ckv[0].reshape(nbc, seq, KV_LORA)
        kr_c = kr[0].reshape(nbc, seq, LANES)
        kv_c = matmul(ckv_c.reshape(lr, KV_LORA).astype(bf16), w_ukv_b, bf16).reshape(nbc, seq, -1)
        main_c = main.reshape(nb * nbc, seq, N_MAIN)
        bufs = [h] + spare
        bufs += [jnp.zeros((nb, lr, d), bf16) for _ in range(3 - len(bufs))]
        bufs = [bb.reshape(nb * nbc, seq, d) for bb in bufs[:3]]
        att_c = attention(main_c, kv_c, kr_c.astype(bf16), nbc, seq, 0, nb * nbc, bufs[0])
        ab_c = dft_channels(main_c, w_ch, nbc, seq, 0).reshape(nbc, 2 * seq, FNET_GROUPS * FNET_GC)
        four_c = dft_sequence(cs_ctx, ab_c, 0, nb * nbc, bufs[1])
        ret_c, sfin_c = retention(main_c, ret_decay_logit[l], ret_gn_g[l], None, nbc, seq, 0, nb * nbc, bufs[2])
        ckv_list.append(ckv_c)
        krope_list.append(kr_c[:, :, :MLA_ROPE])
        ret_list.append(sfin_c)

        ckv_keys = jnp.concatenate([ckv[1:], cache_ckv[:, l]], axis=1).astype(bf16)
        lk = ckv_keys.shape[1]
        kv_s = matmul(ckv_keys.reshape(nbs * lk, KV_LORA), w_ukv_b, bf16).reshape(nbs, lk, -1)
        ck = cache_krope[:, l].astype(bf16)
        kr_keys = jnp.concatenate([krr[1:], jnp.concatenate([ck, ck], axis=-1)], axis=1)
        att = attention(main, kv_s, kr_keys, nbs, ls, 1, nb, att_c.reshape(nb, ls, -1)).reshape(n_rows, -1)
        ab_s = dft_channels_sym(main, w_ch, nbs, ls, 1).reshape(nbs, ls, FNET_GROUPS * FNET_GC)
        f_mid = main[1:, lh:lh + 8, OFF_F:OFF_F + FNET_GROUPS * FNET_GC]
        a_mid = matmul(f_mid.reshape(nbs * 8 * FNET_GROUPS, FNET_GC), w_ch[:, :FNET_GC], f32)
        a_mid = a_mid.reshape(nbs, 8, FNET_GROUPS * FNET_GC)
        four = dft_sequence_sym(cs_lat, ab_s, a_mid, 1, nb, four_c.reshape(nb, ls, -1)).reshape(n_rows, -1)
        ret, _ = retention(main, ret_decay_logit[l], ret_gn_g[l], state_ret[:, l], nbs, ls, 1, nb,
                           ret_c.reshape(nb, ls, -1))
        ret = ret.reshape(n_rows, -1)
        spare = [att, four]

        merged = merge_branches(att, four, ret, w_o_mla[l].astype(bf16), w_o_fnet[l].astype(bf16),
                                w_o_ret[l].astype(bf16), main.reshape(n_rows, N_MAIN))
        x = proj_residual(merged.reshape(nb, lr, d), w_out[l].astype(bf16), x, g1)

        i = l // 2
        if l % 2 == 0:
            h2 = norm_mod(x, norm2_g[l], sc2, sh2, bf16)
            x = ffn_dense(h2, w_gate_dense[i].astype(bf16), w_up_dense[i].astype(bf16),
                          w_down_dense[i].astype(bf16), x, g2)
        else:
            h2 = norm_mod(x, norm2_g[l], sc2, sh2, f32)
            x = moe_layer(h2, x, g2, w_router[i], w_gate_moe[i].astype(bf16), w_up_moe[i].astype(bf16),
                          w_down_moe[i].astype(bf16))

    y_prompt = final_norm(x, final_norm_g, 0, 1).reshape(nbc, seq, d)
    y_sample = final_norm(x, final_norm_g, 1, nbs)
    new_ckv = jnp.stack(ckv_list, axis=1)
    new_krope = jnp.stack(krope_list, axis=1)
    new_ret = jnp.stack(ret_list, axis=1)
    return (y_prompt, y_sample, new_ckv, new_krope, new_ret)
```

```python
import functools
import math

import jax
import jax.numpy as jnp
from jax import lax
from jax.experimental import pallas as pl
from jax.experimental.pallas import tpu as pltpu

f32 = jnp.float32
bf16 = jnp.bfloat16

D_MODEL = 2048
GRID_W = 64
MLA_HEADS = 16
MLA_NOPE = 128
MLA_ROPE = 64
MLA_V = 128
KV_LORA = 512
ROPE_BASE = 10000.0
FNET_GROUPS = 4
FNET_GC = 512
RET_HEADS = 8
RET_DK = 128
RET_DV = 256
RET_CHUNK = 256
N_EXPERTS = 8
EPS = 1e-6
GN_EPS = 1e-5

LANES = 128
VMEM_LIMIT_BYTES = 56 * 1024 * 1024

Q_NOPE_W = MLA_HEADS * MLA_NOPE
Q_ROPE_W = MLA_HEADS * MLA_ROPE
OFF_Q = 0
OFF_QROPE = Q_NOPE_W
OFF_F = OFF_Q + Q_NOPE_W + Q_ROPE_W
OFF_RQ = OFF_F + FNET_GROUPS * FNET_GC
OFF_RK = OFF_RQ + RET_HEADS * RET_DK
OFF_RV = OFF_RK + RET_HEADS * RET_DK
OFF_RG = OFF_RV + RET_HEADS * RET_DV
OFF_GATES = OFF_RG + RET_HEADS * RET_DV
N_MAIN = OFF_GATES + 3 * D_MODEL
CKR_W = KV_LORA + LANES
Q_SCALE = (MLA_NOPE + MLA_ROPE) ** -0.5 * math.log2(math.e)


def _cparams(*sem):
    return pltpu.CompilerParams(dimension_semantics=sem, vmem_limit_bytes=VMEM_LIMIT_BYTES)


def _tile(n, pref):
    t = min(n, pref)
    while n % t:
        t -= 8
    return t


def _silu(x):
    return x * jax.nn.sigmoid(x)


def _rope128(x, cos, sin):
    lane = lax.broadcasted_iota(jnp.int32, x.shape, 1)
    first = (lane % MLA_ROPE) < (MLA_ROPE // 2)
    swapped = jnp.where(first, pltpu.roll(x, LANES - MLA_ROPE // 2, 1), pltpu.roll(x, MLA_ROPE // 2, 1))
    return x * cos + swapped * sin


def _adaln_kernel(c_ref, w_ref, b_ref, o_ref):
    s = _silu(c_ref[...]).astype(bf16)
    o_ref[0] = jnp.dot(s, w_ref[0].astype(bf16), preferred_element_type=f32) + b_ref[0]


def adaln_all(cond, w_ada, b_ada):
    depth, d, n = w_ada.shape
    r = cond.shape[0]
    tn = _tile(n, 1024)
    return pl.pallas_call(
        _adaln_kernel,
        out_shape=jax.ShapeDtypeStruct((depth, r, n), f32),
        grid=(depth, n // tn),
        in_specs=[pl.BlockSpec((r, d), lambda l, j: (0, 0)),
                  pl.BlockSpec((1, d, tn), lambda l, j: (l, 0, j)),
                  pl.BlockSpec((1, 1, tn), lambda l, j: (l, 0, j))],
        out_specs=pl.BlockSpec((1, r, tn), lambda l, j: (l, 0, j)),
        compiler_params=_cparams("arbitrary", "arbitrary"),
        name="adaln",
    )(cond, w_ada, b_ada.reshape(depth, 1, n))


def _norm_mod_kernel(x_ref, g_ref, sc_ref, sh_ref, o_ref):
    x = x_ref[0]
    y = x * lax.rsqrt(jnp.mean(x * x, axis=-1, keepdims=True) + EPS)
    y = y * g_ref[...]
    o_ref[0] = (y * (1.0 + sc_ref[0]) + sh_ref[0]).astype(o_ref.dtype)


def norm_mod(x, g, sc, sh, out_dtype):
    nb, lr, d = x.shape
    tm = _tile(lr, 512)
    return pl.pallas_call(
        _norm_mod_kernel,
        out_shape=jax.ShapeDtypeStruct((nb, lr, d), out_dtype),
        grid=(nb, lr // tm),
        in_specs=[pl.BlockSpec((1, tm, d), lambda b, i: (b, i, 0)),
                  pl.BlockSpec((1, d), lambda b, i: (0, 0)),
                  pl.BlockSpec((1, 1, d), lambda b, i: (b, 0, 0)),
                  pl.BlockSpec((1, 1, d), lambda b, i: (b, 0, 0))],
        out_specs=pl.BlockSpec((1, tm, d), lambda b, i: (b, i, 0)),
        compiler_params=_cparams("arbitrary", "arbitrary"),
        name="norm_mod",
    )(x, g.reshape(1, d), sc, sh)


def _final_norm_kernel(x_ref, g_ref, o_ref):
    x = x_ref[0]
    y = x * lax.rsqrt(jnp.mean(x * x, axis=-1, keepdims=True) + EPS)
    o_ref[0] = y * g_ref[...]


def final_norm(x, g, b_off, nb_out):
    _, lr, d = x.shape
    tm = _tile(lr, 512)
    return pl.pallas_call(
        _final_norm_kernel,
        out_shape=jax.ShapeDtypeStruct((nb_out, lr, d), f32),
        grid=(nb_out, lr // tm),
        in_specs=[pl.BlockSpec((1, tm, d), lambda b, i: (b + b_off, i, 0)),
                  pl.BlockSpec((1, d), lambda b, i: (0, 0))],
        out_specs=pl.BlockSpec((1, tm, d), lambda b, i: (b, i, 0)),
        compiler_params=_cparams("arbitrary", "arbitrary"),
        name="final_norm",
    )(x, g.reshape(1, d))


def _inproj_kernel(h_ref, w_ref, cos_ref, sin_ref, o_ref, *, tn, nchunk):
    b = pl.program_id(0)
    j = pl.program_id(2)
    tm = h_ref.shape[1]
    rc = tm // nchunk
    j_rope = OFF_QROPE // tn
    j_rk = OFF_RK // tn
    j_rv = OFF_RV // tn
    j_rg = OFF_RG // tn
    j_gates = OFF_GATES // tn
    is_q = j < OFF_F // tn
    is_rope = jnp.logical_and(jnp.logical_and(j >= j_rope, is_q), b >= 1)
    is_qplain = jnp.logical_and(is_q, jnp.logical_not(is_rope))
    is_rk = jnp.logical_and(j >= j_rk, j < j_rv)
    is_silu = jnp.logical_and(j >= j_rg, j < j_gates)
    is_sig = j >= j_gates
    plain = jnp.logical_not(is_q | is_rk | is_silu | is_sig)

    def run(epilogue):
        pending = None
        for r in range(nchunk):
            rows = slice(r * rc, (r + 1) * rc)
            acc = jnp.dot(h_ref[0, rows, :], w_ref[...], preferred_element_type=f32)
            if pending is not None:
                prow, pacc = pending
                o_ref[0, prow, :] = epilogue(pacc, prow).astype(o_ref.dtype)
            pending = (rows, acc)
        prow, pacc = pending
        o_ref[0, prow, :] = epilogue(pacc, prow).astype(o_ref.dtype)

    @pl.when(plain)
    def _():
        run(lambda acc, rows: acc)

    @pl.when(is_qplain)
    def _():
        run(lambda acc, rows: acc * Q_SCALE)

    @pl.when(is_rope)
    def _():
        def rope(acc, rows):
            cos = cos_ref[rows, :] * Q_SCALE
            sin = sin_ref[rows, :] * Q_SCALE
            return jnp.concatenate([_rope128(acc[:, s * LANES:(s + 1) * LANES], cos, sin)
                                    for s in range(tn // LANES)], axis=1)
        run(rope)

    @pl.when(is_rk)
    def _():
        run(lambda acc, rows: acc * (RET_DK ** -0.5))

    @pl.when(is_silu)
    def _():
        run(lambda acc, rows: _silu(acc))

    @pl.when(is_sig)
    def _():
        run(lambda acc, rows: jax.nn.sigmoid(acc))


def inproj_main(h, w_main, cos128, sin128):
    nb, lr, d = h.shape
    tm = _tile(lr, 1024)
    tn = 1024
    return pl.pallas_call(
        functools.partial(_inproj_kernel, tn=tn, nchunk=4),
        out_shape=jax.ShapeDtypeStruct((nb, lr, N_MAIN), bf16),
        grid=(nb, lr // tm, N_MAIN // tn),
        in_specs=[pl.BlockSpec((1, tm, d), lambda b, i, j: (b, i, 0)),
                  pl.BlockSpec((d, tn), lambda b, i, j: (0, j)),
                  pl.BlockSpec((tm, LANES), lambda b, i, j: (i, 0)),
                  pl.BlockSpec((tm, LANES), lambda b, i, j: (i, 0))],
        out_specs=pl.BlockSpec((1, tm, tn), lambda b, i, j: (b, i, j)),
        compiler_params=_cparams("arbitrary", "arbitrary", "arbitrary"),
        name="inproj_main",
    )(h, w_main, cos128, sin128)


def _ckr_kernel(h_ref, w_ref, g_ref, cos_ref, sin_ref, ckv_ref, kr_ref, krr_ref):
    acc = jnp.dot(h_ref[0], w_ref[...], preferred_element_type=f32)
    c = acc[:, :KV_LORA]
    y = c * lax.rsqrt(jnp.mean(c * c, axis=-1, keepdims=True) + EPS)
    ckv_ref[0] = y * g_ref[...]
    kr = acc[:, KV_LORA:]
    kr_ref[0] = kr
    krr_ref[0] = _rope128(kr, cos_ref[...], sin_ref[...]).astype(krr_ref.dtype)


def inproj_ckr(h, w_ckr, kv_g, cos128, sin128):
    nb, lr, d = h.shape
    tm = _tile(lr, 1024)
    return pl.pallas_call(
        _ckr_kernel,
        out_shape=(jax.ShapeDtypeStruct((nb, lr, KV_LORA), f32),
                   jax.ShapeDtypeStruct((nb, lr, LANES), f32),
                   jax.ShapeDtypeStruct((nb, lr, LANES), bf16)),
        grid=(nb, lr // tm),
        in_specs=[pl.BlockSpec((1, tm, d), lambda b, i: (b, i, 0)),
                  pl.BlockSpec((d, CKR_W), lambda b, i: (0, 0)),
                  pl.BlockSpec((1, KV_LORA), lambda b, i: (0, 0)),
                  pl.BlockSpec((tm, LANES), lambda b, i: (i, 0)),
                  pl.BlockSpec((tm, LANES), lambda b, i: (i, 0))],
        out_specs=(pl.BlockSpec((1, tm, KV_LORA), lambda b, i: (b, i, 0)),
                   pl.BlockSpec((1, tm, LANES), lambda b, i: (b, i, 0)),
                   pl.BlockSpec((1, tm, LANES), lambda b, i: (b, i, 0))),
        compiler_params=_cparams("arbitrary", "arbitrary"),
        name="inproj_ckr",
    )(h, w_ckr, kv_g.reshape(1, KV_LORA), cos128, sin128)


def _mm_kernel(x_ref, w_ref, o_ref):
    o_ref[...] = jnp.dot(x_ref[...], w_ref[...], preferred_element_type=f32).astype(o_ref.dtype)


def matmul(x, w, out_dtype, tm_pref=512, tn_pref=4096):
    m, k = x.shape
    _, n = w.shape
    tm = _tile(m, tm_pref)
    tn = _tile(n, tn_pref)
    return pl.pallas_call(
        _mm_kernel,
        out_shape=jax.ShapeDtypeStruct((m, n), out_dtype),
        grid=(m // tm, n // tn),
        in_specs=[pl.BlockSpec((tm, k), lambda i, j: (i, 0)),
                  pl.BlockSpec((k, tn), lambda i, j: (0, j))],
        out_specs=pl.BlockSpec((tm, tn), lambda i, j: (i, j)),
        compiler_params=_cparams("arbitrary", "arbitrary"),
        name="matmul",
    )(x, w)


def _attn_single_kernel(qn_ref, qr_ref, kn_ref, v_ref, kr_ref, o_ref, kcat_ref, vt_ref, s_ref, *, nk, tk, nparts):
    h = pl.program_id(1)
    qi = pl.program_id(2)
    lk = kcat_ref.shape[0]

    @pl.when(qi == 0)
    def _():
        kcat_ref[:, :MLA_NOPE] = kn_ref[0]
        kcat_ref[:, MLA_NOPE:] = kr_ref[0]
        for c in range(nk):
            vt_ref[:, c * tk:(c + 1) * tk] = v_ref[0, c * tk:(c + 1) * tk, :].astype(f32).T.astype(bf16)

    qr = qr_ref[0]
    lane = lax.broadcasted_iota(jnp.int32, qr.shape, 1)
    qr = jnp.where((lane // MLA_ROPE) == (h % 2), qr, jnp.zeros_like(qr))
    qcat = jnp.concatenate([qn_ref[0], qr], axis=1)
    tq = qcat.shape[0]

    bounds = [((nk * i) // nparts) * tk for i in range(nparts + 1)]
    m = jnp.full((1, tq), -jnp.inf, f32)
    pending = None
    for i in range(nparts):
        r0, r1 = bounds[i], bounds[i + 1]
        s = lax.dot_general(kcat_ref[r0:r1, :], qcat, (((1,), (1,)), ((), ())), preferred_element_type=f32)
        s_ref[r0:r1, :] = s
        if pending is not None:
            m = jnp.maximum(m, jnp.max(pending, axis=0, keepdims=True))
        pending = s
    m = jnp.maximum(m, jnp.max(pending, axis=0, keepdims=True))

    l = jnp.zeros((1, tq), f32)
    acc = jnp.zeros((MLA_V, tq), f32)
    for c in range(nk):
        p = jnp.exp2(s_ref[c * tk:(c + 1) * tk, :] - m)
        l = l + jnp.sum(p, axis=0, keepdims=True)
        acc = acc + jnp.dot(vt_ref[:, c * tk:(c + 1) * tk], p.astype(bf16), preferred_element_type=f32)
    o_ref[0] = (acc * (1.0 / l)).T.astype(o_ref.dtype)


def _attn_kernel(qn_ref, qr_ref, kn_ref, v_ref, kr_ref, o_ref, kcat_ref, vt_ref, sa_ref, sb_ref, ma_ref, mb_ref,
                 *, nk, tk, nparts):
    h = pl.program_id(1)
    i = pl.program_id(2)

    @pl.when(i == 0)
    def _():
        kcat_ref[:, :MLA_NOPE] = kn_ref[0]
        kcat_ref[:, MLA_NOPE:] = kr_ref[0]
        for c in range(nk):
            vt_ref[:, c * tk:(c + 1) * tk] = v_ref[0, c * tk:(c + 1) * tk, :].astype(f32).T.astype(bf16)
        sb_ref[...] = jnp.zeros(sb_ref.shape, f32)
        mb_ref[...] = jnp.zeros(mb_ref.shape, f32)

    def step(s_new, s_old, m_new_ref, m_old_ref):
        qr = qr_ref[0]
        lane = lax.broadcasted_iota(jnp.int32, qr.shape, 1)
        qr = jnp.where((lane // MLA_ROPE) == (h % 2), qr, jnp.zeros_like(qr))
        qcat = jnp.concatenate([qn_ref[0], qr], axis=1)
        tq = qcat.shape[0]
        m_old = m_old_ref[0:1, :]
        bounds = [(nk * g) // nparts for g in range(nparts + 1)]
        m = jnp.full((1, tq), -jnp.inf, f32)
        l = jnp.zeros((1, tq), f32)
        acc = jnp.zeros((MLA_V, tq), f32)
        for g in range(nparts):
            r0, r1 = bounds[g] * tk, bounds[g + 1] * tk
            s = lax.dot_general(kcat_ref[r0:r1, :], qcat, (((1,), (1,)), ((), ())), preferred_element_type=f32)
            s_new[r0:r1, :] = s
            m = jnp.maximum(m, jnp.max(s, axis=0, keepdims=True))
            for c in range(bounds[g], bounds[g + 1]):
                p = jnp.exp2(s_old[c * tk:(c + 1) * tk, :] - m_old)
                l = l + jnp.sum(p, axis=0, keepdims=True)
                acc = acc + jnp.dot(vt_ref[:, c * tk:(c + 1) * tk], p.astype(bf16), preferred_element_type=f32)
        m_new_ref[...] = jnp.broadcast_to(m, m_new_ref.shape)
        o_ref[0] = (acc * (1.0 / l)).T.astype(o_ref.dtype)

    @pl.when(i % 2 == 0)
    def _():
        step(sa_ref, sb_ref, ma_ref, mb_ref)

    @pl.when(i % 2 == 1)
    def _():
        step(sb_ref, sa_ref, mb_ref, ma_ref)


def _with_prev(kern, in_specs, args, prev):
    if prev is None:
        return kern, in_specs, args, {}
    n_in = len(args)

    def wrapped(*refs):
        return kern(*refs[:n_in], *refs[n_in + 1:])

    return wrapped, in_specs + [pl.BlockSpec(memory_space=pl.ANY)], args + [prev], {n_in: 0}


def attention(main, kv, krope, nb, lq, b_off, nb_total, prev):
    lk = kv.shape[1]
    tq = _tile(lq, 256)
    tk = _tile(lk, 256)
    nq = lq // tq
    qr_blk = OFF_QROPE // LANES
    nk = lk // tk
    nparts = min(4, nk)
    skewed = nq > 1
    if skewed:
        steps = nq + 1
        q_idx = lambda i: jnp.minimum(i, nq - 1)
        o_idx = lambda i: jnp.maximum(i - 1, 0)
        kern = functools.partial(_attn_kernel, nk=nk, tk=tk, nparts=nparts)
        scratch = [pltpu.VMEM((lk, MLA_NOPE + LANES), bf16), pltpu.VMEM((MLA_V, lk), bf16),
                   pltpu.VMEM((lk, tq), f32), pltpu.VMEM((lk, tq), f32),
                   pltpu.VMEM((8, tq), f32), pltpu.VMEM((8, tq), f32)]
    else:
        steps = nq
        q_idx = lambda i: i
        o_idx = lambda i: i
        kern = functools.partial(_attn_single_kernel, nk=nk, tk=tk, nparts=nparts)
        scratch = [pltpu.VMEM((lk, MLA_NOPE + LANES), bf16), pltpu.VMEM((MLA_V, lk), bf16),
                   pltpu.VMEM((lk, tq), f32)]
    in_specs = [pl.BlockSpec((1, tq, MLA_NOPE), lambda b, h, i: (b + b_off, q_idx(i), h)),
                pl.BlockSpec((1, tq, LANES), lambda b, h, i: (b + b_off, q_idx(i), qr_blk + h // 2)),
                pl.BlockSpec((1, lk, MLA_NOPE), lambda b, h, i: (b, 0, 2 * h)),
                pl.BlockSpec((1, lk, MLA_V), lambda b, h, i: (b, 0, 2 * h + 1)),
                pl.BlockSpec((1, lk, LANES), lambda b, h, i: (b, 0, 0))]
    kern, in_specs, args, aliases = _with_prev(kern, in_specs, [main, main, kv, kv, krope], prev)
    return pl.pallas_call(
        kern,
        out_shape=jax.ShapeDtypeStruct((nb_total, lq, MLA_HEADS * MLA_V), bf16),
        grid=(nb, MLA_HEADS, steps),
        in_specs=in_specs,
        out_specs=pl.BlockSpec((1, tq, MLA_V), lambda b, h, i: (b + b_off, o_idx(i), h)),
        scratch_shapes=scratch,
        input_output_aliases=aliases,
        compiler_params=_cparams("arbitrary", "arbitrary", "arbitrary"),
        name="attention",
    )(*args)


def _dft_ch_kernel(xa_ref, xb_ref, w_ref, o_ref):
    half = FNET_GROUPS // 2
    for g in range(FNET_GROUPS):
        x_ref = xa_ref if g < half else xb_ref
        lo = (g % half) * FNET_GC
        y = jnp.dot(x_ref[0, :, lo:lo + FNET_GC], w_ref[...], preferred_element_type=f32)
        o_ref[0, 0, :, g * FNET_GC:(g + 1) * FNET_GC] = y[:, :FNET_GC].astype(o_ref.dtype)
        o_ref[0, 1, :, g * FNET_GC:(g + 1) * FNET_GC] = y[:, FNET_GC:].astype(o_ref.dtype)


def dft_channels(main, w_ch, nb, ls, b_off):
    tm = _tile(ls, 512)
    wb = FNET_GROUPS * FNET_GC // 2
    f_blk = OFF_F // wb
    return pl.pallas_call(
        _dft_ch_kernel,
        out_shape=jax.ShapeDtypeStruct((nb, 2, ls, FNET_GROUPS * FNET_GC), bf16),
        grid=(nb, ls // tm),
        in_specs=[pl.BlockSpec((1, tm, wb), lambda b, i: (b + b_off, i, f_blk)),
                  pl.BlockSpec((1, tm, wb), lambda b, i: (b + b_off, i, f_blk + 1)),
                  pl.BlockSpec((FNET_GC, 2 * FNET_GC), lambda b, i: (0, 0))],
        out_specs=pl.BlockSpec((1, 2, tm, FNET_GROUPS * FNET_GC), lambda b, i: (b, 0, i, 0)),
        compiler_params=_cparams("arbitrary", "arbitrary"),
        name="dft_channels",
    )(main, main, w_ch)


def _dft_seq_kernel(l_ref, r_ref, o_ref):
    o_ref[0] = jnp.dot(l_ref[...], r_ref[0], preferred_element_type=f32).astype(o_ref.dtype)


def dft_sequence(cs, ab, b_off, nb_total, prev):
    nb, k, w = ab.shape
    ls = cs.shape[0]
    tm = _tile(ls, 512)
    tn = _tile(w, 512)
    in_specs = [pl.BlockSpec((tm, k), lambda b, j, i: (i, 0)),
                pl.BlockSpec((1, k, tn), lambda b, j, i: (b, 0, j))]
    kern, in_specs, args, aliases = _with_prev(_dft_seq_kernel, in_specs, [cs, ab], prev)
    return pl.pallas_call(
        kern,
        out_shape=jax.ShapeDtypeStruct((nb_total, ls, w), bf16),
        grid=(nb, w // tn, ls // tm),
        in_specs=in_specs,
        out_specs=pl.BlockSpec((1, tm, tn), lambda b, j, i: (b + b_off, i, j)),
        input_output_aliases=aliases,
        compiler_params=_cparams("arbitrary", "arbitrary", "arbitrary"),
        name="dft_sequence",
    )(*args)


def _dft_ch_sym_kernel(xa_ref, xb_ref, ma_ref, mb_ref, ea_ref, eb_ref, j_ref, w_ref, o_ref):
    half = FNET_GROUPS // 2
    tm = xa_ref.shape[1]
    first = lax.broadcasted_iota(jnp.int32, (tm, FNET_GC), 0) == 0
    for g in range(FNET_GROUPS):
        x_ref, m_ref, e_ref = (xa_ref, ma_ref, ea_ref) if g < half else (xb_ref, mb_ref, eb_ref)
        lo = (g % half) * FNET_GC
        x = x_ref[0, :, lo:lo + FNET_GC].astype(f32)
        r = jnp.dot(j_ref[...], m_ref[0, :, lo:lo + FNET_GC], preferred_element_type=f32)
        r = jnp.where(first, e_ref[0, 0:1, lo:lo + FNET_GC].astype(f32), r)
        cols = slice(g * FNET_GC, (g + 1) * FNET_GC)
        o_ref[0, 0, :, cols] = jnp.dot((x + r).astype(bf16), w_ref[:, :FNET_GC],
                                       preferred_element_type=f32).astype(o_ref.dtype)
        o_ref[0, 1, :, cols] = jnp.dot((x - r).astype(bf16), w_ref[:, FNET_GC:],
                                       preferred_element_type=f32).astype(o_ref.dtype)


def dft_channels_sym(main, w_ch, nb, ls, b_off):
    lh = ls // 2
    tm = _tile(lh, 512)
    nt = ls // tm
    wb = FNET_GROUPS * FNET_GC // 2
    f_blk = OFF_F // wb
    jj = jnp.arange(tm, dtype=jnp.int32)
    jmat = ((jj[:, None] + jj[None, :]) == tm).astype(bf16)
    eb = tm // 8
    x_spec = lambda c: pl.BlockSpec((1, tm, wb), lambda b, i: (b + b_off, i, f_blk + c))
    m_spec = lambda c: pl.BlockSpec((1, tm, wb), lambda b, i: (b + b_off, nt - 1 - i, f_blk + c))
    e_spec = lambda c: pl.BlockSpec((1, 8, wb), lambda b, i: (b + b_off, ((nt - i) % nt) * eb, f_blk + c))
    return pl.pallas_call(
        _dft_ch_sym_kernel,
        out_shape=jax.ShapeDtypeStruct((nb, 2, lh, FNET_GROUPS * FNET_GC), bf16),
        grid=(nb, lh // tm),
        in_specs=[x_spec(0), x_spec(1), m_spec(0), m_spec(1), e_spec(0), e_spec(1),
                  pl.BlockSpec((tm, tm), lambda b, i: (0, 0)),
                  pl.BlockSpec((FNET_GC, 2 * FNET_GC), lambda b, i: (0, 0))],
        out_specs=pl.BlockSpec((1, 2, tm, FNET_GROUPS * FNET_GC), lambda b, i: (b, 0, i, 0)),
        compiler_params=_cparams("arbitrary", "arbitrary"),
        name="dft_channels_sym",
    )(main, main, main, main, main, main, jmat, w_ch)


def _dft_seq_sym_kernel(l_ref, r_ref, a_ref, o_ref, *, mid_scale):
    y = jnp.dot(l_ref[...], r_ref[0], preferred_element_type=f32)
    rows = lax.broadcasted_iota(jnp.int32, y.shape, 0)
    sign = jnp.where(rows % 2 == 0, mid_scale, -mid_scale)
    o_ref[0] = (y + sign * a_ref[0, 0:1, :]).astype(o_ref.dtype)


def dft_sequence_sym(cs, ab, a_mid, b_off, nb_total, prev):
    nb, k, w = ab.shape
    ls = cs.shape[0]
    tm = _tile(ls, 512)
    tn = _tile(w, 512)
    in_specs = [pl.BlockSpec((tm, k), lambda b, j, i: (i, 0)),
                pl.BlockSpec((1, k, tn), lambda b, j, i: (b, 0, j)),
                pl.BlockSpec((1, 8, tn), lambda b, j, i: (b, 0, j))]
    kern, in_specs, args, aliases = _with_prev(functools.partial(_dft_seq_sym_kernel, mid_scale=ls ** -0.5),
                                               in_specs, [cs, ab, a_mid], prev)
    return pl.pallas_call(
        kern,
        out_shape=jax.ShapeDtypeStruct((nb_total, ls, w), bf16),
        grid=(nb, w // tn, ls // tm),
        in_specs=in_specs,
        out_specs=pl.BlockSpec((1, tm, tn), lambda b, j, i: (b + b_off, i, j)),
        input_output_aliases=aliases,
        compiler_params=_cparams("arbitrary", "arbitrary", "arbitrary"),
        name="dft_sequence_sym",
    )(*args)


def _dft_mats(n):
    idx = jnp.arange(n, dtype=jnp.int32)
    ang = ((idx[:, None] * idx[None, :]) % n).astype(f32) * (2.0 * math.pi / n)
    s = n ** -0.5
    return jnp.cos(ang) * s, jnp.sin(ang) * s


def _log_sigmoid(x):
    return jnp.minimum(x, 0.0) - jnp.log1p(jnp.exp(-jnp.abs(x)))


def _ret_kernel(*refs, nc, use_s0):
    if use_s0:
        dl_ref, q_ref, k_ref, v_ref, g_ref, gn_ref, s0_ref, o_ref, sfin_ref, of_ref, ob_ref = refs
    else:
        dl_ref, q_ref, k_ref, v_ref, g_ref, gn_ref, o_ref, sfin_ref, of_ref, ob_ref = refs
        s0_ref = None
    C = RET_CHUNK
    lgf = _log_sigmoid(dl_ref[0, 0])[0:1, :]
    lgb = _log_sigmoid(dl_ref[1, 0])[0:1, :]
    lgf2 = jnp.concatenate([lgf, lgf], axis=1)
    lgb2 = jnp.concatenate([lgb, lgb], axis=1)
    lgf_c = jnp.concatenate([lgf] * (C // LANES), axis=1)
    lgb_c = jnp.concatenate([lgb] * (C // LANES), axis=1)
    rows = lax.broadcasted_iota(jnp.int32, (C, C), 0)
    cols = lax.broadcasted_iota(jnp.int32, (C, C), 1)
    rel = (rows - cols).astype(f32)
    d_f = jnp.where(rel >= 0, jnp.exp(jnp.maximum(rel, 0.0) * lgf_c), 0.0)
    d_b = jnp.where(rel <= 0, jnp.exp(jnp.maximum(-rel, 0.0) * lgb_c), 0.0)
    r_v = lax.broadcasted_iota(jnp.int32, (C, RET_DV), 0).astype(f32)
    r_k = lax.broadcasted_iota(jnp.int32, (C, RET_DK), 0).astype(f32)
    qdec_f = jnp.exp((r_v + 1.0) * lgf2)
    qdec_b = jnp.exp((C - r_v) * lgb2)
    kdec_f = jnp.exp((C - 1.0 - r_k) * lgf)
    kdec_b = jnp.exp(r_k * lgb)
    cdec_f = jnp.exp(C * lgf2)
    cdec_b = jnp.exp(C * lgb2)

    def chunk(c, s, dmat, qdec, kdec, cdec):
        r = pl.multiple_of(c * C, C)
        q = q_ref[0, pl.ds(r, C), :]
        k = k_ref[0, pl.ds(r, C), :]
        v = v_ref[0, pl.ds(r, C), :]
        inner = lax.dot_general(q, k, (((1,), (1,)), ((), ())), preferred_element_type=f32) * dmat
        o = (jnp.dot(inner.astype(bf16), v, preferred_element_type=f32)
             + jnp.dot(q, s.astype(bf16), preferred_element_type=f32) * qdec)
        kd = (k.astype(f32) * kdec).T.astype(bf16)
        s_new = s * cdec + jnp.dot(kd, v, preferred_element_type=f32)
        return r, o, s_new

    def body(c, carry):
        sf, sb = carry
        r, o, sf = chunk(c, sf, d_f, qdec_f, kdec_f, cdec_f)
        of_ref[pl.ds(r, C), :] = o
        r, o, sb = chunk(nc - 1 - c, sb, d_b, qdec_b, kdec_b, cdec_b)
        ob_ref[pl.ds(r, C), :] = o
        return sf, sb

    if use_s0:
        init = (s0_ref[0, 0, 0], s0_ref[0, 1, 0])
    else:
        init = (jnp.zeros((RET_DK, RET_DV), f32), jnp.zeros((RET_DK, RET_DV), f32))
    sf, sb = lax.fori_loop(0, nc, body, init, unroll=2)
    sfin_ref[0, 0, 0] = sf
    sfin_ref[0, 1, 0] = sb

    gn = gn_ref[...]

    def norm_body(c, carry):
        r = pl.multiple_of(c * C, C)
        o = of_ref[pl.ds(r, C), :] + ob_ref[pl.ds(r, C), :]
        mu = jnp.mean(o, axis=-1, keepdims=True)
        d = o - mu
        y = d * lax.rsqrt(jnp.mean(d * d, axis=-1, keepdims=True) + GN_EPS) * gn
        o_ref[0, pl.ds(r, C), :] = (y * g_ref[0, pl.ds(r, C), :].astype(f32)).astype(o_ref.dtype)
        return carry

    lax.fori_loop(0, nc, norm_body, 0, unroll=2)


def retention(main, decay_logit, gn_g, s0, nb, ls, b_off, nb_total, prev):
    use_s0 = s0 is not None
    nc = ls // RET_CHUNK
    dl = jnp.broadcast_to(decay_logit.astype(f32)[:, :, None, None], (2, RET_HEADS, 8, LANES))
    in_specs = [pl.BlockSpec((2, 1, 8, LANES), lambda b, h: (0, h, 0, 0)),
                pl.BlockSpec((1, ls, RET_DK), lambda b, h: (b + b_off, 0, OFF_RQ // RET_DK + h)),
                pl.BlockSpec((1, ls, RET_DK), lambda b, h: (b + b_off, 0, OFF_RK // RET_DK + h)),
                pl.BlockSpec((1, ls, RET_DV), lambda b, h: (b + b_off, 0, OFF_RV // RET_DV + h)),
                pl.BlockSpec((1, ls, RET_DV), lambda b, h: (b + b_off, 0, OFF_RG // RET_DV + h)),
                pl.BlockSpec((1, RET_DV), lambda b, h: (0, h))]
    args = [dl, main, main, main, main, gn_g.reshape(1, RET_HEADS * RET_DV)]
    if use_s0:
        in_specs.append(pl.BlockSpec((1, 2, 1, RET_DK, RET_DV), lambda b, h: (b, 0, h, 0, 0)))
        args.append(s0)
    kern, in_specs, args, aliases = _with_prev(functools.partial(_ret_kernel, nc=nc, use_s0=use_s0),
                                               in_specs, args, prev)
    return pl.pallas_call(
        kern,
        out_shape=(jax.ShapeDtypeStruct((nb_total, ls, RET_HEADS * RET_DV), bf16),
                   jax.ShapeDtypeStruct((nb, 2, RET_HEADS, RET_DK, RET_DV), f32)),
        grid=(nb, RET_HEADS),
        in_specs=in_specs,
        out_specs=(pl.BlockSpec((1, ls, RET_DV), lambda b, h: (b + b_off, 0, h)),
                   pl.BlockSpec((1, 2, 1, RET_DK, RET_DV), lambda b, h: (b, 0, h, 0, 0))),
        scratch_shapes=[pltpu.VMEM((ls, RET_DV), f32), pltpu.VMEM((ls, RET_DV), f32)],
        input_output_aliases=aliases,
        compiler_params=_cparams("arbitrary", "arbitrary"),
        name="retention",
    )(*args)


def _merge_kernel(a_ref, f_ref, r_ref, wa_ref, wf_ref, wr_ref, *rest):
    gates, o_ref = rest[:6], rest[6]
    hw = o_ref.shape[1] // 2
    for c in range(2):
        cols = slice(c * hw, (c + 1) * hw)
        m = gates[c][...].astype(f32) * jnp.dot(a_ref[...], wa_ref[:, cols], preferred_element_type=f32)
        m += gates[2 + c][...].astype(f32) * jnp.dot(f_ref[...], wf_ref[:, cols], preferred_element_type=f32)
        m += gates[4 + c][...].astype(f32) * jnp.dot(r_ref[...], wr_ref[:, cols], preferred_element_type=f32)
        o_ref[:, cols] = m.astype(o_ref.dtype)


def merge_branches(att, four, ret, wa, wf, wr, main2d):
    m, d = att.shape
    tm = _tile(m, 256)
    hw = d // 2
    gb = OFF_GATES // hw
    x_spec = pl.BlockSpec((tm, d), lambda i: (i, 0))
    w_spec = pl.BlockSpec((d, d), lambda i: (0, 0), pipeline_mode=pl.Buffered(1))
    g_specs = [pl.BlockSpec((tm, hw), functools.partial(lambda i, k: (i, gb + k), k=k)) for k in range(6)]
    return pl.pallas_call(
        _merge_kernel,
        out_shape=jax.ShapeDtypeStruct((m, d), bf16),
        grid=(m // tm,),
        in_specs=[x_spec, x_spec, x_spec, w_spec, w_spec, w_spec] + g_specs,
        out_specs=pl.BlockSpec((tm, d), lambda i: (i, 0)),
        compiler_params=_cparams("arbitrary"),
        name="merge_branches",
    )(att, four, ret, wa, wf, wr, *([main2d] * 6))


def _proj_res_kernel(m_ref, w_ref, x_ref, g_ref, o_ref):
    y = jnp.dot(m_ref[0], w_ref[...], preferred_element_type=f32)
    o_ref[0] = x_ref[0] + g_ref[0] * y


def proj_residual(merged, w, x, gate):
    nb, lr, d = x.shape
    tm = _tile(lr, 1024)
    tn = 1024
    return pl.pallas_call(
        _proj_res_kernel,
        out_shape=jax.ShapeDtypeStruct((nb, lr, d), f32),
        grid=(nb, lr // tm, d // tn),
        in_specs=[pl.BlockSpec((1, tm, d), lambda b, i, j: (b, i, 0)),
                  pl.BlockSpec((d, tn), lambda b, i, j: (0, j)),
                  pl.BlockSpec((1, tm, tn), lambda b, i, j: (b, i, j)),
                  pl.BlockSpec((1, 1, tn), lambda b, i, j: (b, 0, j))],
        out_specs=pl.BlockSpec((1, tm, tn), lambda b, i, j: (b, i, j)),
        compiler_params=_cparams("arbitrary", "arbitrary", "arbitrary"),
        name="proj_residual",
    )(merged, w, x, gate)


def _ffn_kernel(h_ref, wg_ref, wu_ref, wd_ref, x_ref, g_ref, o_ref, acc_ref):
    j = pl.program_id(2)

    @pl.when(j == 0)
    def _():
        acc_ref[...] = jnp.zeros_like(acc_ref)

    h = h_ref[0]
    a = jnp.dot(h, wg_ref[...], preferred_element_type=f32)
    u = jnp.dot(h, wu_ref[...], preferred_element_type=f32)
    t = (_silu(a) * u).astype(bf16)
    acc_ref[...] += jnp.dot(t, wd_ref[...], preferred_element_type=f32)

    @pl.when(j == pl.num_programs(2) - 1)
    def _():
        o_ref[0] = x_ref[0] + g_ref[0] * acc_ref[...]


def ffn_dense(h, wg, wu, wd, x, gate):
    nb, lr, d = x.shape
    dff = wg.shape[1]
    tm = _tile(lr, 512)
    tf = _tile(dff, 512)
    return pl.pallas_call(
        _ffn_kernel,
        out_shape=jax.ShapeDtypeStruct((nb, lr, d), f32),
        grid=(nb, lr // tm, dff // tf),
        in_specs=[pl.BlockSpec((1, tm, d), lambda b, i, j: (b, i, 0)),
                  pl.BlockSpec((d, tf), lambda b, i, j: (0, j)),
                  pl.BlockSpec((d, tf), lambda b, i, j: (0, j)),
                  pl.BlockSpec((tf, d), lambda b, i, j: (j, 0)),
                  pl.BlockSpec((1, tm, d), lambda b, i, j: (b, i, 0)),
                  pl.BlockSpec((1, 1, d), lambda b, i, j: (b, 0, 0))],
        out_specs=pl.BlockSpec((1, tm, d), lambda b, i, j: (b, i, 0)),
        scratch_shapes=[pltpu.VMEM((tm, d), f32)],
        compiler_params=_cparams("arbitrary", "arbitrary", "arbitrary"),
        name="ffn_dense",
    )(h, wg, wu, wd, x, gate)


MOE_TM = 512
MOE_TT = 256


def _router_kernel(h_ref, w_ref, e_ref, p_ref):
    logits = jnp.dot(h_ref[...].astype(bf16), w_ref[...], preferred_element_type=f32)
    lane = lax.broadcasted_iota(jnp.int32, logits.shape, 1)
    lane_f = lane.astype(f32)
    neg = jnp.full_like(logits, -jnp.inf)
    l0 = jnp.where(lane < N_EXPERTS, logits, neg)
    v1 = jnp.max(l0, axis=1, keepdims=True)
    i1 = jnp.min(jnp.where(l0 == v1, lane_f, float(LANES)), axis=1, keepdims=True)
    l1 = jnp.where(lane_f == i1, neg, l0)
    v2 = jnp.max(l1, axis=1, keepdims=True)
    i2 = jnp.min(jnp.where(l1 == v2, lane_f, float(LANES)), axis=1, keepdims=True)
    e = jnp.exp(v2 - v1)
    den = 1.0 + e
    e_ref[...] = jnp.where(lane == 0, i1, jnp.where(lane == 1, i2, 0.0)).astype(jnp.int32)
    p_ref[...] = jnp.where(lane == 0, 1.0 / den, jnp.where(lane == 1, e / den, 0.0))


def moe_router(h2d, w_router_pad):
    n, d = h2d.shape
    tm = _tile(n, 512)
    return pl.pallas_call(
        _router_kernel,
        out_shape=(jax.ShapeDtypeStruct((n, LANES), jnp.int32), jax.ShapeDtypeStruct((n, LANES), f32)),
        grid=(n // tm,),
        in_specs=[pl.BlockSpec((tm, d), lambda i: (i, 0)),
                  pl.BlockSpec((d, LANES), lambda i: (0, 0))],
        out_specs=(pl.BlockSpec((tm, LANES), lambda i: (i, 0)), pl.BlockSpec((tm, LANES), lambda i: (i, 0))),
        compiler_params=_cparams("arbitrary"),
        name="moe_router",
    )(h2d, w_router_pad)


def _dispatch_kernel(pos_ref, h_ref, xs_in_ref, xs_ref, sem):
    del xs_in_ref
    tt = h_ref.shape[0]

    def row_copy(t, p):
        return pltpu.make_async_copy(h_ref.at[pl.ds(t, 1), :], xs_ref.at[pl.ds(p, 1), :], sem)

    def start(t, c):
        row_copy(t, pos_ref[0, 0, 2 * t]).start()
        row_copy(t, pos_ref[0, 0, 2 * t + 1]).start()
        return c

    lax.fori_loop(0, tt, start, 0, unroll=8)
    tile_copy = pltpu.make_async_copy(h_ref, xs_ref.at[pl.ds(0, tt), :], sem)
    tile_copy.wait()
    tile_copy.wait()


def moe_dispatch(h2d, pos, p_rows):
    n, d = h2d.shape
    tt = _tile(n, MOE_TT)
    pos3 = pos.reshape(n // tt, 1, 2 * tt)
    xs0 = jnp.zeros((p_rows, d), h2d.dtype)
    return pl.pallas_call(
        _dispatch_kernel,
        out_shape=jax.ShapeDtypeStruct((p_rows, d), h2d.dtype),
        grid=(n // tt,),
        in_specs=[pl.BlockSpec((1, 1, 2 * tt), lambda i: (i, 0, 0), memory_space=pltpu.SMEM),
                  pl.BlockSpec((tt, d), lambda i: (i, 0)),
                  pl.BlockSpec(memory_space=pl.ANY)],
        out_specs=pl.BlockSpec(memory_space=pl.ANY),
        scratch_shapes=[pltpu.SemaphoreType.DMA(())],
        input_output_aliases={2: 0},
        compiler_params=_cparams("arbitrary"),
        name="moe_dispatch",
    )(pos3, h2d, xs0)


def _moe_ffn_kernel(be_ref, nu_ref, x_ref, wg_ref, wu_ref, wd_ref, y_ref, xb_ref, acc_ref):
    del be_ref
    i = pl.program_id(0)
    j = pl.program_id(1)
    active = i < nu_ref[0]

    @pl.when(jnp.logical_and(active, j == 0))
    def _():
        xb_ref[...] = x_ref[...].astype(bf16)
        acc_ref[...] = jnp.zeros_like(acc_ref)

    @pl.when(active)
    def _():
        xb = xb_ref[...]
        a = jnp.dot(xb, wg_ref[0], preferred_element_type=f32)
        u = jnp.dot(xb, wu_ref[0], preferred_element_type=f32)
        t = (_silu(a) * u).astype(bf16)
        acc_ref[...] += jnp.dot(t, wd_ref[0], preferred_element_type=f32)

    @pl.when(j == pl.num_programs(1) - 1)
    def _():
        y_ref[...] = jnp.where(active, acc_ref[...], 0.0)


def moe_ffn(xs, blk_expert, n_used, wg, wu, wd):
    p_rows, d = xs.shape
    dff = wg.shape[2]
    tm = MOE_TM
    tf = _tile(dff, 512)
    nf = dff // tf

    def jj(i, j, nu):
        return jnp.where(i < nu[0], j, nf - 1)

    grid_spec = pltpu.PrefetchScalarGridSpec(
        num_scalar_prefetch=2,
        grid=(p_rows // tm, nf),
        in_specs=[pl.BlockSpec((tm, d), lambda i, j, be, nu: (i, 0)),
                  pl.BlockSpec((1, d, tf), lambda i, j, be, nu: (be[i], 0, jj(i, j, nu))),
                  pl.BlockSpec((1, d, tf), lambda i, j, be, nu: (be[i], 0, jj(i, j, nu))),
                  pl.BlockSpec((1, tf, d), lambda i, j, be, nu: (be[i], jj(i, j, nu), 0))],
        out_specs=pl.BlockSpec((tm, d), lambda i, j, be, nu: (i, 0)),
        scratch_shapes=[pltpu.VMEM((tm, d), bf16), pltpu.VMEM((tm, d), f32)],
    )
    return pl.pallas_call(
        _moe_ffn_kernel,
        out_shape=jax.ShapeDtypeStruct((p_rows, d), f32),
        grid_spec=grid_spec,
        compiler_params=_cparams("arbitrary", "arbitrary"),
        name="moe_ffn",
    )(blk_expert, n_used, xs, wg, wu, wd)


def _combine_kernel(pos_ref, y_ref, p_ref, x_ref, g_ref, o_ref, ybuf_ref, sem):
    tt = x_ref.shape[1]

    def row_copy(p, dst):
        return pltpu.make_async_copy(y_ref.at[pl.ds(p, 1), :], ybuf_ref.at[pl.ds(dst, 1), :], sem)

    def start(t, c):
        row_copy(pos_ref[0, 0, 2 * t], t).start()
        row_copy(pos_ref[0, 0, 2 * t + 1], tt + t).start()
        return c

    lax.fori_loop(0, tt, start, 0, unroll=8)
    pltpu.make_async_copy(y_ref.at[pl.ds(0, 2 * tt), :], ybuf_ref, sem).wait()
    p = p_ref[0]
    mix = p[:, 0:1] * ybuf_ref[pl.ds(0, tt), :] + p[:, 1:2] * ybuf_ref[pl.ds(tt, tt), :]
    o_ref[0] = x_ref[0] + g_ref[0] * mix


def moe_combine(y, pos, probs, x, gate):
    nb, lr, d = x.shape
    tt = _tile(lr, MOE_TT)
    nt = lr // tt
    pos3 = pos.reshape(nb * nt, 1, 2 * tt)
    probs3 = probs.reshape(nb, lr, LANES)
    return pl.pallas_call(
        _combine_kernel,
        out_shape=jax.ShapeDtypeStruct((nb, lr, d), f32),
        grid=(nb, nt),
        in_specs=[pl.BlockSpec((1, 1, 2 * tt), lambda b, i: (b * nt + i, 0, 0), memory_space=pltpu.SMEM),
                  pl.BlockSpec(memory_space=pl.ANY),
                  pl.BlockSpec((1, tt, LANES), lambda b, i: (b, i, 0)),
                  pl.BlockSpec((1, tt, d), lambda b, i: (b, i, 0)),
                  pl.BlockSpec((1, 1, d), lambda b, i: (b, 0, 0))],
        out_specs=pl.BlockSpec((1, tt, d), lambda b, i: (b, i, 0)),
        scratch_shapes=[pltpu.VMEM((2 * tt, d), f32), pltpu.SemaphoreType.DMA(())],
        compiler_params=_cparams("arbitrary", "arbitrary"),
        name="moe_combine",
    )(pos3, y, probs3, x, gate)


def moe_layer(h, x, gate, w_router, wg, wu, wd):
    nb, lr, d = h.shape
    n = nb * lr
    h2d = h.reshape(n, d)
    w_r = jnp.zeros((d, LANES), bf16).at[:, :N_EXPERTS].set(w_router.astype(bf16))
    e_out, p_out = moe_router(h2d, w_r)
    e_flat = e_out[:, :2].reshape(-1)
    onehot = (e_flat[:, None] == jnp.arange(N_EXPERTS, dtype=jnp.int32)[None, :]).astype(jnp.int32)
    csum = jnp.cumsum(onehot, axis=0)
    counts = csum[-1]
    rank = jnp.sum(onehot * csum, axis=1) - 1
    padded = (counts + MOE_TM - 1) // MOE_TM * MOE_TM
    pad_ends = jnp.cumsum(padded)
    pad_starts = pad_ends - padded
    pos = jnp.sum(onehot * pad_starts[None, :], axis=1) + rank
    n_blocks = (2 * n) // MOE_TM + N_EXPERTS
    blk_start = jnp.arange(n_blocks, dtype=jnp.int32) * MOE_TM
    blk_expert = jnp.minimum(jnp.sum((blk_start[:, None] >= pad_ends[None, :]).astype(jnp.int32), axis=1),
                             N_EXPERTS - 1).astype(jnp.int32)
    n_used = (pad_ends[-1] // MOE_TM).astype(jnp.int32).reshape(1)
    xs = moe_dispatch(h2d, pos.astype(jnp.int32), n_blocks * MOE_TM)
    y = moe_ffn(xs, blk_expert, n_used, wg, wu, wd)
    return moe_combine(y, pos.astype(jnp.int32), p_out, x, gate)


def _rope_tables(n_tok):
    rows = n_tok // GRID_W
    r = jnp.repeat(jnp.arange(rows, dtype=f32), GRID_W)
    cl = jnp.tile(jnp.arange(GRID_W, dtype=f32), rows)
    n_freq = MLA_ROPE // 4
    inv = ROPE_BASE ** (-jnp.arange(n_freq, dtype=f32) / n_freq)
    ang = jnp.concatenate([r[:, None] * inv, cl[:, None] * inv], axis=-1)
    cos, sin = jnp.cos(ang), jnp.sin(ang)
    return jnp.tile(cos, (1, 4)), jnp.concatenate([-sin, sin, -sin, sin], axis=-1)


def _prep_in_weights(w_in_l):
    d = w_in_l.shape[0]
    nq = MLA_HEADS * (MLA_NOPE + MLA_ROPE)
    wq = w_in_l[:, :nq].reshape(d, MLA_HEADS, MLA_NOPE + MLA_ROPE)
    w_ckv = w_in_l[:, nq:nq + KV_LORA]
    w_kr = w_in_l[:, nq + KV_LORA:nq + KV_LORA + MLA_ROPE]
    rest = w_in_l[:, nq + KV_LORA + MLA_ROPE:]
    w_main = jnp.concatenate([wq[:, :, :MLA_NOPE].reshape(d, Q_NOPE_W), wq[:, :, MLA_NOPE:].reshape(d, Q_ROPE_W),
                              rest], axis=1).astype(bf16)
    w_ckr = jnp.concatenate([w_ckv, w_kr, w_kr], axis=1).astype(bf16)
    return w_main, w_ckr


def kernel(x_prompt, x_sample, c, cache_ckv, cache_krope, state_ret, c_ctx, w_ada, b_ada, norm1_g, w_in, kv_norm_g, w_ukv, ret_decay_logit, ret_gn_g, w_o_mla, w_o_fnet, w_o_ret, w_out, norm2_g, w_gate_dense, w_up_dense, w_down_dense, w_router, w_gate_moe, w_up_moe, w_down_moe, final_norm_g):
    nbc, seq, d = x_prompt.shape
    nbs, ls, _ = x_sample.shape
    depth = w_in.shape[0]
    lr = nbc * seq
    assert lr == ls, "unified row layout needs BATCH * SEQ == DEC_SEQ"
    nb = 1 + nbs
    n_rows = nb * lr

    x = jnp.concatenate([x_prompt.reshape(1, lr, d), x_sample], axis=0)
    cond = jnp.concatenate([c_ctx[None, :], c, jnp.zeros((16 - nb, d), f32)], axis=0)
    mod = adaln_all(cond, w_ada, b_ada)[:, :nb]
    cos128, sin128 = _rope_tables(ls)

    w_ch = jnp.concatenate(_dft_mats(FNET_GC), axis=1).astype(bf16)
    c_c, s_c = _dft_mats(seq)
    cs_ctx = jnp.concatenate([c_c, -s_c], axis=1).astype(bf16)
    c_s, s_s = _dft_mats(ls)
    lh = ls // 2
    c_half = c_s[:, :lh] * jnp.where(jnp.arange(lh) == 0, 0.5, 1.0)[None, :]
    cs_lat = jnp.concatenate([c_half, -s_s[:, :lh]], axis=1).astype(bf16)

    ckv_list, krope_list, ret_list = [], [], []
    spare = []
    for l in range(depth):
        sh1, sc1, g1, sh2, sc2, g2 = [m.reshape(nb, 1, d) for m in jnp.split(mod[l], 6, axis=-1)]
        w_main, w_ckr = _prep_in_weights(w_in[l])
        h = norm_mod(x, norm1_g[l], sc1, sh1, bf16)
        main = inproj_main(h, w_main, cos128, sin128)
        ckv, kr, krr = inproj_ckr(h, w_ckr, kv_norm_g[l], cos128, sin128)
        w_ukv_b = w_ukv[l].astype(bf16)

        ckv_c = ckv[0].reshape(nbc, seq, KV_LORA)
        kr_c = kr[0].reshape(nbc, seq, LANES)
        kv_c = matmul(ckv_c.reshape(lr, KV_LORA).astype(bf16), w_ukv_b, bf16).reshape(nbc, seq, -1)
        main_c = main.reshape(nb * nbc, seq, N_MAIN)
        bufs = [h] + spare
        bufs += [jnp.zeros((nb, lr, d), bf16) for _ in range(3 - len(bufs))]
        bufs = [bb.reshape(nb * nbc, seq, d) for bb in bufs[:3]]
        att_c = attention(main_c, kv_c, kr_c.astype(bf16), nbc, seq, 0, nb * nbc, bufs[0])
        ab_c = dft_channels(main_c, w_ch, nbc, seq, 0).reshape(nbc, 2 * seq, FNET_GROUPS * FNET_GC)
        four_c = dft_sequence(cs_ctx, ab_c, 0, nb * nbc, bufs[1])
        ret_c, sfin_c = retention(main_c, ret_decay_logit[l], ret_gn_g[l], None, nbc, seq, 0, nb * nbc, bufs[2])
        ckv_list.append(ckv_c)
        krope_list.append(kr_c[:, :, :MLA_ROPE])
        ret_list.append(sfin_c)

        ckv_keys = jnp.concatenate([ckv[1:], cache_ckv[:, l]], axis=1).astype(bf16)
        lk = ckv_keys.shape[1]
        kv_s = matmul(ckv_keys.reshape(nbs * lk, KV_LORA), w_ukv_b, bf16).reshape(nbs, lk, -1)
        ck = cache_krope[:, l].astype(bf16)
        kr_keys = jnp.concatenate([krr[1:], jnp.concatenate([ck, ck], axis=-1)], axis=1)
        att = attention(main, kv_s, kr_keys, nbs, ls, 1, nb, att_c.reshape(nb, ls, -1)).reshape(n_rows, -1)
        ab_s = dft_channels_sym(main, w_ch, nbs, ls, 1).reshape(nbs, ls, FNET_GROUPS * FNET_GC)
        f_mid = main[1:, lh:lh + 8, OFF_F:OFF_F + FNET_GROUPS * FNET_GC]
        a_mid = matmul(f_mid.reshape(nbs * 8 * FNET_GROUPS, FNET_GC), w_ch[:, :FNET_GC], f32)
        a_mid = a_mid.reshape(nbs, 8, FNET_GROUPS * FNET_GC)
        four = dft_sequence_sym(cs_lat, ab_s, a_mid, 1, nb, four_c.reshape(nb, ls, -1)).reshape(n_rows, -1)
        ret, _ = retention(main, ret_decay_logit[l], ret_gn_g[l], state_ret[:, l], nbs, ls, 1, nb,
                           ret_c.reshape(nb, ls, -1))
        ret = ret.reshape(n_rows, -1)
        spare = [att, four]

        merged = merge_branches(att, four, ret, w_o_mla[l].astype(bf16), w_o_fnet[l].astype(bf16),
                                w_o_ret[l].astype(bf16), main.reshape(n_rows, N_MAIN))
        x = proj_residual(merged.reshape(nb, lr, d), w_out[l].astype(bf16), x, g1)

        i = l // 2
        if l % 2 == 0:
            h2 = norm_mod(x, norm2_g[l], sc2, sh2, bf16)
            x = ffn_dense(h2, w_gate_dense[i].astype(bf16), w_up_dense[i].astype(bf16),
                          w_down_dense[i].astype(bf16), x, g2)
        else:
            h2 = norm_mod(x, norm2_g[l], sc2, sh2, f32)
            x = moe_layer(h2, x, g2, w_router[i], w_gate_moe[i].astype(bf16), w_up_moe[i].astype(bf16),
                          w_down_moe[i].astype(bf16))

    y_prompt = final_norm(x, final_norm_g, 0, 1).reshape(nbc, seq, d)
    y_sample = final_norm(x, final_norm_g, 1, nbs)
    new_ckv = jnp.stack(ckv_list, axis=1)
    new_krope = jnp.stack(krope_list, axis=1)
    new_ret = jnp.stack(ret_list, axis=1)
    return (y_prompt, y_sample, new_ckv, new_krope, new_ret)
```

```python
import functools
import math

import jax
import jax.numpy as jnp
from jax import lax
from jax.experimental import pallas as pl
from jax.experimental.pallas import tpu as pltpu

f32 = jnp.float32
bf16 = jnp.bfloat16

D_MODEL = 2048
GRID_W = 64
MLA_HEADS = 16
MLA_NOPE = 128
MLA_ROPE = 64
MLA_V = 128
KV_LORA = 512
ROPE_BASE = 10000.0
FNET_GROUPS = 4
FNET_GC = 512
RET_HEADS = 8
RET_DK = 128
RET_DV = 256
RET_CHUNK = 256
N_EXPERTS = 8
EPS = 1e-6
GN_EPS = 1e-5

LANES = 128
VMEM_LIMIT_BYTES = 56 * 1024 * 1024

Q_NOPE_W = MLA_HEADS * MLA_NOPE
Q_ROPE_W = MLA_HEADS * MLA_ROPE
OFF_Q = 0
OFF_QROPE = Q_NOPE_W
OFF_F = OFF_Q + Q_NOPE_W + Q_ROPE_W
OFF_RQ = OFF_F + FNET_GROUPS * FNET_GC
OFF_RK = OFF_RQ + RET_HEADS * RET_DK
OFF_RV = OFF_RK + RET_HEADS * RET_DK
OFF_RG = OFF_RV + RET_HEADS * RET_DV
OFF_GATES = OFF_RG + RET_HEADS * RET_DV
N_MAIN = OFF_GATES + 3 * D_MODEL
CKR_W = KV_LORA + LANES
Q_SCALE = (MLA_NOPE + MLA_ROPE) ** -0.5 * math.log2(math.e)


def _cparams(*sem):
    return pltpu.CompilerParams(dimension_semantics=sem, vmem_limit_bytes=VMEM_LIMIT_BYTES)


def _tile(n, pref):
    t = min(n, pref)
    while n % t:
        t -= 8
    return t


def _silu(x):
    return x * jax.nn.sigmoid(x)


def _rope128(x, cos, sin):
    lane = lax.broadcasted_iota(jnp.int32, x.shape, 1)
    first = (lane % MLA_ROPE) < (MLA_ROPE // 2)
    swapped = jnp.where(first, pltpu.roll(x, LANES - MLA_ROPE // 2, 1), pltpu.roll(x, MLA_ROPE // 2, 1))
    return x * cos + swapped * sin


def _adaln_kernel(c_ref, w_ref, b_ref, o_ref):
    s = _silu(c_ref[...]).astype(bf16)
    o_ref[0] = jnp.dot(s, w_ref[0].astype(bf16), preferred_element_type=f32) + b_ref[0]


def adaln_all(cond, w_ada, b_ada):
    depth, d, n = w_ada.shape
    r = cond.shape[0]
    tn = _tile(n, 1024)
    return pl.pallas_call(
        _adaln_kernel,
        out_shape=jax.ShapeDtypeStruct((depth, r, n), f32),
        grid=(depth, n // tn),
        in_specs=[pl.BlockSpec((r, d), lambda l, j: (0, 0)),
                  pl.BlockSpec((1, d, tn), lambda l, j: (l, 0, j)),
                  pl.BlockSpec((1, 1, tn), lambda l, j: (l, 0, j))],
        out_specs=pl.BlockSpec((1, r, tn), lambda l, j: (l, 0, j)),
        compiler_params=_cparams("arbitrary", "arbitrary"),
        name="adaln",
    )(cond, w_ada, b_ada.reshape(depth, 1, n))


def _norm_mod_kernel(x_ref, g_ref, sc_ref, sh_ref, o_ref):
    x = x_ref[0]
    y = x * lax.rsqrt(jnp.mean(x * x, axis=-1, keepdims=True) + EPS)
    y = y * g_ref[...]
    o_ref[0] = (y * (1.0 + sc_ref[0]) + sh_ref[0]).astype(o_ref.dtype)


def norm_mod(x, g, sc, sh, out_dtype):
    nb, lr, d = x.shape
    tm = _tile(lr, 512)
    return pl.pallas_call(
        _norm_mod_kernel,
        out_shape=jax.ShapeDtypeStruct((nb, lr, d), out_dtype),
        grid=(nb, lr // tm),
        in_specs=[pl.BlockSpec((1, tm, d), lambda b, i: (b, i, 0)),
                  pl.BlockSpec((1, d), lambda b, i: (0, 0)),
                  pl.BlockSpec((1, 1, d), lambda b, i: (b, 0, 0)),
                  pl.BlockSpec((1, 1, d), lambda b, i: (b, 0, 0))],
        out_specs=pl.BlockSpec((1, tm, d), lambda b, i: (b, i, 0)),
        compiler_params=_cparams("arbitrary", "arbitrary"),
        name="norm_mod",
    )(x, g.reshape(1, d), sc, sh)


def _final_norm_kernel(x_ref, g_ref, o_ref):
    x = x_ref[0]
    y = x * lax.rsqrt(jnp.mean(x * x, axis=-1, keepdims=True) + EPS)
    o_ref[0] = y * g_ref[...]


def final_norm(x, g, b_off, nb_out):
    _, lr, d = x.shape
    tm = _tile(lr, 512)
    return pl.pallas_call(
        _final_norm_kernel,
        out_shape=jax.ShapeDtypeStruct((nb_out, lr, d), f32),
        grid=(nb_out, lr // tm),
        in_specs=[pl.BlockSpec((1, tm, d), lambda b, i: (b + b_off, i, 0)),
                  pl.BlockSpec((1, d), lambda b, i: (0, 0))],
        out_specs=pl.BlockSpec((1, tm, d), lambda b, i: (b, i, 0)),
        compiler_params=_cparams("arbitrary", "arbitrary"),
        name="final_norm",
    )(x, g.reshape(1, d))


def _inproj_kernel(h_ref, w_ref, cos_ref, sin_ref, o_ref, *, tn, nchunk):
    b = pl.program_id(0)
    j = pl.program_id(2)
    tm = h_ref.shape[1]
    rc = tm // nchunk
    j_rope = OFF_QROPE // tn
    j_rk = OFF_RK // tn
    j_rv = OFF_RV // tn
    j_rg = OFF_RG // tn
    j_gates = OFF_GATES // tn
    is_q = j < OFF_F // tn
    is_rope = jnp.logical_and(jnp.logical_and(j >= j_rope, is_q), b >= 1)
    is_qplain = jnp.logical_and(is_q, jnp.logical_not(is_rope))
    is_rk = jnp.logical_and(j >= j_rk, j < j_rv)
    is_silu = jnp.logical_and(j >= j_rg, j < j_gates)
    is_sig = j >= j_gates
    plain = jnp.logical_not(is_q | is_rk | is_silu | is_sig)

    def run(epilogue):
        pending = None
        for r in range(nchunk):
            rows = slice(r * rc, (r + 1) * rc)
            acc = jnp.dot(h_ref[0, rows, :], w_ref[...], preferred_element_type=f32)
            if pending is not None:
                prow, pacc = pending
                o_ref[0, prow, :] = epilogue(pacc, prow).astype(o_ref.dtype)
            pending = (rows, acc)
        prow, pacc = pending
        o_ref[0, prow, :] = epilogue(pacc, prow).astype(o_ref.dtype)

    @pl.when(plain)
    def _():
        run(lambda acc, rows: acc)

    @pl.when(is_qplain)
    def _():
        run(lambda acc, rows: acc * Q_SCALE)

    @pl.when(is_rope)
    def _():
        def rope(acc, rows):
            cos = cos_ref[rows, :] * Q_SCALE
            sin = sin_ref[rows, :] * Q_SCALE
            return jnp.concatenate([_rope128(acc[:, s * LANES:(s + 1) * LANES], cos, sin)
                                    for s in range(tn // LANES)], axis=1)
        run(rope)

    @pl.when(is_rk)
    def _():
        run(lambda acc, rows: acc * (RET_DK ** -0.5))

    @pl.when(is_silu)
    def _():
        run(lambda acc, rows: _silu(acc))

    @pl.when(is_sig)
    def _():
        run(lambda acc, rows: jax.nn.sigmoid(acc))


def inproj_main(h, w_main, cos128, sin128):
    nb, lr, d = h.shape
    tm = _tile(lr, 1024)
    tn = 1024
    return pl.pallas_call(
        functools.partial(_inproj_kernel, tn=tn, nchunk=4),
        out_shape=jax.ShapeDtypeStruct((nb, lr, N_MAIN), bf16),
        grid=(nb, lr // tm, N_MAIN // tn),
        in_specs=[pl.BlockSpec((1, tm, d), lambda b, i, j: (b, i, 0)),
                  pl.BlockSpec((d, tn), lambda b, i, j: (0, j)),
                  pl.BlockSpec((tm, LANES), lambda b, i, j: (i, 0)),
                  pl.BlockSpec((tm, LANES), lambda b, i, j: (i, 0))],
        out_specs=pl.BlockSpec((1, tm, tn), lambda b, i, j: (b, i, j)),
        compiler_params=_cparams("arbitrary", "arbitrary", "arbitrary"),
        name="inproj_main",
    )(h, w_main, cos128, sin128)


def _ckr_kernel(h_ref, w_ref, g_ref, cos_ref, sin_ref, ckv_ref, kr_ref, krr_ref):
    acc = jnp.dot(h_ref[0], w_ref[...], preferred_element_type=f32)
    c = acc[:, :KV_LORA]
    y = c * lax.rsqrt(jnp.mean(c * c, axis=-1, keepdims=True) + EPS)
    ckv_ref[0] = y * g_ref[...]
    kr = acc[:, KV_LORA:]
    kr_ref[0] = kr
    krr_ref[0] = _rope128(kr, cos_ref[...], sin_ref[...]).astype(krr_ref.dtype)


def inproj_ckr(h, w_ckr, kv_g, cos128, sin128):
    nb, lr, d = h.shape
    tm = _tile(lr, 1024)
    return pl.pallas_call(
        _ckr_kernel,
        out_shape=(jax.ShapeDtypeStruct((nb, lr, KV_LORA), f32),
                   jax.ShapeDtypeStruct((nb, lr, LANES), f32),
                   jax.ShapeDtypeStruct((nb, lr, LANES), bf16)),
        grid=(nb, lr // tm),
        in_specs=[pl.BlockSpec((1, tm, d), lambda b, i: (b, i, 0)),
                  pl.BlockSpec((d, CKR_W), lambda b, i: (0, 0)),
                  pl.BlockSpec((1, KV_LORA), lambda b, i: (0, 0)),
                  pl.BlockSpec((tm, LANES), lambda b, i: (i, 0)),
                  pl.BlockSpec((tm, LANES), lambda b, i: (i, 0))],
        out_specs=(pl.BlockSpec((1, tm, KV_LORA), lambda b, i: (b, i, 0)),
                   pl.BlockSpec((1, tm, LANES), lambda b, i: (b, i, 0)),
                   pl.BlockSpec((1, tm, LANES), lambda b, i: (b, i, 0))),
        compiler_params=_cparams("arbitrary", "arbitrary"),
        name="inproj_ckr",
    )(h, w_ckr, kv_g.reshape(1, KV_LORA), cos128, sin128)


def _mm_kernel(x_ref, w_ref, o_ref):
    o_ref[...] = jnp.dot(x_ref[...], w_ref[...], preferred_element_type=f32).astype(o_ref.dtype)


def matmul(x, w, out_dtype, tm_pref=512, tn_pref=4096):
    m, k = x.shape
    _, n = w.shape
    tm = _tile(m, tm_pref)
    tn = _tile(n, tn_pref)
    return pl.pallas_call(
        _mm_kernel,
        out_shape=jax.ShapeDtypeStruct((m, n), out_dtype),
        grid=(m // tm, n // tn),
        in_specs=[pl.BlockSpec((tm, k), lambda i, j: (i, 0)),
                  pl.BlockSpec((k, tn), lambda i, j: (0, j))],
        out_specs=pl.BlockSpec((tm, tn), lambda i, j: (i, j)),
        compiler_params=_cparams("arbitrary", "arbitrary"),
        name="matmul",
    )(x, w)


def _attn_single_kernel(qn_ref, qr_ref, kn_ref, v_ref, kr_ref, o_ref, kcat_ref, vt_ref, s_ref, *, nk, tk, nparts):
    h = pl.program_id(1)
    qi = pl.program_id(2)
    lk = kcat_ref.shape[0]

    @pl.when(qi == 0)
    def _():
        kcat_ref[:, :MLA_NOPE] = kn_ref[0]
        kcat_ref[:, MLA_NOPE:] = kr_ref[0]
        for c in range(nk):
            vt_ref[:, c * tk:(c + 1) * tk] = v_ref[0, c * tk:(c + 1) * tk, :].astype(f32).T.astype(bf16)

    qr = qr_ref[0]
    lane = lax.broadcasted_iota(jnp.int32, qr.shape, 1)
    qr = jnp.where((lane // MLA_ROPE) == (h % 2), qr, jnp.zeros_like(qr))
    qcat = jnp.concatenate([qn_ref[0], qr], axis=1)
    tq = qcat.shape[0]

    bounds = [((nk * i) // nparts) * tk for i in range(nparts + 1)]
    m = jnp.full((1, tq), -jnp.inf, f32)
    pending = None
    for i in range(nparts):
        r0, r1 = bounds[i], bounds[i + 1]
        s = lax.dot_general(kcat_ref[r0:r1, :], qcat, (((1,), (1,)), ((), ())), preferred_element_type=f32)
        s_ref[r0:r1, :] = s
        if pending is not None:
            m = jnp.maximum(m, jnp.max(pending, axis=0, keepdims=True))
        pending = s
    m = jnp.maximum(m, jnp.max(pending, axis=0, keepdims=True))

    l = jnp.zeros((1, tq), f32)
    acc = jnp.zeros((MLA_V, tq), f32)
    for c in range(nk):
        p = jnp.exp2(s_ref[c * tk:(c + 1) * tk, :] - m)
        l = l + jnp.sum(p, axis=0, keepdims=True)
        acc = acc + jnp.dot(vt_ref[:, c * tk:(c + 1) * tk], p.astype(bf16), preferred_element_type=f32)
    o_ref[0] = (acc * (1.0 / l)).T.astype(o_ref.dtype)


def _attn_kernel(qn_ref, qr_ref, kn_ref, v_ref, kr_ref, o_ref, kcat_ref, vt_ref, sa_ref, sb_ref, ma_ref, mb_ref,
                 *, nk, tk, nparts):
    h = pl.program_id(1)
    i = pl.program_id(2)

    @pl.when(i == 0)
    def _():
        kcat_ref[:, :MLA_NOPE] = kn_ref[0]
        kcat_ref[:, MLA_NOPE:] = kr_ref[0]
        for c in range(nk):
            vt_ref[:, c * tk:(c + 1) * tk] = v_ref[0, c * tk:(c + 1) * tk, :].astype(f32).T.astype(bf16)
        sb_ref[...] = jnp.zeros(sb_ref.shape, f32)
        mb_ref[...] = jnp.zeros(mb_ref.shape, f32)

    def step(s_new, s_old, m_new_ref, m_old_ref):
        qr = qr_ref[0]
        lane = lax.broadcasted_iota(jnp.int32, qr.shape, 1)
        qr = jnp.where((lane // MLA_ROPE) == (h % 2), qr, jnp.zeros_like(qr))
        qcat = jnp.concatenate([qn_ref[0], qr], axis=1)
        tq = qcat.shape[0]
        m_old = m_old_ref[0:1, :]
        bounds = [(nk * g) // nparts for g in range(nparts + 1)]
        m = jnp.full((1, tq), -jnp.inf, f32)
        l = jnp.zeros((1, tq), f32)
        acc = jnp.zeros((MLA_V, tq), f32)
        for g in range(nparts):
            r0, r1 = bounds[g] * tk, bounds[g + 1] * tk
            s = lax.dot_general(kcat_ref[r0:r1, :], qcat, (((1,), (1,)), ((), ())), preferred_element_type=f32)
            s_new[r0:r1, :] = s
            m = jnp.maximum(m, jnp.max(s, axis=0, keepdims=True))
            for c in range(bounds[g], bounds[g + 1]):
                p = jnp.exp2(s_old[c * tk:(c + 1) * tk, :] - m_old)
                l = l + jnp.sum(p, axis=0, keepdims=True)
                acc = acc + jnp.dot(vt_ref[:, c * tk:(c + 1) * tk], p.astype(bf16), preferred_element_type=f32)
        m_new_ref[...] = jnp.broadcast_to(m, m_new_ref.shape)
        o_ref[0] = (acc * (1.0 / l)).T.astype(o_ref.dtype)

    @pl.when(i % 2 == 0)
    def _():
        step(sa_ref, sb_ref, ma_ref, mb_ref)

    @pl.when(i % 2 == 1)
    def _():
        step(sb_ref, sa_ref, mb_ref, ma_ref)


def _with_prev(kern, in_specs, args, prev):
    if prev is None:
        return kern, in_specs, args, {}
    n_in = len(args)

    def wrapped(*refs):
        return kern(*refs[:n_in], *refs[n_in + 1:])

    return wrapped, in_specs + [pl.BlockSpec(memory_space=pl.ANY)], args + [prev], {n_in: 0}


def attention(main, kv, krope, nb, lq, b_off, nb_total, prev):
    lk = kv.shape[1]
    tq = _tile(lq, 256)
    tk = _tile(lk, 256)
    nq = lq // tq
    qr_blk = OFF_QROPE // LANES
    nk = lk // tk
    nparts = min(4, nk)
    skewed = nq > 1
    if skewed:
        steps = nq + 1
        q_idx = lambda i: jnp.minimum(i, nq - 1)
        o_idx = lambda i: jnp.maximum(i - 1, 0)
        kern = functools.partial(_attn_kernel, nk=nk, tk=tk, nparts=nparts)
        scratch = [pltpu.VMEM((lk, MLA_NOPE + LANES), bf16), pltpu.VMEM((MLA_V, lk), bf16),
                   pltpu.VMEM((lk, tq), f32), pltpu.VMEM((lk, tq), f32),
                   pltpu.VMEM((8, tq), f32), pltpu.VMEM((8, tq), f32)]
    else:
        steps = nq
        q_idx = lambda i: i
        o_idx = lambda i: i
        kern = functools.partial(_attn_single_kernel, nk=nk, tk=tk, nparts=nparts)
        scratch = [pltpu.VMEM((lk, MLA_NOPE + LANES), bf16), pltpu.VMEM((MLA_V, lk), bf16),
                   pltpu.VMEM((lk, tq), f32)]
    in_specs = [pl.BlockSpec((1, tq, MLA_NOPE), lambda b, h, i: (b + b_off, q_idx(i), h)),
                pl.BlockSpec((1, tq, LANES), lambda b, h, i: (b + b_off, q_idx(i), qr_blk + h // 2)),
                pl.BlockSpec((1, lk, MLA_NOPE), lambda b, h, i: (b, 0, 2 * h)),
                pl.BlockSpec((1, lk, MLA_V), lambda b, h, i: (b, 0, 2 * h + 1)),
                pl.BlockSpec((1, lk, LANES), lambda b, h, i: (b, 0, 0))]
    kern, in_specs, args, aliases = _with_prev(kern, in_specs, [main, main, kv, kv, krope], prev)
    return pl.pallas_call(
        kern,
        out_shape=jax.ShapeDtypeStruct((nb_total, lq, MLA_HEADS * MLA_V), bf16),
        grid=(nb, MLA_HEADS, steps),
        in_specs=in_specs,
        out_specs=pl.BlockSpec((1, tq, MLA_V), lambda b, h, i: (b + b_off, o_idx(i), h)),
        scratch_shapes=scratch,
        input_output_aliases=aliases,
        compiler_params=_cparams("arbitrary", "arbitrary", "arbitrary"),
        name="attention",
    )(*args)


def _dft_ch_kernel(xa_ref, xb_ref, w_ref, o_ref):
    half = FNET_GROUPS // 2
    for g in range(FNET_GROUPS):
        x_ref = xa_ref if g < half else xb_ref
        lo = (g % half) * FNET_GC
        y = jnp.dot(x_ref[0, :, lo:lo + FNET_GC], w_ref[...], preferred_element_type=f32)
        o_ref[0, 0, :, g * FNET_GC:(g + 1) * FNET_GC] = y[:, :FNET_GC].astype(o_ref.dtype)
        o_ref[0, 1, :, g * FNET_GC:(g + 1) * FNET_GC] = y[:, FNET_GC:].astype(o_ref.dtype)


def dft_channels(main, w_ch, nb, ls, b_off):
    tm = _tile(ls, 512)
    wb = FNET_GROUPS * FNET_GC // 2
    f_blk = OFF_F // wb
    return pl.pallas_call(
        _dft_ch_kernel,
        out_shape=jax.ShapeDtypeStruct((nb, 2, ls, FNET_GROUPS * FNET_GC), bf16),
        grid=(nb, ls // tm),
        in_specs=[pl.BlockSpec((1, tm, wb), lambda b, i: (b + b_off, i, f_blk)),
                  pl.BlockSpec((1, tm, wb), lambda b, i: (b + b_off, i, f_blk + 1)),
                  pl.BlockSpec((FNET_GC, 2 * FNET_GC), lambda b, i: (0, 0))],
        out_specs=pl.BlockSpec((1, 2, tm, FNET_GROUPS * FNET_GC), lambda b, i: (b, 0, i, 0)),
        compiler_params=_cparams("arbitrary", "arbitrary"),
        name="dft_channels",
    )(main, main, w_ch)


def _dft_seq_kernel(l_ref, r_ref, o_ref):
    o_ref[0] = jnp.dot(l_ref[...], r_ref[0], preferred_element_type=f32).astype(o_ref.dtype)


def dft_sequence(cs, ab, b_off, nb_total, prev):
    nb, k, w = ab.shape
    ls = cs.shape[0]
    tm = _tile(ls, 512)
    tn = _tile(w, 512)
    in_specs = [pl.BlockSpec((tm, k), lambda b, j, i: (i, 0)),
                pl.BlockSpec((1, k, tn), lambda b, j, i: (b, 0, j))]
    kern, in_specs, args, aliases = _with_prev(_dft_seq_kernel, in_specs, [cs, ab], prev)
    return pl.pallas_call(
        kern,
        out_shape=jax.ShapeDtypeStruct((nb_total, ls, w), bf16),
        grid=(nb, w // tn, ls // tm),
        in_specs=in_specs,
        out_specs=pl.BlockSpec((1, tm, tn), lambda b, j, i: (b + b_off, i, j)),
        input_output_aliases=aliases,
        compiler_params=_cparams("arbitrary", "arbitrary", "arbitrary"),
        name="dft_sequence",
    )(*args)


def _dft_ch_sym_kernel(xa_ref, xb_ref, ma_ref, mb_ref, ea_ref, eb_ref, j_ref, w_ref, o_ref):
    half = FNET_GROUPS // 2
    tm = xa_ref.shape[1]
    first = lax.broadcasted_iota(jnp.int32, (tm, FNET_GC), 0) == 0
    for g in range(FNET_GROUPS):
        x_ref, m_ref, e_ref = (xa_ref, ma_ref, ea_ref) if g < half else (xb_ref, mb_ref, eb_ref)
        lo = (g % half) * FNET_GC
        x = x_ref[0, :, lo:lo + FNET_GC].astype(f32)
        r = jnp.dot(j_ref[...], m_ref[0, :, lo:lo + FNET_GC], preferred_element_type=f32)
        r = jnp.where(first, e_ref[0, 0:1, lo:lo + FNET_GC].astype(f32), r)
        cols = slice(g * FNET_GC, (g + 1) * FNET_GC)
        o_ref[0, 0, :, cols] = jnp.dot((x + r).astype(bf16), w_ref[:, :FNET_GC],
                                       preferred_element_type=f32).astype(o_ref.dtype)
        o_ref[0, 1, :, cols] = jnp.dot((x - r).astype(bf16), w_ref[:, FNET_GC:],
                                       preferred_element_type=f32).astype(o_ref.dtype)


def dft_channels_sym(main, w_ch, nb, ls, b_off):
    lh = ls // 2
    tm = _tile(lh, 512)
    nt = ls // tm
    wb = FNET_GROUPS * FNET_GC // 2
    f_blk = OFF_F // wb
    jj = jnp.arange(tm, dtype=jnp.int32)
    jmat = ((jj[:, None] + jj[None, :]) == tm).astype(bf16)
    eb = tm // 8
    x_spec = lambda c: pl.BlockSpec((1, tm, wb), lambda b, i: (b + b_off, i, f_blk + c))
    m_spec = lambda c: pl.BlockSpec((1, tm, wb), lambda b, i: (b + b_off, nt - 1 - i, f_blk + c))
    e_spec = lambda c: pl.BlockSpec((1, 8, wb), lambda b, i: (b + b_off, ((nt - i) % nt) * eb, f_blk + c))
    return pl.pallas_call(
        _dft_ch_sym_kernel,
        out_shape=jax.ShapeDtypeStruct((nb, 2, lh, FNET_GROUPS * FNET_GC), bf16),
        grid=(nb, lh // tm),
        in_specs=[x_spec(0), x_spec(1), m_spec(0), m_spec(1), e_spec(0), e_spec(1),
                  pl.BlockSpec((tm, tm), lambda b, i: (0, 0)),
                  pl.BlockSpec((FNET_GC, 2 * FNET_GC), lambda b, i: (0, 0))],
        out_specs=pl.BlockSpec((1, 2, tm, FNET_GROUPS * FNET_GC), lambda b, i: (b, 0, i, 0)),
        compiler_params=_cparams("arbitrary", "arbitrary"),
        name="dft_channels_sym",
    )(main, main, main, main, main, main, jmat, w_ch)


def _dft_seq_sym_kernel(l_ref, r_ref, a_ref, o_ref, *, mid_scale):
    y = jnp.dot(l_ref[...], r_ref[0], preferred_element_type=f32)
    rows = lax.broadcasted_iota(jnp.int32, y.shape, 0)
    sign = jnp.where(rows % 2 == 0, mid_scale, -mid_scale)
    o_ref[0] = (y + sign * a_ref[0, 0:1, :]).astype(o_ref.dtype)


def dft_sequence_sym(cs, ab, a_mid, b_off, nb_total, prev):
    nb, k, w = ab.shape
    ls = cs.shape[0]
    tm = _tile(ls, 512)
    tn = _tile(w, 512)
    in_specs = [pl.BlockSpec((tm, k), lambda b, j, i: (i, 0)),
                pl.BlockSpec((1, k, tn), lambda b, j, i: (b, 0, j)),
                pl.BlockSpec((1, 8, tn), lambda b, j, i: (b, 0, j))]
    kern, in_specs, args, aliases = _with_prev(functools.partial(_dft_seq_sym_kernel, mid_scale=ls ** -0.5),
                                               in_specs, [cs, ab, a_mid], prev)
    return pl.pallas_call(
        kern,
        out_shape=jax.ShapeDtypeStruct((nb_total, ls, w), bf16),
        grid=(nb, w // tn, ls // tm),
        in_specs=in_specs,
        out_specs=pl.BlockSpec((1, tm, tn), lambda b, j, i: (b + b_off, i, j)),
        input_output_aliases=aliases,
        compiler_params=_cparams("arbitrary", "arbitrary", "arbitrary"),
        name="dft_sequence_sym",
    )(*args)


def _dft_mats(n):
    idx = jnp.arange(n, dtype=jnp.int32)
    ang = ((idx[:, None] * idx[None, :]) % n).astype(f32) * (2.0 * math.pi / n)
    s = n ** -0.5
    return jnp.cos(ang) * s, jnp.sin(ang) * s


def _log_sigmoid(x):
    return jnp.minimum(x, 0.0) - jnp.log1p(jnp.exp(-jnp.abs(x)))


def _ret_kernel(*refs, nc, use_s0):
    if use_s0:
        dl_ref, q_ref, k_ref, v_ref, g_ref, gn_ref, s0_ref, o_ref, sfin_ref, of_ref, ob_ref = refs
    else:
        dl_ref, q_ref, k_ref, v_ref, g_ref, gn_ref, o_ref, sfin_ref, of_ref, ob_ref = refs
        s0_ref = None
    C = RET_CHUNK
    lgf = _log_sigmoid(dl_ref[0, 0])[0:1, :]
    lgb = _log_sigmoid(dl_ref[1, 0])[0:1, :]
    lgf2 = jnp.concatenate([lgf, lgf], axis=1)
    lgb2 = jnp.concatenate([lgb, lgb], axis=1)
    lgf_c = jnp.concatenate([lgf] * (C // LANES), axis=1)
    lgb_c = jnp.concatenate([lgb] * (C // LANES), axis=1)
    rows = lax.broadcasted_iota(jnp.int32, (C, C), 0)
    cols = lax.broadcasted_iota(jnp.int32, (C, C), 1)
    rel = (rows - cols).astype(f32)
    d_f = jnp.where(rel >= 0, jnp.exp(jnp.maximum(rel, 0.0) * lgf_c), 0.0)
    d_b = jnp.where(rel <= 0, jnp.exp(jnp.maximum(-rel, 0.0) * lgb_c), 0.0)
    r_v = lax.broadcasted_iota(jnp.int32, (C, RET_DV), 0).astype(f32)
    r_k = lax.broadcasted_iota(jnp.int32, (C, RET_DK), 0).astype(f32)
    qdec_f = jnp.exp((r_v + 1.0) * lgf2)
    qdec_b = jnp.exp((C - r_v) * lgb2)
    kdec_f = jnp.exp((C - 1.0 - r_k) * lgf)
    kdec_b = jnp.exp(r_k * lgb)
    cdec_f = jnp.exp(C * lgf2)
    cdec_b = jnp.exp(C * lgb2)

    def chunk(c, s, dmat, qdec, kdec, cdec):
        r = pl.multiple_of(c * C, C)
        q = q_ref[0, pl.ds(r, C), :]
        k = k_ref[0, pl.ds(r, C), :]
        v = v_ref[0, pl.ds(r, C), :]
        inner = lax.dot_general(q, k, (((1,), (1,)), ((), ())), preferred_element_type=f32) * dmat
        o = (jnp.dot(inner.astype(bf16), v, preferred_element_type=f32)
             + jnp.dot(q, s.astype(bf16), preferred_element_type=f32) * qdec)
        kd = (k.astype(f32) * kdec).T.astype(bf16)
        s_new = s * cdec + jnp.dot(kd, v, preferred_element_type=f32)
        return r, o, s_new

    def body(c, carry):
        sf, sb = carry
        r, o, sf = chunk(c, sf, d_f, qdec_f, kdec_f, cdec_f)
        of_ref[pl.ds(r, C), :] = o
        r, o, sb = chunk(nc - 1 - c, sb, d_b, qdec_b, kdec_b, cdec_b)
        ob_ref[pl.ds(r, C), :] = o
        return sf, sb

    if use_s0:
        init = (s0_ref[0, 0, 0], s0_ref[0, 1, 0])
    else:
        init = (jnp.zeros((RET_DK, RET_DV), f32), jnp.zeros((RET_DK, RET_DV), f32))
    sf, sb = lax.fori_loop(0, nc, body, init, unroll=2)
    sfin_ref[0, 0, 0] = sf
    sfin_ref[0, 1, 0] = sb

    gn = gn_ref[...]

    def norm_body(c, carry):
        r = pl.multiple_of(c * C, C)
        o = of_ref[pl.ds(r, C), :] + ob_ref[pl.ds(r, C), :]
        mu = jnp.mean(o, axis=-1, keepdims=True)
        d = o - mu
        y = d * lax.rsqrt(jnp.mean(d * d, axis=-1, keepdims=True) + GN_EPS) * gn
        o_ref[0, pl.ds(r, C), :] = (y * g_ref[0, pl.ds(r, C), :].astype(f32)).astype(o_ref.dtype)
        return carry

    lax.fori_loop(0, nc, norm_body, 0, unroll=2)


def retention(main, decay_logit, gn_g, s0, nb, ls, b_off, nb_total, prev):
    use_s0 = s0 is not None
    nc = ls // RET_CHUNK
    dl = jnp.broadcast_to(decay_logit.astype(f32)[:, :, None, None], (2, RET_HEADS, 8, LANES))
    in_specs = [pl.BlockSpec((2, 1, 8, LANES), lambda b, h: (0, h, 0, 0)),
                pl.BlockSpec((1, ls, RET_DK), lambda b, h: (b + b_off, 0, OFF_RQ // RET_DK + h)),
                pl.BlockSpec((1, ls, RET_DK), lambda b, h: (b + b_off, 0, OFF_RK // RET_DK + h)),
                pl.BlockSpec((1, ls, RET_DV), lambda b, h: (b + b_off, 0, OFF_RV // RET_DV + h)),
                pl.BlockSpec((1, ls, RET_DV), lambda b, h: (b + b_off, 0, OFF_RG // RET_DV + h)),
                pl.BlockSpec((1, RET_DV), lambda b, h: (0, h))]
    args = [dl, main, main, main, main, gn_g.reshape(1, RET_HEADS * RET_DV)]
    if use_s0:
        in_specs.append(pl.BlockSpec((1, 2, 1, RET_DK, RET_DV), lambda b, h: (b, 0, h, 0, 0)))
        args.append(s0)
    kern, in_specs, args, aliases = _with_prev(functools.partial(_ret_kernel, nc=nc, use_s0=use_s0),
                                               in_specs, args, prev)
    return pl.pallas_call(
        kern,
        out_shape=(jax.ShapeDtypeStruct((nb_total, ls, RET_HEADS * RET_DV), bf16),
                   jax.ShapeDtypeStruct((nb, 2, RET_HEADS, RET_DK, RET_DV), f32)),
        grid=(nb, RET_HEADS),
        in_specs=in_specs,
        out_specs=(pl.BlockSpec((1, ls, RET_DV), lambda b, h: (b + b_off, 0, h)),
                   pl.BlockSpec((1, 2, 1, RET_DK, RET_DV), lambda b, h: (b, 0, h, 0, 0))),
        scratch_shapes=[pltpu.VMEM((ls, RET_DV), f32), pltpu.VMEM((ls, RET_DV), f32)],
        input_output_aliases=aliases,
        compiler_params=_cparams("arbitrary", "arbitrary"),
        name="retention",
    )(*args)


def _merge_kernel(a_ref, f_ref, r_ref, wa_ref, wf_ref, wr_ref, *rest):
    gates, o_ref = rest[:6], rest[6]
    hw = o_ref.shape[1] // 2
    for c in range(2):
        cols = slice(c * hw, (c + 1) * hw)
        m = gates[c][...].astype(f32) * jnp.dot(a_ref[...], wa_ref[:, cols], preferred_element_type=f32)
        m += gates[2 + c][...].astype(f32) * jnp.dot(f_ref[...], wf_ref[:, cols], preferred_element_type=f32)
        m += gates[4 + c][...].astype(f32) * jnp.dot(r_ref[...], wr_ref[:, cols], preferred_element_type=f32)
        o_ref[:, cols] = m.astype(o_ref.dtype)


def merge_branches(att, four, ret, wa, wf, wr, main2d):
    m, d = att.shape
    tm = _tile(m, 256)
    hw = d // 2
    gb = OFF_GATES // hw
    x_spec = pl.BlockSpec((tm, d), lambda i: (i, 0))
    w_spec = pl.BlockSpec((d, d), lambda i: (0, 0), pipeline_mode=pl.Buffered(1))
    g_specs = [pl.BlockSpec((tm, hw), functools.partial(lambda i, k: (i, gb + k), k=k)) for k in range(6)]
    return pl.pallas_call(
        _merge_kernel,
        out_shape=jax.ShapeDtypeStruct((m, d), bf16),
        grid=(m // tm,),
        in_specs=[x_spec, x_spec, x_spec, w_spec, w_spec, w_spec] + g_specs,
        out_specs=pl.BlockSpec((tm, d), lambda i: (i, 0)),
        compiler_params=_cparams("arbitrary"),
        name="merge_branches",
    )(att, four, ret, wa, wf, wr, *([main2d] * 6))


def _proj_res_norm_kernel(m_ref, w_ref, x_ref, g_ref, ng_ref, sc_ref, sh_ref, o_ref, h_ref, *, nchunk):
    rc = m_ref.shape[1] // nchunk

    def finish(rows, y):
        xn = x_ref[0, rows, :] + g_ref[0] * y
        o_ref[0, rows, :] = xn
        z = xn * lax.rsqrt(jnp.mean(xn * xn, axis=-1, keepdims=True) + EPS)
        z = z * ng_ref[...]
        h_ref[0, rows, :] = (z * (1.0 + sc_ref[0]) + sh_ref[0]).astype(h_ref.dtype)

    pending = None
    for r in range(nchunk):
        rows = slice(r * rc, (r + 1) * rc)
        y = jnp.dot(m_ref[0, rows, :], w_ref[...], preferred_element_type=f32)
        if pending is not None:
            finish(*pending)
        pending = (rows, y)
    finish(*pending)


def proj_residual_norm(merged, w, x, gate, norm_g, sc, sh, h_dtype):
    nb, lr, d = x.shape
    tm = _tile(lr, 512)
    row_spec = pl.BlockSpec((1, tm, d), lambda b, i: (b, i, 0))
    mod_spec = pl.BlockSpec((1, 1, d), lambda b, i: (b, 0, 0))
    return pl.pallas_call(
        functools.partial(_proj_res_norm_kernel, nchunk=2),
        out_shape=(jax.ShapeDtypeStruct((nb, lr, d), f32), jax.ShapeDtypeStruct((nb, lr, d), h_dtype)),
        grid=(nb, lr // tm),
        in_specs=[row_spec,
                  pl.BlockSpec((d, d), lambda b, i: (0, 0), pipeline_mode=pl.Buffered(1)),
                  row_spec, mod_spec,
                  pl.BlockSpec((1, d), lambda b, i: (0, 0)),
                  mod_spec, mod_spec],
        out_specs=(row_spec, row_spec),
        compiler_params=_cparams("arbitrary", "arbitrary"),
        name="proj_residual_norm",
    )(merged, w, x, gate, norm_g.reshape(1, d), sc, sh)


def _ffn_kernel(h_ref, wg_ref, wu_ref, wd_ref, x_ref, g_ref, o_ref, acc_ref):
    j = pl.program_id(2)

    @pl.when(j == 0)
    def _():
        acc_ref[...] = jnp.zeros_like(acc_ref)

    h = h_ref[0]
    a = jnp.dot(h, wg_ref[...], preferred_element_type=f32)
    u = jnp.dot(h, wu_ref[...], preferred_element_type=f32)
    t = (_silu(a) * u).astype(bf16)
    acc_ref[...] += jnp.dot(t, wd_ref[...], preferred_element_type=f32)

    @pl.when(j == pl.num_programs(2) - 1)
    def _():
        o_ref[0] = x_ref[0] + g_ref[0] * acc_ref[...]


def ffn_dense(h, wg, wu, wd, x, gate):
    nb, lr, d = x.shape
    dff = wg.shape[1]
    tm = _tile(lr, 512)
    tf = _tile(dff, 512)
    return pl.pallas_call(
        _ffn_kernel,
        out_shape=jax.ShapeDtypeStruct((nb, lr, d), f32),
        grid=(nb, lr // tm, dff // tf),
        in_specs=[pl.BlockSpec((1, tm, d), lambda b, i, j: (b, i, 0)),
                  pl.BlockSpec((d, tf), lambda b, i, j: (0, j)),
                  pl.BlockSpec((d, tf), lambda b, i, j: (0, j)),
                  pl.BlockSpec((tf, d), lambda b, i, j: (j, 0)),
                  pl.BlockSpec((1, tm, d), lambda b, i, j: (b, i, 0)),
                  pl.BlockSpec((1, 1, d), lambda b, i, j: (b, 0, 0))],
        out_specs=pl.BlockSpec((1, tm, d), lambda b, i, j: (b, i, 0)),
        scratch_shapes=[pltpu.VMEM((tm, d), f32)],
        compiler_params=_cparams("arbitrary", "arbitrary", "arbitrary"),
        name="ffn_dense",
    )(h, wg, wu, wd, x, gate)


MOE_TM = 512
MOE_TT = 256


def _router_kernel(h_ref, w_ref, e_ref, p_ref):
    logits = jnp.dot(h_ref[...].astype(bf16), w_ref[...], preferred_element_type=f32)
    lane = lax.broadcasted_iota(jnp.int32, logits.shape, 1)
    lane_f = lane.astype(f32)
    neg = jnp.full_like(logits, -jnp.inf)
    l0 = jnp.where(lane < N_EXPERTS, logits, neg)
    v1 = jnp.max(l0, axis=1, keepdims=True)
    i1 = jnp.min(jnp.where(l0 == v1, lane_f, float(LANES)), axis=1, keepdims=True)
    l1 = jnp.where(lane_f == i1, neg, l0)
    v2 = jnp.max(l1, axis=1, keepdims=True)
    i2 = jnp.min(jnp.where(l1 == v2, lane_f, float(LANES)), axis=1, keepdims=True)
    e = jnp.exp(v2 - v1)
    den = 1.0 + e
    e_ref[...] = jnp.where(lane == 0, i1, jnp.where(lane == 1, i2, 0.0)).astype(jnp.int32)
    p_ref[...] = jnp.where(lane == 0, 1.0 / den, jnp.where(lane == 1, e / den, 0.0))


def moe_router(h2d, w_router_pad):
    n, d = h2d.shape
    tm = _tile(n, 512)
    return pl.pallas_call(
        _router_kernel,
        out_shape=(jax.ShapeDtypeStruct((n, LANES), jnp.int32), jax.ShapeDtypeStruct((n, LANES), f32)),
        grid=(n // tm,),
        in_specs=[pl.BlockSpec((tm, d), lambda i: (i, 0)),
                  pl.BlockSpec((d, LANES), lambda i: (0, 0))],
        out_specs=(pl.BlockSpec((tm, LANES), lambda i: (i, 0)), pl.BlockSpec((tm, LANES), lambda i: (i, 0))),
        compiler_params=_cparams("arbitrary"),
        name="moe_router",
    )(h2d, w_router_pad)


def _dispatch_kernel(pos_ref, h_ref, xs_in_ref, xs_ref, sem):
    del xs_in_ref
    tt = h_ref.shape[0]

    def row_copy(t, p):
        return pltpu.make_async_copy(h_ref.at[pl.ds(t, 1), :], xs_ref.at[pl.ds(p, 1), :], sem)

    def start(t, c):
        row_copy(t, pos_ref[0, 0, 2 * t]).start()
        row_copy(t, pos_ref[0, 0, 2 * t + 1]).start()
        return c

    lax.fori_loop(0, tt, start, 0, unroll=8)
    tile_copy = pltpu.make_async_copy(h_ref, xs_ref.at[pl.ds(0, tt), :], sem)
    tile_copy.wait()
    tile_copy.wait()


def moe_dispatch(h2d, pos, p_rows):
    n, d = h2d.shape
    tt = _tile(n, MOE_TT)
    pos3 = pos.reshape(n // tt, 1, 2 * tt)
    xs0 = jnp.zeros((p_rows, d), h2d.dtype)
    return pl.pallas_call(
        _dispatch_kernel,
        out_shape=jax.ShapeDtypeStruct((p_rows, d), h2d.dtype),
        grid=(n // tt,),
        in_specs=[pl.BlockSpec((1, 1, 2 * tt), lambda i: (i, 0, 0), memory_space=pltpu.SMEM),
                  pl.BlockSpec((tt, d), lambda i: (i, 0)),
                  pl.BlockSpec(memory_space=pl.ANY)],
        out_specs=pl.BlockSpec(memory_space=pl.ANY),
        scratch_shapes=[pltpu.SemaphoreType.DMA(())],
        input_output_aliases={2: 0},
        compiler_params=_cparams("arbitrary"),
        name="moe_dispatch",
    )(pos3, h2d, xs0)


def _moe_ffn_kernel(be_ref, nu_ref, x_ref, wg_ref, wu_ref, wd_ref, y_ref, xb_ref, acc_ref):
    del be_ref
    i = pl.program_id(0)
    j = pl.program_id(1)
    active = i < nu_ref[0]

    @pl.when(jnp.logical_and(active, j == 0))
    def _():
        xb_ref[...] = x_ref[...].astype(bf16)
        acc_ref[...] = jnp.zeros_like(acc_ref)

    @pl.when(active)
    def _():
        xb = xb_ref[...]
        a = jnp.dot(xb, wg_ref[0], preferred_element_type=f32)
        u = jnp.dot(xb, wu_ref[0], preferred_element_type=f32)
        t = (_silu(a) * u).astype(bf16)
        acc_ref[...] += jnp.dot(t, wd_ref[0], preferred_element_type=f32)

    @pl.when(j == pl.num_programs(1) - 1)
    def _():
        y_ref[...] = jnp.where(active, acc_ref[...], 0.0)


def moe_ffn(xs, blk_expert, n_used, wg, wu, wd):
    p_rows, d = xs.shape
    dff = wg.shape[2]
    tm = MOE_TM
    tf = _tile(dff, 512)
    nf = dff // tf

    def jj(i, j, nu):
        return jnp.where(i < nu[0], j, nf - 1)

    grid_spec = pltpu.PrefetchScalarGridSpec(
        num_scalar_prefetch=2,
        grid=(p_rows // tm, nf),
        in_specs=[pl.BlockSpec((tm, d), lambda i, j, be, nu: (i, 0)),
                  pl.BlockSpec((1, d, tf), lambda i, j, be, nu: (be[i], 0, jj(i, j, nu))),
                  pl.BlockSpec((1, d, tf), lambda i, j, be, nu: (be[i], 0, jj(i, j, nu))),
                  pl.BlockSpec((1, tf, d), lambda i, j, be, nu: (be[i], jj(i, j, nu), 0))],
        out_specs=pl.BlockSpec((tm, d), lambda i, j, be, nu: (i, 0)),
        scratch_shapes=[pltpu.VMEM((tm, d), bf16), pltpu.VMEM((tm, d), f32)],
    )
    return pl.pallas_call(
        _moe_ffn_kernel,
        out_shape=jax.ShapeDtypeStruct((p_rows, d), f32),
        grid_spec=grid_spec,
        compiler_params=_cparams("arbitrary", "arbitrary"),
        name="moe_ffn",
    )(blk_expert, n_used, xs, wg, wu, wd)


def _combine_kernel(pos_ref, y_ref, p_ref, x_ref, g_ref, o_ref, ybuf_ref, sem):
    tt = x_ref.shape[1]

    def row_copy(p, dst):
        return pltpu.make_async_copy(y_ref.at[pl.ds(p, 1), :], ybuf_ref.at[pl.ds(dst, 1), :], sem)

    def start(t, c):
        row_copy(pos_ref[0, 0, 2 * t], t).start()
        row_copy(pos_ref[0, 0, 2 * t + 1], tt + t).start()
        return c

    lax.fori_loop(0, tt, start, 0, unroll=8)
    pltpu.make_async_copy(y_ref.at[pl.ds(0, 2 * tt), :], ybuf_ref, sem).wait()
    p = p_ref[0]
    mix = p[:, 0:1] * ybuf_ref[pl.ds(0, tt), :] + p[:, 1:2] * ybuf_ref[pl.ds(tt, tt), :]
    o_ref[0] = x_ref[0] + g_ref[0] * mix


def moe_combine(y, pos, probs, x, gate):
    nb, lr, d = x.shape
    tt = _tile(lr, MOE_TT)
    nt = lr // tt
    pos3 = pos.reshape(nb * nt, 1, 2 * tt)
    probs3 = probs.reshape(nb, lr, LANES)
    return pl.pallas_call(
        _combine_kernel,
        out_shape=jax.ShapeDtypeStruct((nb, lr, d), f32),
        grid=(nb, nt),
        in_specs=[pl.BlockSpec((1, 1, 2 * tt), lambda b, i: (b * nt + i, 0, 0), memory_space=pltpu.SMEM),
                  pl.BlockSpec(memory_space=pl.ANY),
                  pl.BlockSpec((1, tt, LANES), lambda b, i: (b, i, 0)),
                  pl.BlockSpec((1, tt, d), lambda b, i: (b, i, 0)),
                  pl.BlockSpec((1, 1, d), lambda b, i: (b, 0, 0))],
        out_specs=pl.BlockSpec((1, tt, d), lambda b, i: (b, i, 0)),
        scratch_shapes=[pltpu.VMEM((2 * tt, d), f32), pltpu.SemaphoreType.DMA(())],
        compiler_params=_cparams("arbitrary", "arbitrary"),
        name="moe_combine",
    )(pos3, y, probs3, x, gate)


def moe_layer(h, x, gate, w_router, wg, wu, wd):
    nb, lr, d = h.shape
    n = nb * lr
    h2d = h.reshape(n, d)
    w_r = jnp.zeros((d, LANES), bf16).at[:, :N_EXPERTS].set(w_router.astype(bf16))
    e_out, p_out = moe_router(h2d, w_r)
    e_flat = e_out[:, :2].reshape(-1)
    onehot = (e_flat[:, None] == jnp.arange(N_EXPERTS, dtype=jnp.int32)[None, :]).astype(jnp.int32)
    csum = jnp.cumsum(onehot, axis=0)
    counts = csum[-1]
    rank = jnp.sum(onehot * csum, axis=1) - 1
    padded = (counts + MOE_TM - 1) // MOE_TM * MOE_TM
    pad_ends = jnp.cumsum(padded)
    pad_starts = pad_ends - padded
    pos = jnp.sum(onehot * pad_starts[None, :], axis=1) + rank
    n_blocks = (2 * n) // MOE_TM + N_EXPERTS
    blk_start = jnp.arange(n_blocks, dtype=jnp.int32) * MOE_TM
    blk_expert = jnp.minimum(jnp.sum((blk_start[:, None] >= pad_ends[None, :]).astype(jnp.int32), axis=1),
                             N_EXPERTS - 1).astype(jnp.int32)
    n_used = (pad_ends[-1] // MOE_TM).astype(jnp.int32).reshape(1)
    xs = moe_dispatch(h2d, pos.astype(jnp.int32), n_blocks * MOE_TM)
    y = moe_ffn(xs, blk_expert, n_used, wg, wu, wd)
    return moe_combine(y, pos.astype(jnp.int32), p_out, x, gate)


def _rope_tables(n_tok):
    rows = n_tok // GRID_W
    r = jnp.repeat(jnp.arange(rows, dtype=f32), GRID_W)
    cl = jnp.tile(jnp.arange(GRID_W, dtype=f32), rows)
    n_freq = MLA_ROPE // 4
    inv = ROPE_BASE ** (-jnp.arange(n_freq, dtype=f32) / n_freq)
    ang = jnp.concatenate([r[:, None] * inv, cl[:, None] * inv], axis=-1)
    cos, sin = jnp.cos(ang), jnp.sin(ang)
    return jnp.tile(cos, (1, 4)), jnp.concatenate([-sin, sin, -sin, sin], axis=-1)


def _prep_in_weights(w_in_l):
    d = w_in_l.shape[0]
    nq = MLA_HEADS * (MLA_NOPE + MLA_ROPE)
    wq = w_in_l[:, :nq].reshape(d, MLA_HEADS, MLA_NOPE + MLA_ROPE)
    w_ckv = w_in_l[:, nq:nq + KV_LORA]
    w_kr = w_in_l[:, nq + KV_LORA:nq + KV_LORA + MLA_ROPE]
    rest = w_in_l[:, nq + KV_LORA + MLA_ROPE:]
    w_main = jnp.concatenate([wq[:, :, :MLA_NOPE].reshape(d, Q_NOPE_W), wq[:, :, MLA_NOPE:].reshape(d, Q_ROPE_W),
                              rest], axis=1).astype(bf16)
    w_ckr = jnp.concatenate([w_ckv, w_kr, w_kr], axis=1).astype(bf16)
    return w_main, w_ckr


def kernel(x_prompt, x_sample, c, cache_ckv, cache_krope, state_ret, c_ctx, w_ada, b_ada, norm1_g, w_in, kv_norm_g, w_ukv, ret_decay_logit, ret_gn_g, w_o_mla, w_o_fnet, w_o_ret, w_out, norm2_g, w_gate_dense, w_up_dense, w_down_dense, w_router, w_gate_moe, w_up_moe, w_down_moe, final_norm_g):
    nbc, seq, d = x_prompt.shape
    nbs, ls, _ = x_sample.shape
    depth = w_in.shape[0]
    lr = nbc * seq
    assert lr == ls, "unified row layout needs BATCH * SEQ == DEC_SEQ"
    nb = 1 + nbs
    n_rows = nb * lr

    x = jnp.concatenate([x_prompt.reshape(1, lr, d), x_sample], axis=0)
    cond = jnp.concatenate([c_ctx[None, :], c, jnp.zeros((16 - nb, d), f32)], axis=0)
    mod = adaln_all(cond, w_ada, b_ada)[:, :nb]
    cos128, sin128 = _rope_tables(ls)

    w_ch = jnp.concatenate(_dft_mats(FNET_GC), axis=1).astype(bf16)
    c_c, s_c = _dft_mats(seq)
    cs_ctx = jnp.concatenate([c_c, -s_c], axis=1).astype(bf16)
    c_s, s_s = _dft_mats(ls)
    lh = ls // 2
    c_half = c_s[:, :lh] * jnp.where(jnp.arange(lh) == 0, 0.5, 1.0)[None, :]
    cs_lat = jnp.concatenate([c_half, -s_s[:, :lh]], axis=1).astype(bf16)

    ckv_list, krope_list, ret_list = [], [], []
    spare = []
    for l in range(depth):
        sh1, sc1, g1, sh2, sc2, g2 = [m.reshape(nb, 1, d) for m in jnp.split(mod[l], 6, axis=-1)]
        w_main, w_ckr = _prep_in_weights(w_in[l])
        h = norm_mod(x, norm1_g[l], sc1, sh1, bf16)
        main = inproj_main(h, w_main, cos128, sin128)
        ckv, kr, krr = inproj_ckr(h, w_ckr, kv_norm_g[l], cos128, sin128)
        w_ukv_b = w_ukv[l].astype(bf16)

        ckv_c = ckv[0].reshape(nbc, seq, KV_LORA)
        kr_c = kr[0].reshape(nbc, seq, LANES)
        kv_c = matmul(ckv_c.reshape(lr, KV_LORA).astype(bf16), w_ukv_b, bf16).reshape(nbc, seq, -1)
        main_c = main.reshape(nb * nbc, seq, N_MAIN)
        bufs = [h] + spare
        bufs += [jnp.zeros((nb, lr, d), bf16) for _ in range(3 - len(bufs))]
        bufs = [bb.reshape(nb * nbc, seq, d) for bb in bufs[:3]]
        att_c = attention(main_c, kv_c, kr_c.astype(bf16), nbc, seq, 0, nb * nbc, bufs[0])
        ab_c = dft_channels(main_c, w_ch, nbc, seq, 0).reshape(nbc, 2 * seq, FNET_GROUPS * FNET_GC)
        four_c = dft_sequence(cs_ctx, ab_c, 0, nb * nbc, bufs[1])
        ret_c, sfin_c = retention(main_c, ret_decay_logit[l], ret_gn_g[l], None, nbc, seq, 0, nb * nbc, bufs[2])
        ckv_list.append(ckv_c)
        krope_list.append(kr_c[:, :, :MLA_ROPE])
        ret_list.append(sfin_c)

        ckv_keys = jnp.concatenate([ckv[1:], cache_ckv[:, l]], axis=1).astype(bf16)
        lk = ckv_keys.shape[1]
        kv_s = matmul(ckv_keys.reshape(nbs * lk, KV_LORA), w_ukv_b, bf16).reshape(nbs, lk, -1)
        ck = cache_krope[:, l].astype(bf16)
        kr_keys = jnp.concatenate([krr[1:], jnp.concatenate([ck, ck], axis=-1)], axis=1)
        att = attention(main, kv_s, kr_keys, nbs, ls, 1, nb, att_c.reshape(nb, ls, -1)).reshape(n_rows, -1)
        ab_s = dft_channels_sym(main, w_ch, nbs, ls, 1).reshape(nbs, ls, FNET_GROUPS * FNET_GC)
        f_mid = main[1:, lh:lh + 8, OFF_F:OFF_F + FNET_GROUPS * FNET_GC]
        a_mid = matmul(f_mid.reshape(nbs * 8 * FNET_GROUPS, FNET_GC), w_ch[:, :FNET_GC], f32)
        a_mid = a_mid.reshape(nbs, 8, FNET_GROUPS * FNET_GC)
        four = dft_sequence_sym(cs_lat, ab_s, a_mid, 1, nb, four_c.reshape(nb, ls, -1)).reshape(n_rows, -1)
        ret, _ = retention(main, ret_decay_logit[l], ret_gn_g[l], state_ret[:, l], nbs, ls, 1, nb,
                           ret_c.reshape(nb, ls, -1))
        ret = ret.reshape(n_rows, -1)
        spare = [att, four]

        merged = merge_branches(att, four, ret, w_o_mla[l].astype(bf16), w_o_fnet[l].astype(bf16),
                                w_o_ret[l].astype(bf16), main.reshape(n_rows, N_MAIN))
        x, h2 = proj_residual_norm(merged.reshape(nb, lr, d), w_out[l].astype(bf16), x, g1, norm2_g[l], sc2, sh2,
                                   bf16 if l % 2 == 0 else f32)

        i = l // 2
        if l % 2 == 0:
            x = ffn_dense(h2, w_gate_dense[i].astype(bf16), w_up_dense[i].astype(bf16),
                          w_down_dense[i].astype(bf16), x, g2)
        else:
            x = moe_layer(h2, x, g2, w_router[i], w_gate_moe[i].astype(bf16), w_up_moe[i].astype(bf16),
                          w_down_moe[i].astype(bf16))

    y_prompt = final_norm(x, final_norm_g, 0, 1).reshape(nbc, seq, d)
    y_sample = final_norm(x, final_norm_g, 1, nbs)
    new_ckv = jnp.stack(ckv_list, axis=1)
    new_krope = jnp.stack(krope_list, axis=1)
    new_ret = jnp.stack(ret_list, axis=1)
    return (y_prompt, y_sample, new_ckv, new_krope, new_ret)
```
